```python
import jax, jax.numpy as jnp
from jax import lax
import numpy as np

D_MODEL = 2048
BATCH = 4
SEQ = 2048
DEPTH = 2

D_MIX = D_MODEL
EPS = 1e-6
SWA_HEAD_DIM = 64
SWA_HEADS = D_MIX // 2 // SWA_HEAD_DIM
SWA_KV_HEADS = 4
SWA_GROUP = SWA_HEADS // SWA_KV_HEADS
SWA_WIDTH = SWA_HEADS * SWA_HEAD_DIM
SWA_KV_WIDTH = SWA_KV_HEADS * SWA_HEAD_DIM
WINDOW = 128
ROT_DIM = SWA_HEAD_DIM // 4
ROPE_THETA = 500000.0
SG_WIDTH = D_MIX // 4
SG_GROUPS = 8
SG_GROUP_DIM = SG_WIDTH // SG_GROUPS
SG_CHUNK = 128
GLA_HEADS = 4
GLA_WIDTH = D_MIX // 4
GLA_DV = GLA_WIDTH // GLA_HEADS
GLA_DK = GLA_DV // 2
GLA_KEY_WIDTH = GLA_HEADS * GLA_DK
GLA_GATE_RANK = 16
GLA_GATE_TAU = 16.0
GLA_CHUNK = 64
IN_SIZES = (SWA_WIDTH, SWA_KV_WIDTH, SWA_KV_WIDTH,
            SG_WIDTH, SG_WIDTH,
            GLA_KEY_WIDTH, GLA_KEY_WIDTH, GLA_WIDTH,
            GLA_GATE_RANK,
            D_MIX)
IN_PROJ_WIDTH = SWA_WIDTH + 2 * SWA_KV_WIDTH + 2 * SG_WIDTH + 2 * GLA_KEY_WIDTH + GLA_WIDTH + GLA_GATE_RANK + D_MIX

kernel_name = 'hybrid_swa_sgmlp_gla_parallel_heads'


def rms_norm(x, g):
    x32 = x.astype(jnp.float32)
    y = x32 * lax.rsqrt(jnp.mean(x32 * x32, axis=-1, keepdims=True) + EPS)
    return (y * g.astype(jnp.float32)).astype(x.dtype)


def partial_rope(t, positions):
    half = ROT_DIM // 2
    inv_freq = ROPE_THETA ** (-(jnp.arange(half, dtype=jnp.float32) * (2.0 / ROT_DIM)))
    ang = positions.astype(jnp.float32)[..., None] * inv_freq
    cos = jnp.cos(ang)[:, :, None, :]
    sin = jnp.sin(ang)[:, :, None, :]
    tr = t[..., :ROT_DIM].astype(jnp.float32)
    t1, t2 = tr[..., :half], tr[..., half:]
    rot = jnp.concatenate([t1 * cos - t2 * sin, t2 * cos + t1 * sin], axis=-1)
    return jnp.concatenate([rot.astype(t.dtype), t[..., ROT_DIM:]], axis=-1)


def sliding_window_attention(q, k, v, sinks):
    bsz, seq = q.shape[0], q.shape[1]
    nb = seq // WINDOW
    qb = q.reshape(bsz, nb, WINDOW, SWA_KV_HEADS, SWA_GROUP, SWA_HEAD_DIM).astype(jnp.float32)

    def band(t):
        tb = t.reshape(bsz, nb, WINDOW, SWA_KV_HEADS, SWA_HEAD_DIM)
        prev = jnp.pad(tb[:, :-1], ((0, 0), (1, 0), (0, 0), (0, 0), (0, 0)))
        return jnp.concatenate([prev, tb], axis=2).astype(jnp.float32)

    kb, vb = band(k), band(v)
    scores = jnp.einsum('bnqgrd,bnkgd->bngrqk', qb, kb) * (SWA_HEAD_DIM ** -0.5)
    qi = jnp.arange(WINDOW)[:, None]
    kj = jnp.arange(2 * WINDOW)[None, :]
    dist = qi + WINDOW - kj
    blk = jnp.arange(nb)[:, None, None]
    valid = (dist >= 0) & (dist < WINDOW) & (blk * WINDOW + kj[None] - WINDOW >= 0)
    scores = jnp.where(valid[None, :, None, None], scores, -jnp.inf)
    sink = sinks.astype(jnp.float32).reshape(1, 1, SWA_KV_HEADS, SWA_GROUP, 1, 1)
    m = jnp.maximum(scores.max(axis=-1, keepdims=True), sink)
    p = jnp.exp(scores - m)
    probs = p / (p.sum(axis=-1, keepdims=True) + jnp.exp(sink - m))
    out = jnp.einsum('bngrqk,bnkgd->bnqgrd', probs, vb)
    return out.reshape(bsz, seq, SWA_WIDTH).astype(q.dtype)


def chunked_spatial_gating(u, v, w_s, b_s, ln_g, ln_b):
    bsz, seq = v.shape[0], v.shape[1]
    nc = seq // SG_CHUNK
    v32 = v.astype(jnp.float32)
    mu = jnp.mean(v32, axis=-1, keepdims=True)
    var = jnp.mean(jnp.square(v32 - mu), axis=-1, keepdims=True)
    vn = (v32 - mu) * lax.rsqrt(var + EPS) * ln_g.astype(jnp.float32) + ln_b.astype(jnp.float32)
    vn = vn.reshape(bsz, nc, SG_CHUNK, SG_GROUPS, SG_GROUP_DIM)
    causal = jnp.tril(jnp.ones((SG_CHUNK, SG_CHUNK), dtype=bool))
    w = jnp.where(causal[None], w_s.astype(jnp.float32), 0.0)
    mixed = jnp.einsum('gts,bnsgc->bntgc', w, vn) + b_s.astype(jnp.float32).T[None, None, :, :, None]
    return (u.astype(jnp.float32) * mixed.reshape(bsz, seq, SG_WIDTH)).astype(u.dtype)


def gated_linear_attention(q, k, v, log_alpha):
    bsz, seq = q.shape[0], q.shape[1]
    nc = seq // GLA_CHUNK

    def to_chunks(t):
        return t.astype(jnp.float32).reshape(bsz, nc, GLA_CHUNK, GLA_HEADS, t.shape[-1]).transpose(1, 0, 3, 2, 4)

    causal = jnp.tril(jnp.ones((GLA_CHUNK, GLA_CHUNK), dtype=bool))

    def step(state, xs):
        qc, kc, vc, lac = xs
        b = jnp.cumsum(lac, axis=2)
        diff = b[:, :, :, None, :] - b[:, :, None, :, :]
        decay = jnp.exp(jnp.where(causal[:, :, None], diff, -jnp.inf))
        scores = jnp.einsum('bhtd,bhsd,bhtsd->bhts', qc, kc, decay)
        o = jnp.einsum('bhts,bhsv->bhtv', scores, vc) + jnp.einsum('bhtd,bhdv->bhtv', qc * jnp.exp(b), state)
        b_last = b[:, :, -1:, :]
        state = state * jnp.exp(b_last)[:, :, 0, :, None] + jnp.einsum('bhsd,bhsv->bhdv', kc * jnp.exp(b_last - b), vc)
        return state, o

    state0 = jnp.zeros((bsz, GLA_HEADS, GLA_DK, GLA_DV), dtype=jnp.float32)
    qs = to_chunks(q) * (GLA_DK ** -0.5)
    _, o = lax.scan(step, state0, (qs, to_chunks(k), to_chunks(v), to_chunks(log_alpha)))
    return o.transpose(1, 0, 3, 2, 4).reshape(bsz, seq, GLA_HEADS, GLA_DV)


def hybrid_layer(x, c, positions, w_mod, b_mod, g_pre, g_post, w_in, w_out, swa_sinks,
                 sg_w, sg_b, sg_ln_g, sg_ln_b, gla_w_gate_up, gla_b_gate, gla_norm_g):
    bsz, seq = x.shape[0], x.shape[1]
    mod = jax.nn.silu(c) @ w_mod + b_mod
    shift, scale, gate = jnp.split(mod, 3, axis=-1)
    h = rms_norm(x, g_pre) * (1.0 + scale[:, None, :]) + shift[:, None, :]
    proj = h @ w_in
    offsets = np.cumsum(IN_SIZES)[:-1].tolist()
    a_q, a_k, a_v, s_u, s_v, c_q, c_k, c_v, c_g, z = jnp.split(proj, offsets, axis=-1)
    a_q = partial_rope(a_q.reshape(bsz, seq, SWA_HEADS, SWA_HEAD_DIM), positions)
    a_k = partial_rope(a_k.reshape(bsz, seq, SWA_KV_HEADS, SWA_HEAD_DIM), positions)
    a_v = a_v.reshape(bsz, seq, SWA_KV_HEADS, SWA_HEAD_DIM)
    y_a = sliding_window_attention(a_q, a_k, a_v, swa_sinks)
    y_b = chunked_spatial_gating(jax.nn.gelu(s_u), jax.nn.gelu(s_v), sg_w, sg_b, sg_ln_g, sg_ln_b)
    gate_logits = (c_g @ gla_w_gate_up + gla_b_gate).astype(jnp.float32)
    log_alpha = jax.nn.log_sigmoid(gate_logits) / GLA_GATE_TAU
    o_c = gated_linear_attention(c_q.reshape(bsz, seq, GLA_HEADS, GLA_DK),
                                 c_k.reshape(bsz, seq, GLA_HEADS, GLA_DK),
                                 c_v.reshape(bsz, seq, GLA_HEADS, GLA_DV),
                                 log_alpha.reshape(bsz, seq, GLA_HEADS, GLA_DK))
    y_c = rms_norm(o_c, gla_norm_g).reshape(bsz, seq, GLA_WIDTH).astype(x.dtype)
    y = jnp.concatenate([y_a, y_b, y_c], axis=-1) * jax.nn.silu(z)
    out = y @ w_out
    return x + gate[:, None, :] * rms_norm(out, g_post)


def setup_inputs(seed: int = 0) -> dict:
    key = jax.random.key(seed)
    ks = jax.random.split(key, 17)
    nrm = jax.random.normal
    f32 = jnp.float32
    x = nrm(ks[0], (BATCH, SEQ, D_MODEL), f32)
    c = nrm(ks[1], (BATCH, D_MODEL), f32)
    positions = jax.random.randint(ks[2], (BATCH, 1), 0, 4096, dtype=jnp.int32) + jnp.arange(SEQ, dtype=jnp.int32)[None, :]
    w_mod = nrm(ks[3], (DEPTH, D_MODEL, 3 * D_MODEL), f32) * (0.5 * D_MODEL ** -0.5)
    b_mod = 0.01 * nrm(ks[4], (DEPTH, 3 * D_MODEL), f32)
    g_pre = 1.0 + 0.05 * nrm(ks[5], (DEPTH, D_MODEL), f32)
    g_post = 1.0 + 0.05 * nrm(ks[6], (DEPTH, D_MODEL), f32)
    w_in = nrm(ks[7], (DEPTH, D_MODEL, IN_PROJ_WIDTH), f32) * (D_MODEL ** -0.5)
    w_out = nrm(ks[8], (DEPTH, D_MIX, D_MODEL), f32) * (D_MIX ** -0.5)
    swa_sinks = 0.5 * nrm(ks[9], (DEPTH, SWA_HEADS), f32)
    sg_w = nrm(ks[10], (DEPTH, SG_GROUPS, SG_CHUNK, SG_CHUNK), f32) * (SG_CHUNK ** -0.5)
    sg_b = 1.0 + 0.1 * nrm(ks[11], (DEPTH, SG_GROUPS, SG_CHUNK), f32)
    sg_ln_g = 1.0 + 0.05 * nrm(ks[12], (DEPTH, SG_WIDTH), f32)
    sg_ln_b = 0.02 * nrm(ks[13], (DEPTH, SG_WIDTH), f32)
    gla_w_gate_up = nrm(ks[14], (DEPTH, GLA_GATE_RANK, GLA_KEY_WIDTH), f32) * (GLA_GATE_RANK ** -0.5)
    gla_b_gate = 0.1 * nrm(ks[15], (DEPTH, GLA_KEY_WIDTH), f32)
    gla_norm_g = 1.0 + 0.05 * nrm(ks[16], (DEPTH, GLA_DV), f32)
    return {'x': x, 'c': c, 'positions': positions, 'w_mod': w_mod, 'b_mod': b_mod,
            'g_pre': g_pre, 'g_post': g_post, 'w_in': w_in, 'w_out': w_out, 'swa_sinks': swa_sinks,
            'sg_w': sg_w, 'sg_b': sg_b, 'sg_ln_g': sg_ln_g, 'sg_ln_b': sg_ln_b,
            'gla_w_gate_up': gla_w_gate_up, 'gla_b_gate': gla_b_gate, 'gla_norm_g': gla_norm_g}


def reference(x, c, positions, w_mod, b_mod, g_pre, g_post, w_in, w_out, swa_sinks,
              sg_w, sg_b, sg_ln_g, sg_ln_b, gla_w_gate_up, gla_b_gate, gla_norm_g):
    for l in range(DEPTH):
        x = hybrid_layer(x, c, positions, w_mod[l], b_mod[l], g_pre[l], g_post[l], w_in[l], w_out[l],
                         swa_sinks[l], sg_w[l], sg_b[l], sg_ln_g[l], sg_ln_b[l],
                         gla_w_gate_up[l], gla_b_gate[l], gla_norm_g[l])
    return x
```

```python
import functools

import numpy as np
import jax
import jax.numpy as jnp
from jax import lax
from jax.experimental import pallas as pl
from jax.experimental.pallas import tpu as pltpu

F32 = jnp.float32
BF16 = jnp.bfloat16

D_MODEL = 2048
BATCH = 4
SEQ = 2048
DEPTH = 2
EPS = 1e-6
ROWS = BATCH * SEQ

SWA_HEAD_DIM = 64
SWA_HEADS = 16
SWA_KV_HEADS = 4
SWA_GROUP = SWA_HEADS // SWA_KV_HEADS
SWA_WIDTH = SWA_HEADS * SWA_HEAD_DIM
SWA_KV_WIDTH = SWA_KV_HEADS * SWA_HEAD_DIM
WINDOW = 128
ROT_DIM = 16
ROT_HALF = ROT_DIM // 2
ROPE_THETA = 500000.0

SG_WIDTH = 512
SG_GROUPS = 8
SG_GROUP_DIM = 64
SG_CHUNK = 128

GLA_HEADS = 4
GLA_WIDTH = 512
GLA_DV = 128
GLA_DK = 64
GLA_KEY_WIDTH = 256
GLA_GATE_RANK = 16
GLA_GATE_TAU = 16.0
GLA_CHUNK = 64
GLA_LEVELS = 6

LANES = 128
A_WIDTH = SWA_WIDTH + 2 * SWA_KV_WIDTH
B_WIDTH = 2 * SG_WIDTH
C_WIDTH = 2 * GLA_KEY_WIDTH + GLA_WIDTH
Z_WIDTH = D_MODEL
G_WIDTH = LANES
W_IN_PAD = A_WIDTH + B_WIDTH + C_WIDTH + Z_WIDTH + G_WIDTH
MOD_WIDTH = 3 * D_MODEL

VMEM_LIMIT = 56 * 1024 * 1024


def _sigmoid(x):
    return 1.0 / (1.0 + jnp.exp(-x))


def _gelu_tanh(x):
    return 0.5 * x * (1.0 + jnp.tanh(0.7978845608028654 * (x + 0.044715 * (x * x * x))))


def _dot_nt(a, b):
    return lax.dot_general(a, b, (((1,), (1,)), ((), ())), preferred_element_type=F32)


def _dot_tn(a, b):
    return lax.dot_general(a, b, (((0,), (0,)), ((), ())), preferred_element_type=F32)


MOD_TN = 768


def _mod_kernel(c_ref, w_ref, b_ref, o_ref):
    c = c_ref[...]
    s = (c * _sigmoid(c)).astype(BF16)
    o_ref[0] = jnp.dot(s, w_ref[0].astype(BF16), preferred_element_type=F32) + b_ref[0]


def _modulation(c_pad, w_mod, b_mod):
    return pl.pallas_call(
        _mod_kernel,
        grid=(DEPTH, MOD_WIDTH // MOD_TN),
        in_specs=[
            pl.BlockSpec((8, D_MODEL), lambda l, j: (0, 0)),
            pl.BlockSpec((1, D_MODEL, MOD_TN), lambda l, j: (l, 0, j)),
            pl.BlockSpec((1, 1, MOD_TN), lambda l, j: (l, 0, j)),
        ],
        out_specs=pl.BlockSpec((1, 8, MOD_TN), lambda l, j: (l, 0, j)),
        out_shape=jax.ShapeDtypeStruct((DEPTH, 8, MOD_WIDTH), F32),
        compiler_params=pltpu.CompilerParams(
            dimension_semantics=("arbitrary", "arbitrary"), vmem_limit_bytes=VMEM_LIMIT),
        name="adaln_mod",
    )(c_pad, w_mod, b_mod.reshape(DEPTH, 1, MOD_WIDTH))


ROPE_TM = 1024


def _rope_table_kernel(pos_ref, invf_ref, cos_ref, sin_ref):
    ang = pos_ref[...].astype(F32) * invf_ref[...]
    lane = lax.broadcasted_iota(jnp.int32, (1, LANES), 1) % SWA_HEAD_DIM
    s = jnp.sin(ang)
    cos_ref[...] = jnp.cos(ang)
    sin_ref[...] = jnp.where(lane < ROT_HALF, -s, s)


def _rope_tables(positions):
    half = np.arange(ROT_HALF, dtype=np.float32)
    inv_freq = (np.float32(ROPE_THETA) ** (-(half * np.float32(2.0 / ROT_DIM)))).astype(np.float32)
    lane = np.arange(LANES) % SWA_HEAD_DIM
    invf = np.where(lane < ROT_DIM, inv_freq[lane % ROT_HALF], 0.0).astype(np.float32)[None, :]
    return pl.pallas_call(
        _rope_table_kernel,
        grid=(ROWS // ROPE_TM,),
        in_specs=[
            pl.BlockSpec((ROPE_TM, 1), lambda i: (i, 0)),
            pl.BlockSpec((1, LANES), lambda i: (0, 0)),
        ],
        out_specs=[pl.BlockSpec((ROPE_TM, LANES), lambda i: (i, 0))] * 2,
        out_shape=[jax.ShapeDtypeStruct((ROWS, LANES), F32)] * 2,
        compiler_params=pltpu.CompilerParams(dimension_semantics=("arbitrary",)),
        name="rope_tables",
    )(positions.reshape(ROWS, 1), jnp.asarray(invf))


INPROJ_TM = 512
INPROJ_CHUNK = 512


def _inproj_kernel(x_ref, scale_ref, shift_ref, g_ref, w_ref,
                   oa_ref, ob_ref, oc_ref, oz_ref, og_ref, h_ref):
    x = x_ref[...]
    ms = jnp.mean(x * x, axis=-1, keepdims=True)
    y = x * lax.rsqrt(ms + EPS) * g_ref[...]
    h_ref[...] = (y * (1.0 + scale_ref[0]) + shift_ref[0]).astype(BF16)

    col = 0
    for o_ref in (oa_ref, ob_ref, oc_ref, oz_ref, og_ref):
        width = o_ref.shape[1]
        step = min(INPROJ_CHUNK, width)
        for c0 in range(0, width, step):
            o_ref[:, c0:c0 + step] = jnp.dot(
                h_ref[...], w_ref[:, col + c0:col + c0 + step],
                preferred_element_type=F32).astype(o_ref.dtype)
        col += width


def _inproj(x2, scale, shift, g_pre, w_in_p):
    tiles_per_batch = SEQ // INPROJ_TM
    widths = (A_WIDTH, B_WIDTH, C_WIDTH, Z_WIDTH, G_WIDTH)
    dtypes = (BF16, BF16, BF16, BF16, F32)
    return pl.pallas_call(
        _inproj_kernel,
        grid=(ROWS // INPROJ_TM,),
        in_specs=[
            pl.BlockSpec((INPROJ_TM, D_MODEL), lambda i: (i, 0)),
            pl.BlockSpec((1, 1, D_MODEL), lambda i: (i // tiles_per_batch, 0, 0)),
            pl.BlockSpec((1, 1, D_MODEL), lambda i: (i // tiles_per_batch, 0, 0)),
            pl.BlockSpec((1, D_MODEL), lambda i: (0, 0)),
            pl.BlockSpec((D_MODEL, W_IN_PAD), lambda i: (0, 0), pipeline_mode=pl.Buffered(1)),
        ],
        out_specs=[pl.BlockSpec((INPROJ_TM, w), lambda i: (i, 0)) for w in widths],
        out_shape=[jax.ShapeDtypeStruct((ROWS, w), dt) for w, dt in zip(widths, dtypes)],
        scratch_shapes=[pltpu.VMEM((INPROJ_TM, D_MODEL), BF16)],
        compiler_params=pltpu.CompilerParams(
            dimension_semantics=("arbitrary",), vmem_limit_bytes=VMEM_LIMIT),
        name="prenorm_inproj",
    )(x2, scale, shift, g_pre, w_in_p)


def _swa_kernel(sinks_ref, q_ref, k_ref, v_ref, kp_ref, vp_ref,
                cos_ref, sin_ref, cosp_ref, sinp_ref, o_ref):
    n = pl.program_id(1)
    lane = lax.broadcasted_iota(jnp.int32, (1, LANES), 1) % SWA_HEAD_DIM
    first_half = lane < ROT_HALF

    def rope(t, c, s):
        partner = jnp.where(first_half, pltpu.roll(t, LANES - ROT_HALF, 1), pltpu.roll(t, ROT_HALF, 1))
        return t * c + partner * s

    cos_c, sin_c = cos_ref[...], sin_ref[...]
    cos_b = jnp.concatenate([cosp_ref[...], cos_c], axis=0)
    sin_b = jnp.concatenate([sinp_ref[...], sin_c], axis=0)

    q_scale = SWA_HEAD_DIM ** -0.5
    q_tiles = []
    for t in range(SWA_WIDTH // LANES):
        qt = q_ref[:, t * LANES:(t + 1) * LANES].astype(F32) * q_scale
        q_tiles.append(rope(qt, cos_c, sin_c).astype(BF16))
    k_tiles = []
    for t in range(SWA_KV_WIDTH // LANES):
        kt = jnp.concatenate([kp_ref[:, t * LANES:(t + 1) * LANES],
                              k_ref[:, t * LANES:(t + 1) * LANES]], axis=0).astype(F32)
        k_tiles.append(rope(kt, cos_b, sin_b).astype(BF16))
    v_band = jnp.concatenate([vp_ref[...], v_ref[...]], axis=0)

    qi = lax.broadcasted_iota(jnp.int32, (WINDOW, 2 * WINDOW), 0)
    kj = lax.broadcasted_iota(jnp.int32, (WINDOW, 2 * WINDOW), 1)
    dist = qi + WINDOW - kj
    valid = (dist >= 0) & (dist < WINDOW) & ((kj >= WINDOW) | (n > 0))

    heads_per_tile = LANES // SWA_HEAD_DIM
    for t in range(SWA_WIDTH // LANES):
        outs = []
        for u in range(heads_per_tile):
            h = t * heads_per_tile + u
            g = h // SWA_GROUP
            qh = q_tiles[t][:, u * SWA_HEAD_DIM:(u + 1) * SWA_HEAD_DIM]
            kt = k_tiles[g // heads_per_tile]
            off = (g % heads_per_tile) * SWA_HEAD_DIM
            kg = kt[:, off:off + SWA_HEAD_DIM]
            vg = v_band[:, g * SWA_HEAD_DIM:(g + 1) * SWA_HEAD_DIM]
            s = jnp.where(valid, _dot_nt(qh, kg), -jnp.inf)
            sink = sinks_ref[h]
            m = jnp.maximum(jnp.max(s, axis=-1, keepdims=True), sink)
            p = jnp.exp(s - m)
            denom = jnp.sum(p, axis=-1, keepdims=True) + jnp.exp(sink - m)
            pv = jnp.dot(p.astype(BF16), vg, preferred_element_type=F32)
            outs.append(pv / denom)
        o_ref[:, t * LANES:(t + 1) * LANES] = jnp.concatenate(outs, axis=1).astype(o_ref.dtype)


def _swa(sinks, proj_a, cos_t, sin_t):
    nb = SEQ // WINDOW
    kcol = SWA_WIDTH // SWA_KV_WIDTH
    cur = lambda b, n: b * nb + n
    prev = lambda b, n: b * nb + jnp.maximum(n - 1, 0)
    return pl.pallas_call(
        _swa_kernel,
        grid=(BATCH, nb),
        in_specs=[
            pl.BlockSpec(memory_space=pltpu.SMEM),
            pl.BlockSpec((WINDOW, SWA_WIDTH), lambda b, n: (cur(b, n), 0)),
            pl.BlockSpec((WINDOW, SWA_KV_WIDTH), lambda b, n: (cur(b, n), kcol)),
            pl.BlockSpec((WINDOW, SWA_KV_WIDTH), lambda b, n: (cur(b, n), kcol + 1)),
            pl.BlockSpec((WINDOW, SWA_KV_WIDTH), lambda b, n: (prev(b, n), kcol)),
            pl.BlockSpec((WINDOW, SWA_KV_WIDTH), lambda b, n: (prev(b, n), kcol + 1)),
            pl.BlockSpec((WINDOW, LANES), lambda b, n: (cur(b, n), 0)),
            pl.BlockSpec((WINDOW, LANES), lambda b, n: (cur(b, n), 0)),
            pl.BlockSpec((WINDOW, LANES), lambda b, n: (prev(b, n), 0)),
            pl.BlockSpec((WINDOW, LANES), lambda b, n: (prev(b, n), 0)),
        ],
        out_specs=pl.BlockSpec((WINDOW, SWA_WIDTH), lambda b, n: (cur(b, n), 0)),
        out_shape=jax.ShapeDtypeStruct((ROWS, SWA_WIDTH), BF16),
        compiler_params=pltpu.CompilerParams(dimension_semantics=("arbitrary", "arbitrary")),
        name="swa",
    )(sinks, proj_a, proj_a, proj_a, proj_a, proj_a, cos_t, sin_t, cos_t, sin_t)


def _sg_kernel(u_ref, v_ref, w_ref, bias_ref, lng_ref, lnb_ref, o_ref):
    u = _gelu_tanh(u_ref[...].astype(F32))
    v = _gelu_tanh(v_ref[...].astype(F32))
    mu = jnp.mean(v, axis=-1, keepdims=True)
    vc = v - mu
    var = jnp.mean(vc * vc, axis=-1, keepdims=True)
    vn = (vc * lax.rsqrt(var + EPS) * lng_ref[...] + lnb_ref[...]).astype(BF16)
    t = lax.broadcasted_iota(jnp.int32, (SG_CHUNK, SG_CHUNK), 0)
    s = lax.broadcasted_iota(jnp.int32, (SG_CHUNK, SG_CHUNK), 1)
    causal = t >= s
    parts = []
    for g in range(SG_GROUPS):
        w = jnp.where(causal, w_ref[g], 0.0).astype(BF16)
        parts.append(jnp.dot(w, vn[:, g * SG_GROUP_DIM:(g + 1) * SG_GROUP_DIM],
                             preferred_element_type=F32))
    mixed = jnp.concatenate(parts, axis=1) + bias_ref[...]
    o_ref[...] = (u * mixed).astype(o_ref.dtype)


def _spatial_gating(proj_b, sg_w, bias_tile, ln_g, ln_b):
    return pl.pallas_call(
        _sg_kernel,
        grid=(ROWS // SG_CHUNK,),
        in_specs=[
            pl.BlockSpec((SG_CHUNK, SG_WIDTH), lambda i: (i, 0)),
            pl.BlockSpec((SG_CHUNK, SG_WIDTH), lambda i: (i, 1)),
            pl.BlockSpec((SG_GROUPS, SG_CHUNK, SG_CHUNK), lambda i: (0, 0, 0)),
            pl.BlockSpec((SG_CHUNK, SG_WIDTH), lambda i: (0, 0)),
            pl.BlockSpec((1, SG_WIDTH), lambda i: (0, 0)),
            pl.BlockSpec((1, SG_WIDTH), lambda i: (0, 0)),
        ],
        out_specs=pl.BlockSpec((SG_CHUNK, SG_WIDTH), lambda i: (i, 0)),
        out_shape=jax.ShapeDtypeStruct((ROWS, SG_WIDTH), BF16),
        compiler_params=pltpu.CompilerParams(dimension_semantics=("arbitrary",)),
        name="spatial_gating",
    )(proj_b, proj_b, sg_w, bias_tile, ln_g, ln_b)


def _gla_sum_matrix():
    c = GLA_CHUNK
    mat = np.zeros(((2 + GLA_LEVELS) * c, c), np.float32)
    for t in range(c):
        mat[t, :t + 1] = 1.0
        mat[c + t, t + 1:] = 1.0
        for k in range(GLA_LEVELS):
            m = 1 << k
            r = (t >> (k + 1) << (k + 1)) + m
            row = (2 + k) * c + t
            if (t >> k) & 1:
                mat[row, r + 1:t + 1] = 1.0
            else:
                mat[row, t + 1:r + 1] = 1.0
    return mat


def _gla_kernel(q_ref, k_ref, v_ref, cg_ref, wup_ref, bg_ref, summat_ref, ng_ref, o_ref, st_ref):
    c = GLA_CHUNK

    @pl.when(pl.program_id(1) == 0)
    def _():
        st_ref[...] = jnp.zeros_like(st_ref)

    logits = jnp.dot(cg_ref[...].astype(BF16), wup_ref[...], preferred_element_type=F32) + bg_ref[...]
    log_alpha = (jnp.minimum(logits, 0.0) - jnp.log1p(jnp.exp(-jnp.abs(logits)))) * (1.0 / GLA_GATE_TAU)

    hi = log_alpha.astype(BF16)
    r1 = log_alpha - hi.astype(F32)
    mid = r1.astype(BF16)
    lo = (r1 - mid.astype(F32)).astype(BF16)
    sums = jnp.dot(summat_ref[...], jnp.concatenate([hi, mid, lo], axis=1), preferred_element_type=F32)
    expo = (sums[:, :GLA_KEY_WIDTH] + sums[:, GLA_KEY_WIDTH:2 * GLA_KEY_WIDTH]
            + sums[:, 2 * GLA_KEY_WIDTH:])

    q = q_ref[...].astype(F32) * (GLA_DK ** -0.5)
    k = k_ref[...].astype(F32)
    v = v_ref[...]
    b = expo[0:c]
    q_inter = (q * jnp.exp(b)).astype(BF16)
    k_state = (k * jnp.exp(expo[c:2 * c])).astype(BF16)
    decay = jnp.exp(b[c - 1:c, :])

    t_i = lax.broadcasted_iota(jnp.int32, (c, c), 0)
    s_i = lax.broadcasted_iota(jnp.int32, (c, c), 1)
    q_bf, k_bf = q.astype(BF16), k.astype(BF16)
    q_lv, k_lv, masks = [q_bf], [k_bf], [t_i == s_i]
    for lv in range(GLA_LEVELS):
        f = jnp.exp(expo[(2 + lv) * c:(3 + lv) * c])
        q_lv.append((q * f).astype(BF16))
        k_lv.append((k * f).astype(BF16))
        masks.append(((t_i >> (lv + 1)) == (s_i >> (lv + 1)))
                     & (((t_i >> lv) & 1) == 1) & (((s_i >> lv) & 1) == 0))

    for h in range(GLA_HEADS):
        ks = slice(h * GLA_DK, (h + 1) * GLA_DK)
        vs = slice(h * GLA_DV, (h + 1) * GLA_DV)
        a = jnp.zeros((c, c), F32)
        for ql, kl, mask in zip(q_lv, k_lv, masks):
            a = a + jnp.where(mask, _dot_nt(ql[:, ks], kl[:, ks]), 0.0)
        vh = v[:, vs]
        st = st_ref[:, ks]
        o = jnp.dot(a.astype(BF16), vh, preferred_element_type=F32) + _dot_nt(q_inter[:, ks], st.astype(BF16))
        st_ref[:, ks] = st * decay[:, ks] + _dot_tn(vh, k_state[:, ks])
        y = o * lax.rsqrt(jnp.mean(o * o, axis=-1, keepdims=True) + EPS) * ng_ref[...]
        o_ref[:, vs] = y.astype(o_ref.dtype)


def _gla(proj_c, proj_g, wup_pad, b_gate, summat, norm_g):
    nc = SEQ // GLA_CHUNK
    row = lambda b, c: b * nc + c
    return pl.pallas_call(
        _gla_kernel,
        grid=(BATCH, nc),
        in_specs=[
            pl.BlockSpec((GLA_CHUNK, GLA_KEY_WIDTH), lambda b, c: (row(b, c), 0)),
            pl.BlockSpec((GLA_CHUNK, GLA_KEY_WIDTH), lambda b, c: (row(b, c), 1)),
            pl.BlockSpec((GLA_CHUNK, GLA_WIDTH), lambda b, c: (row(b, c), 1)),
            pl.BlockSpec((GLA_CHUNK, G_WIDTH), lambda b, c: (row(b, c), 0)),
            pl.BlockSpec((G_WIDTH, GLA_KEY_WIDTH), lambda b, c: (0, 0)),
            pl.BlockSpec((1, GLA_KEY_WIDTH), lambda b, c: (0, 0)),
            pl.BlockSpec(((2 + GLA_LEVELS) * GLA_CHUNK, GLA_CHUNK), lambda b, c: (0, 0)),
            pl.BlockSpec((1, GLA_DV), lambda b, c: (0, 0)),
        ],
        out_specs=pl.BlockSpec((GLA_CHUNK, GLA_WIDTH), lambda b, c: (row(b, c), 0)),
        out_shape=jax.ShapeDtypeStruct((ROWS, GLA_WIDTH), BF16),
        scratch_shapes=[pltpu.VMEM((GLA_DV, GLA_KEY_WIDTH), F32)],
        compiler_params=pltpu.CompilerParams(dimension_semantics=("arbitrary", "arbitrary")),
        name="gla",
    )(proj_c, proj_c, proj_c, proj_g, wup_pad, b_gate, summat, norm_g)


OUT_TM = 512


def _outproj_kernel(ya_ref, yb_ref, yc_ref, z_ref, x_ref, gate_ref, g_ref, w_ref, o_ref):
    acc = None
    col = 0
    for y_ref in (ya_ref, yb_ref, yc_ref):
        width = y_ref.shape[1]
        z = z_ref[:, col:col + width].astype(F32)
        y = (y_ref[...].astype(F32) * (z * _sigmoid(z))).astype(BF16)
        part = jnp.dot(y, w_ref[col:col + width, :], preferred_element_type=F32)
        acc = part if acc is None else acc + part
        col += width
    normed = acc * lax.rsqrt(jnp.mean(acc * acc, axis=-1, keepdims=True) + EPS) * g_ref[...]
    o_ref[...] = x_ref[...] + gate_ref[0] * normed


def _outproj(y_a, y_b, y_c, z, x2, gate, g_post, w_out_bf):
    tiles_per_batch = SEQ // OUT_TM
    return pl.pallas_call(
        _outproj_kernel,
        grid=(ROWS // OUT_TM,),
        in_specs=[
            pl.BlockSpec((OUT_TM, SWA_WIDTH), lambda i: (i, 0)),
            pl.BlockSpec((OUT_TM, SG_WIDTH), lambda i: (i, 0)),
            pl.BlockSpec((OUT_TM, GLA_WIDTH), lambda i: (i, 0)),
            pl.BlockSpec((OUT_TM, Z_WIDTH), lambda i: (i, 0)),
            pl.BlockSpec((OUT_TM, D_MODEL), lambda i: (i, 0)),
            pl.BlockSpec((1, 1, D_MODEL), lambda i: (i // tiles_per_batch, 0, 0)),
            pl.BlockSpec((1, D_MODEL), lambda i: (0, 0)),
            pl.BlockSpec((D_MODEL, D_MODEL), lambda i: (0, 0), pipeline_mode=pl.Buffered(1)),
        ],
        out_specs=pl.BlockSpec((OUT_TM, D_MODEL), lambda i: (i, 0)),
        out_shape=jax.ShapeDtypeStruct((ROWS, D_MODEL), F32),
        compiler_params=pltpu.CompilerParams(
            dimension_semantics=("arbitrary",), vmem_limit_bytes=VMEM_LIMIT),
        name="gate_outproj_residual",
    )(y_a, y_b, y_c, z, x2, gate, g_post, w_out_bf)


def kernel(x, c, positions, w_mod, b_mod, g_pre, g_post, w_in, w_out, swa_sinks,
           sg_w, sg_b, sg_ln_g, sg_ln_b, gla_w_gate_up, gla_b_gate, gla_norm_g):
    assert x.shape == (BATCH, SEQ, D_MODEL) and w_in.shape[0] == DEPTH

    c_pad = jnp.pad(c, ((0, 8 - BATCH), (0, 0)))
    mod = _modulation(c_pad, w_mod, b_mod)[:, :BATCH, :]
    cos_t, sin_t = _rope_tables(positions)
    summat = jnp.asarray(_gla_sum_matrix(), dtype=BF16)

    gate_col = A_WIDTH + B_WIDTH + C_WIDTH
    x2 = x.reshape(ROWS, D_MODEL)
    for l in range(DEPTH):
        shift = mod[l, :, 0:D_MODEL].reshape(BATCH, 1, D_MODEL)
        scale = mod[l, :, D_MODEL:2 * D_MODEL].reshape(BATCH, 1, D_MODEL)
        gate = mod[l, :, 2 * D_MODEL:].reshape(BATCH, 1, D_MODEL)
        w_in_p = jnp.concatenate(
            [w_in[l][:, :gate_col], w_in[l][:, gate_col + GLA_GATE_RANK:],
             jnp.pad(w_in[l][:, gate_col:gate_col + GLA_GATE_RANK], ((0, 0), (0, G_WIDTH - GLA_GATE_RANK)))],
            axis=1).astype(BF16)
        proj_a, proj_b, proj_c, proj_z, proj_g = _inproj(
            x2, scale, shift, g_pre[l].reshape(1, D_MODEL), w_in_p)

        y_a = _swa(swa_sinks[l], proj_a, cos_t, sin_t)
        bias_tile = jnp.repeat(sg_b[l].T, SG_GROUP_DIM, axis=1)
        y_b = _spatial_gating(proj_b, sg_w[l], bias_tile,
                              sg_ln_g[l].reshape(1, SG_WIDTH), sg_ln_b[l].reshape(1, SG_WIDTH))
        wup_pad = jnp.pad(gla_w_gate_up[l], ((0, G_WIDTH - GLA_GATE_RANK), (0, 0))).astype(BF16)
        y_c = _gla(proj_c, proj_g, wup_pad, gla_b_gate[l].reshape(1, GLA_KEY_WIDTH), summat,
                   gla_norm_g[l].reshape(1, GLA_DV))

        x2 = _outproj(y_a, y_b, y_c, proj_z, x2, gate, g_post[l].reshape(1, D_MODEL),
                      w_out[l].astype(BF16))
    return x2.reshape(BATCH, SEQ, D_MODEL)
```

```python
import functools

import numpy as np
import jax
import jax.numpy as jnp
from jax import lax
from jax.experimental import pallas as pl
from jax.experimental.pallas import tpu as pltpu

F32 = jnp.float32
BF16 = jnp.bfloat16

D_MODEL = 2048
BATCH = 4
SEQ = 2048
DEPTH = 2
EPS = 1e-6
ROWS = BATCH * SEQ

SWA_HEAD_DIM = 64
SWA_HEADS = 16
SWA_KV_HEADS = 4
SWA_GROUP = SWA_HEADS // SWA_KV_HEADS
SWA_WIDTH = SWA_HEADS * SWA_HEAD_DIM
SWA_KV_WIDTH = SWA_KV_HEADS * SWA_HEAD_DIM
WINDOW = 128
ROT_DIM = 16
ROT_HALF = ROT_DIM // 2
ROPE_THETA = 500000.0

SG_WIDTH = 512
SG_GROUPS = 8
SG_GROUP_DIM = 64
SG_CHUNK = 128

GLA_HEADS = 4
GLA_WIDTH = 512
GLA_DV = 128
GLA_DK = 64
GLA_KEY_WIDTH = 256
GLA_GATE_RANK = 16
GLA_GATE_TAU = 16.0
GLA_CHUNK = 64
GLA_LEVELS = 6

LANES = 128
A_WIDTH = SWA_WIDTH + 2 * SWA_KV_WIDTH
B_WIDTH = 2 * SG_WIDTH
C_WIDTH = 2 * GLA_KEY_WIDTH + GLA_WIDTH
Z_WIDTH = D_MODEL
G_WIDTH = LANES
W_IN_PAD = A_WIDTH + B_WIDTH + C_WIDTH + Z_WIDTH + G_WIDTH
MOD_WIDTH = 3 * D_MODEL

VMEM_LIMIT = 56 * 1024 * 1024


def _sigmoid(x):
    return 1.0 / (1.0 + jnp.exp(-x))


def _gelu_tanh(x):
    return 0.5 * x * (1.0 + jnp.tanh(0.7978845608028654 * (x + 0.044715 * (x * x * x))))


def _dot_nt(a, b):
    return lax.dot_general(a, b, (((1,), (1,)), ((), ())), preferred_element_type=F32)


def _dot_tn(a, b):
    return lax.dot_general(a, b, (((0,), (0,)), ((), ())), preferred_element_type=F32)


MOD_TN = 768


def _mod_kernel(c_ref, w_ref, b_ref, o_ref):
    c = c_ref[...]
    s = (c * _sigmoid(c)).astype(BF16)
    o_ref[0] = jnp.dot(s, w_ref[0].astype(BF16), preferred_element_type=F32) + b_ref[0]


def _modulation(c_pad, w_mod, b_mod):
    return pl.pallas_call(
        _mod_kernel,
        grid=(DEPTH, MOD_WIDTH // MOD_TN),
        in_specs=[
            pl.BlockSpec((8, D_MODEL), lambda l, j: (0, 0)),
            pl.BlockSpec((1, D_MODEL, MOD_TN), lambda l, j: (l, 0, j)),
            pl.BlockSpec((1, 1, MOD_TN), lambda l, j: (l, 0, j)),
        ],
        out_specs=pl.BlockSpec((1, 8, MOD_TN), lambda l, j: (l, 0, j)),
        out_shape=jax.ShapeDtypeStruct((DEPTH, 8, MOD_WIDTH), F32),
        compiler_params=pltpu.CompilerParams(
            dimension_semantics=("arbitrary", "arbitrary"), vmem_limit_bytes=VMEM_LIMIT),
        name="adaln_mod",
    )(c_pad, w_mod, b_mod.reshape(DEPTH, 1, MOD_WIDTH))


ROPE_TM = 1024


def _rope_table_kernel(pos_ref, invf_ref, cos_ref, sin_ref):
    ang = pos_ref[...].astype(F32) * invf_ref[...]
    lane = lax.broadcasted_iota(jnp.int32, (1, LANES), 1) % SWA_HEAD_DIM
    s = jnp.sin(ang)
    cos_ref[...] = jnp.cos(ang)
    sin_ref[...] = jnp.where(lane < ROT_HALF, -s, s)


def _rope_tables(positions):
    half = np.arange(ROT_HALF, dtype=np.float32)
    inv_freq = (np.float32(ROPE_THETA) ** (-(half * np.float32(2.0 / ROT_DIM)))).astype(np.float32)
    lane = np.arange(LANES) % SWA_HEAD_DIM
    invf = np.where(lane < ROT_DIM, inv_freq[lane % ROT_HALF], 0.0).astype(np.float32)[None, :]
    return pl.pallas_call(
        _rope_table_kernel,
        grid=(ROWS // ROPE_TM,),
        in_specs=[
            pl.BlockSpec((ROPE_TM, 1), lambda i: (i, 0)),
            pl.BlockSpec((1, LANES), lambda i: (0, 0)),
        ],
        out_specs=[pl.BlockSpec((ROPE_TM, LANES), lambda i: (i, 0))] * 2,
        out_shape=[jax.ShapeDtypeStruct((ROWS, LANES), F32)] * 2,
        compiler_params=pltpu.CompilerParams(dimension_semantics=("arbitrary",)),
        name="rope_tables",
    )(positions.reshape(ROWS, 1), jnp.asarray(invf))


INPROJ_TM = 512
INPROJ_CHUNK = 512


def _inproj_kernel(x_ref, scale_ref, shift_ref, g_ref, w_ref,
                   oa_ref, ob_ref, oc_ref, oz_ref, og_ref, h_ref):
    x = x_ref[...]
    ms = jnp.mean(x * x, axis=-1, keepdims=True)
    y = x * lax.rsqrt(ms + EPS) * g_ref[...]
    h_ref[...] = (y * (1.0 + scale_ref[0]) + shift_ref[0]).astype(BF16)

    col = 0
    for o_ref in (oa_ref, ob_ref, oc_ref, oz_ref, og_ref):
        width = o_ref.shape[1]
        step = min(INPROJ_CHUNK, width)
        for c0 in range(0, width, step):
            o_ref[:, c0:c0 + step] = jnp.dot(
                h_ref[...], w_ref[:, col + c0:col + c0 + step],
                preferred_element_type=F32).astype(o_ref.dtype)
        col += width


def _inproj(x2, scale, shift, g_pre, w_in_p):
    tiles_per_batch = SEQ // INPROJ_TM
    widths = (A_WIDTH, B_WIDTH, C_WIDTH, Z_WIDTH, G_WIDTH)
    dtypes = (BF16, BF16, BF16, BF16, F32)
    return pl.pallas_call(
        _inproj_kernel,
        grid=(ROWS // INPROJ_TM,),
        in_specs=[
            pl.BlockSpec((INPROJ_TM, D_MODEL), lambda i: (i, 0)),
            pl.BlockSpec((1, 1, D_MODEL), lambda i: (i // tiles_per_batch, 0, 0)),
            pl.BlockSpec((1, 1, D_MODEL), lambda i: (i // tiles_per_batch, 0, 0)),
            pl.BlockSpec((1, D_MODEL), lambda i: (0, 0)),
            pl.BlockSpec((D_MODEL, W_IN_PAD), lambda i: (0, 0), pipeline_mode=pl.Buffered(1)),
        ],
        out_specs=[pl.BlockSpec((INPROJ_TM, w), lambda i: (i, 0)) for w in widths],
        out_shape=[jax.ShapeDtypeStruct((ROWS, w), dt) for w, dt in zip(widths, dtypes)],
        scratch_shapes=[pltpu.VMEM((INPROJ_TM, D_MODEL), BF16)],
        compiler_params=pltpu.CompilerParams(
            dimension_semantics=("arbitrary",), vmem_limit_bytes=VMEM_LIMIT),
        name="prenorm_inproj",
    )(x2, scale, shift, g_pre, w_in_p)


def _swa_kernel(sinks_ref, q_ref, k_ref, v_ref, kp_ref, vp_ref,
                cos_ref, sin_ref, cosp_ref, sinp_ref, o_ref, s_ref, p_ref):
    n = pl.program_id(1)
    lane = lax.broadcasted_iota(jnp.int32, (1, LANES), 1) % SWA_HEAD_DIM
    first_half = lane < ROT_HALF

    def rope(t, c, s):
        partner = jnp.where(first_half, pltpu.roll(t, LANES - ROT_HALF, 1), pltpu.roll(t, ROT_HALF, 1))
        return t * c + partner * s

    cos_c, sin_c = cos_ref[...], sin_ref[...]
    cos_b = jnp.concatenate([cosp_ref[...], cos_c], axis=0)
    sin_b = jnp.concatenate([sinp_ref[...], sin_c], axis=0)
    low_half = lax.broadcasted_iota(jnp.int32, (1, LANES), 1) < SWA_HEAD_DIM

    q_scale = SWA_HEAD_DIM ** -0.5
    q_tiles = []
    for t in range(SWA_WIDTH // LANES):
        qt = q_ref[:, t * LANES:(t + 1) * LANES].astype(F32) * q_scale
        q_tiles.append(rope(qt, cos_c, sin_c).astype(BF16))

    k_sel, v_dup = [], []
    for t in range(SWA_KV_WIDTH // LANES):
        cols = slice(t * LANES, (t + 1) * LANES)
        kt = rope(jnp.concatenate([kp_ref[:, cols], k_ref[:, cols]], axis=0).astype(F32), cos_b, sin_b)
        kt_sw = pltpu.roll(kt, SWA_HEAD_DIM, 1)
        vt = jnp.concatenate([vp_ref[:, cols], v_ref[:, cols]], axis=0).astype(F32)
        vt_sw = pltpu.roll(vt, SWA_HEAD_DIM, 1)
        for src_lo, src_hi, v_lo, v_hi in ((kt, kt_sw, vt, vt_sw), (kt_sw, kt, vt_sw, vt)):
            k_sel.append((jnp.where(low_half, src_lo, 0.0).astype(BF16),
                          jnp.where(low_half, 0.0, src_hi).astype(BF16)))
            v_dup.append(jnp.where(low_half, v_lo, v_hi).astype(BF16))

    qi = lax.broadcasted_iota(jnp.int32, (WINDOW, WINDOW), 0)
    kj = lax.broadcasted_iota(jnp.int32, (WINDOW, WINDOW), 1)
    from_prev = kj > qi
    valid = (kj <= qi) | (n > 0)

    heads_per_tile = LANES // SWA_HEAD_DIM
    for h in range(SWA_HEADS):
        t, u = divmod(h, heads_per_tile)
        s2 = _dot_nt(q_tiles[t], k_sel[h // SWA_GROUP][u])
        s_ref[h] = jnp.where(from_prev, s2[:, :WINDOW], s2[:, WINDOW:])
    for h in range(SWA_HEADS):
        s = jnp.where(valid, s_ref[h], -jnp.inf)
        sink = sinks_ref[h]
        m = jnp.maximum(jnp.max(s, axis=-1, keepdims=True), sink)
        p = jnp.exp(s - m)
        denom = jnp.sum(p, axis=-1, keepdims=True) + jnp.exp(sink - m)
        p = (p * (1.0 / denom)).astype(BF16)
        zero = jnp.zeros_like(p)
        p_ref[h, :, :WINDOW] = jnp.where(from_prev, p, zero)
        p_ref[h, :, WINDOW:] = jnp.where(from_prev, zero, p)
    for t in range(SWA_WIDTH // LANES):
        vg = v_dup[(t * heads_per_tile) // SWA_GROUP]
        outs = [jnp.dot(p_ref[t * heads_per_tile + u], vg, preferred_element_type=F32)
                for u in range(heads_per_tile)]
        o_ref[:, t * LANES:(t + 1) * LANES] = jnp.where(low_half, outs[0], outs[1]).astype(o_ref.dtype)


def _swa(sinks, proj_a, cos_t, sin_t):
    nb = SEQ // WINDOW
    kcol = SWA_WIDTH // SWA_KV_WIDTH
    cur = lambda b, n: b * nb + n
    prev = lambda b, n: b * nb + jnp.maximum(n - 1, 0)
    return pl.pallas_call(
        _swa_kernel,
        grid=(BATCH, nb),
        in_specs=[
            pl.BlockSpec(memory_space=pltpu.SMEM),
            pl.BlockSpec((WINDOW, SWA_WIDTH), lambda b, n: (cur(b, n), 0)),
            pl.BlockSpec((WINDOW, SWA_KV_WIDTH), lambda b, n: (cur(b, n), kcol)),
            pl.BlockSpec((WINDOW, SWA_KV_WIDTH), lambda b, n: (cur(b, n), kcol + 1)),
            pl.BlockSpec((WINDOW, SWA_KV_WIDTH), lambda b, n: (prev(b, n), kcol)),
            pl.BlockSpec((WINDOW, SWA_KV_WIDTH), lambda b, n: (prev(b, n), kcol + 1)),
            pl.BlockSpec((WINDOW, LANES), lambda b, n: (cur(b, n), 0)),
            pl.BlockSpec((WINDOW, LANES), lambda b, n: (cur(b, n), 0)),
            pl.BlockSpec((WINDOW, LANES), lambda b, n: (prev(b, n), 0)),
            pl.BlockSpec((WINDOW, LANES), lambda b, n: (prev(b, n), 0)),
        ],
        out_specs=pl.BlockSpec((WINDOW, SWA_WIDTH), lambda b, n: (cur(b, n), 0)),
        out_shape=jax.ShapeDtypeStruct((ROWS, SWA_WIDTH), BF16),
        scratch_shapes=[pltpu.VMEM((SWA_HEADS, WINDOW, WINDOW), F32),
                        pltpu.VMEM((SWA_HEADS, WINDOW, 2 * WINDOW), BF16)],
        compiler_params=pltpu.CompilerParams(dimension_semantics=("arbitrary", "arbitrary")),
        name="swa",
    )(sinks, proj_a, proj_a, proj_a, proj_a, proj_a, cos_t, sin_t, cos_t, sin_t)


def _sg_kernel(u_ref, v_ref, w_ref, bias_ref, lng_ref, lnb_ref, o_ref):
    u = _gelu_tanh(u_ref[...].astype(F32))
    v = _gelu_tanh(v_ref[...].astype(F32))
    mu = jnp.mean(v, axis=-1, keepdims=True)
    vc = v - mu
    var = jnp.mean(vc * vc, axis=-1, keepdims=True)
    vn = (vc * lax.rsqrt(var + EPS) * lng_ref[...] + lnb_ref[...]).astype(BF16)
    t = lax.broadcasted_iota(jnp.int32, (SG_CHUNK, SG_CHUNK), 0)
    s = lax.broadcasted_iota(jnp.int32, (SG_CHUNK, SG_CHUNK), 1)
    causal = t >= s
    parts = []
    for g in range(SG_GROUPS):
        w = jnp.where(causal, w_ref[g], 0.0).astype(BF16)
        parts.append(jnp.dot(w, vn[:, g * SG_GROUP_DIM:(g + 1) * SG_GROUP_DIM],
                             preferred_element_type=F32))
    mixed = jnp.concatenate(parts, axis=1) + bias_ref[...]
    o_ref[...] = (u * mixed).astype(o_ref.dtype)


def _spatial_gating(proj_b, sg_w, bias_tile, ln_g, ln_b):
    return pl.pallas_call(
        _sg_kernel,
        grid=(ROWS // SG_CHUNK,),
        in_specs=[
            pl.BlockSpec((SG_CHUNK, SG_WIDTH), lambda i: (i, 0)),
            pl.BlockSpec((SG_CHUNK, SG_WIDTH), lambda i: (i, 1)),
            pl.BlockSpec((SG_GROUPS, SG_CHUNK, SG_CHUNK), lambda i: (0, 0, 0)),
            pl.BlockSpec((SG_CHUNK, SG_WIDTH), lambda i: (0, 0)),
            pl.BlockSpec((1, SG_WIDTH), lambda i: (0, 0)),
            pl.BlockSpec((1, SG_WIDTH), lambda i: (0, 0)),
        ],
        out_specs=pl.BlockSpec((SG_CHUNK, SG_WIDTH), lambda i: (i, 0)),
        out_shape=jax.ShapeDtypeStruct((ROWS, SG_WIDTH), BF16),
        compiler_params=pltpu.CompilerParams(dimension_semantics=("arbitrary",)),
        name="spatial_gating",
    )(proj_b, proj_b, sg_w, bias_tile, ln_g, ln_b)


def _gla_sum_matrix():
    c = GLA_CHUNK
    mat = np.zeros(((2 + GLA_LEVELS) * c, c), np.float32)
    for t in range(c):
        mat[t, :t + 1] = 1.0
        mat[c + t, t + 1:] = 1.0
        for k in range(GLA_LEVELS):
            m = 1 << k
            r = (t >> (k + 1) << (k + 1)) + m
            row = (2 + k) * c + t
            if (t >> k) & 1:
                mat[row, r + 1:t + 1] = 1.0
            else:
                mat[row, t + 1:r + 1] = 1.0
    return mat


def _split_heads_on_rows(x, low_half):
    zero = jnp.zeros_like(x)
    return jnp.concatenate([jnp.where(low_half, x, zero), jnp.where(low_half, zero, x)], axis=0)


def _gla_kernel(q_ref, k_ref, v_ref, cg_ref, wup_ref, bg_ref, summat_ref, ng_ref, o_ref, st_ref):
    c = GLA_CHUNK
    heads_per_tile = LANES // GLA_DK
    n_tiles = BATCH * GLA_KEY_WIDTH // LANES

    @pl.when(pl.program_id(0) == 0)
    def _():
        st_ref[...] = jnp.zeros_like(st_ref)

    cg = cg_ref[...].reshape(BATCH * c, G_WIDTH).astype(BF16)
    logits = jnp.dot(cg, wup_ref[...], preferred_element_type=F32) + bg_ref[...]
    log_alpha = (jnp.minimum(logits, 0.0) - jnp.log1p(jnp.exp(-jnp.abs(logits)))) * (1.0 / GLA_GATE_TAU)

    hi = log_alpha.astype(BF16)
    r1 = log_alpha - hi.astype(F32)
    mid = r1.astype(BF16)
    lo = (r1 - mid.astype(F32)).astype(BF16)
    pieces = jnp.concatenate(
        [jnp.concatenate([p[b * c:(b + 1) * c] for p in (hi, mid, lo)], axis=0) for b in range(BATCH)],
        axis=1)
    expo = jnp.dot(summat_ref[...], pieces, preferred_element_type=F32)

    q = jnp.concatenate([q_ref[b] for b in range(BATCH)], axis=1).astype(F32) * (GLA_DK ** -0.5)
    k = jnp.concatenate([k_ref[b] for b in range(BATCH)], axis=1).astype(F32)
    b_cum = expo[0:c]
    q_inter = (q * jnp.exp(b_cum)).astype(BF16)
    k_state = (k * jnp.exp(expo[c:2 * c])).astype(BF16)
    decay = jnp.exp(b_cum[c - 1:c, :])

    t_i = lax.broadcasted_iota(jnp.int32, (heads_per_tile * c, c), 0) % c
    s_i = lax.broadcasted_iota(jnp.int32, (heads_per_tile * c, c), 1)
    q_lv, k_lv, masks = [q.astype(BF16)], [k.astype(BF16)], [t_i == s_i]
    for lv in range(GLA_LEVELS):
        f = jnp.exp(expo[(2 + lv) * c:(3 + lv) * c])
        q_lv.append((q * f).astype(BF16))
        k_lv.append((k * f).astype(BF16))
        masks.append(((t_i >> (lv + 1)) == (s_i >> (lv + 1)))
                     & (((t_i >> lv) & 1) == 1) & (((s_i >> lv) & 1) == 0))

    low_half = lax.broadcasted_iota(jnp.int32, (1, LANES), 1) < GLA_DK
    scores = []
    for j in range(n_tiles):
        sl = slice(j * LANES, (j + 1) * LANES)
        a2 = jnp.zeros((heads_per_tile * c, c), F32)
        for ql, kl, mask in zip(q_lv, k_lv, masks):
            a2 = a2 + jnp.where(mask, _dot_nt(_split_heads_on_rows(ql[:, sl], low_half), kl[:, sl]), 0.0)
        scores.append(a2.astype(BF16))

    tiles_per_seq = GLA_KEY_WIDTH // LANES
    for j in range(n_tiles):
        b, pair = divmod(j, tiles_per_seq)
        sl = slice(j * LANES, (j + 1) * LANES)
        st_cols = slice(pair * LANES, (pair + 1) * LANES)
        st = st_ref[b, :, st_cols]
        inter = _dot_nt(_split_heads_on_rows(q_inter[:, sl], low_half), st.astype(BF16))
        updates = []
        for u in range(heads_per_tile):
            h = pair * heads_per_tile + u
            vs = slice(h * GLA_DV, (h + 1) * GLA_DV)
            vh = v_ref[b, :, vs]
            o = jnp.dot(scores[j][u * c:(u + 1) * c], vh, preferred_element_type=F32) + inter[u * c:(u + 1) * c]
            y = o * lax.rsqrt(jnp.mean(o * o, axis=-1, keepdims=True) + EPS) * ng_ref[...]
            o_ref[b, :, vs] = y.astype(o_ref.dtype)
            updates.append(_dot_tn(vh, k_state[:, sl]))
        st_ref[b, :, st_cols] = st * decay[:, sl] + jnp.where(low_half, updates[0], updates[1])


def _gla(proj_c, proj_g, wup_pad, b_gate, summat3, norm_g):
    return pl.pallas_call(
        _gla_kernel,
        grid=(SEQ // GLA_CHUNK,),
        in_specs=[
            pl.BlockSpec((BATCH, GLA_CHUNK, GLA_KEY_WIDTH), lambda c: (0, c, 0)),
            pl.BlockSpec((BATCH, GLA_CHUNK, GLA_KEY_WIDTH), lambda c: (0, c, 1)),
            pl.BlockSpec((BATCH, GLA_CHUNK, GLA_WIDTH), lambda c: (0, c, 1)),
            pl.BlockSpec((BATCH, GLA_CHUNK, G_WIDTH), lambda c: (0, c, 0)),
            pl.BlockSpec((G_WIDTH, GLA_KEY_WIDTH), lambda c: (0, 0)),
            pl.BlockSpec((1, GLA_KEY_WIDTH), lambda c: (0, 0)),
            pl.BlockSpec(((2 + GLA_LEVELS) * GLA_CHUNK, 3 * GLA_CHUNK), lambda c: (0, 0)),
            pl.BlockSpec((1, GLA_DV), lambda c: (0, 0)),
        ],
        out_specs=pl.BlockSpec((BATCH, GLA_CHUNK, GLA_WIDTH), lambda c: (0, c, 0)),
        out_shape=jax.ShapeDtypeStruct((BATCH, SEQ, GLA_WIDTH), BF16),
        scratch_shapes=[pltpu.VMEM((BATCH, GLA_DV, GLA_KEY_WIDTH), F32)],
        compiler_params=pltpu.CompilerParams(dimension_semantics=("arbitrary",)),
        name="gla",
    )(proj_c, proj_c, proj_c, proj_g, wup_pad, b_gate, summat3, norm_g)


OUT_TM = 512


def _outproj_kernel(ya_ref, yb_ref, yc_ref, z_ref, x_ref, gate_ref, g_ref, w_ref, o_ref):
    acc = None
    col = 0
    for y_ref in (ya_ref, yb_ref, yc_ref):
        width = y_ref.shape[1]
        z = z_ref[:, col:col + width].astype(F32)
        y = (y_ref[...].astype(F32) * (z * _sigmoid(z))).astype(BF16)
        part = jnp.dot(y, w_ref[col:col + width, :], preferred_element_type=F32)
        acc = part if acc is None else acc + part
        col += width
    normed = acc * lax.rsqrt(jnp.mean(acc * acc, axis=-1, keepdims=True) + EPS) * g_ref[...]
    o_ref[...] = x_ref[...] + gate_ref[0] * normed


def _outproj(y_a, y_b, y_c, z, x2, gate, g_post, w_out_bf):
    tiles_per_batch = SEQ // OUT_TM
    return pl.pallas_call(
        _outproj_kernel,
        grid=(ROWS // OUT_TM,),
        in_specs=[
            pl.BlockSpec((OUT_TM, SWA_WIDTH), lambda i: (i, 0)),
            pl.BlockSpec((OUT_TM, SG_WIDTH), lambda i: (i, 0)),
            pl.BlockSpec((OUT_TM, GLA_WIDTH), lambda i: (i, 0)),
            pl.BlockSpec((OUT_TM, Z_WIDTH), lambda i: (i, 0)),
            pl.BlockSpec((OUT_TM, D_MODEL), lambda i: (i, 0)),
            pl.BlockSpec((1, 1, D_MODEL), lambda i: (i // tiles_per_batch, 0, 0)),
            pl.BlockSpec((1, D_MODEL), lambda i: (0, 0)),
            pl.BlockSpec((D_MODEL, D_MODEL), lambda i: (0, 0), pipeline_mode=pl.Buffered(1)),
        ],
        out_specs=pl.BlockSpec((OUT_TM, D_MODEL), lambda i: (i, 0)),
        out_shape=jax.ShapeDtypeStruct((ROWS, D_MODEL), F32),
        compiler_params=pltpu.CompilerParams(
            dimension_semantics=("arbitrary",), vmem_limit_bytes=VMEM_LIMIT),
        name="gate_outproj_residual",
    )(y_a, y_b, y_c, z, x2, gate, g_post, w_out_bf)


def kernel(x, c, positions, w_mod, b_mod, g_pre, g_post, w_in, w_out, swa_sinks,
           sg_w, sg_b, sg_ln_g, sg_ln_b, gla_w_gate_up, gla_b_gate, gla_norm_g):
    assert x.shape == (BATCH, SEQ, D_MODEL) and w_in.shape[0] == DEPTH

    c_pad = jnp.pad(c, ((0, 8 - BATCH), (0, 0)))
    mod = _modulation(c_pad, w_mod, b_mod)[:, :BATCH, :]
    cos_t, sin_t = _rope_tables(positions)
    summat3 = jnp.asarray(np.tile(_gla_sum_matrix(), (1, 3)), dtype=BF16)

    gate_col = A_WIDTH + B_WIDTH + C_WIDTH
    x2 = x.reshape(ROWS, D_MODEL)
    for l in range(DEPTH):
        shift = mod[l, :, 0:D_MODEL].reshape(BATCH, 1, D_MODEL)
        scale = mod[l, :, D_MODEL:2 * D_MODEL].reshape(BATCH, 1, D_MODEL)
        gate = mod[l, :, 2 * D_MODEL:].reshape(BATCH, 1, D_MODEL)
        w_in_p = jnp.concatenate(
            [w_in[l][:, :gate_col], w_in[l][:, gate_col + GLA_GATE_RANK:],
             jnp.pad(w_in[l][:, gate_col:gate_col + GLA_GATE_RANK], ((0, 0), (0, G_WIDTH - GLA_GATE_RANK)))],
            axis=1).astype(BF16)
        proj_a, proj_b, proj_c, proj_z, proj_g = _inproj(
            x2, scale, shift, g_pre[l].reshape(1, D_MODEL), w_in_p)

        y_a = _swa(swa_sinks[l], proj_a, cos_t, sin_t)
        bias_tile = jnp.repeat(sg_b[l].T, SG_GROUP_DIM, axis=1)
        y_b = _spatial_gating(proj_b, sg_w[l], bias_tile,
                              sg_ln_g[l].reshape(1, SG_WIDTH), sg_ln_b[l].reshape(1, SG_WIDTH))
        wup_pad = jnp.pad(gla_w_gate_up[l], ((0, G_WIDTH - GLA_GATE_RANK), (0, 0))).astype(BF16)
        y_c = _gla(proj_c.reshape(BATCH, SEQ, C_WIDTH), proj_g.reshape(BATCH, SEQ, G_WIDTH), wup_pad,
                   gla_b_gate[l].reshape(1, GLA_KEY_WIDTH), summat3,
                   gla_norm_g[l].reshape(1, GLA_DV)).reshape(ROWS, GLA_WIDTH)

        x2 = _outproj(y_a, y_b, y_c, proj_z, x2, gate, g_post[l].reshape(1, D_MODEL),
                      w_out[l].astype(BF16))
    return x2.reshape(BATCH, SEQ, D_MODEL)
```

```python
import functools

import numpy as np
import jax
import jax.numpy as jnp
from jax import lax
from jax.experimental import pallas as pl
from jax.experimental.pallas import tpu as pltpu

F32 = jnp.float32
BF16 = jnp.bfloat16

D_MODEL = 2048
BATCH = 4
SEQ = 2048
DEPTH = 2
EPS = 1e-6
ROWS = BATCH * SEQ

SWA_HEAD_DIM = 64
SWA_HEADS = 16
SWA_KV_HEADS = 4
SWA_GROUP = SWA_HEADS // SWA_KV_HEADS
SWA_WIDTH = SWA_HEADS * SWA_HEAD_DIM
SWA_KV_WIDTH = SWA_KV_HEADS * SWA_HEAD_DIM
WINDOW = 128
ROT_DIM = 16
ROT_HALF = ROT_DIM // 2
ROPE_THETA = 500000.0

SG_WIDTH = 512
SG_GROUPS = 8
SG_GROUP_DIM = 64
SG_CHUNK = 128

GLA_HEADS = 4
GLA_WIDTH = 512
GLA_DV = 128
GLA_DK = 64
GLA_KEY_WIDTH = 256
GLA_GATE_RANK = 16
GLA_GATE_TAU = 16.0
GLA_CHUNK = 64
GLA_LEVELS = 6

LANES = 128
A_WIDTH = SWA_WIDTH + 2 * SWA_KV_WIDTH
B_WIDTH = 2 * SG_WIDTH
C_WIDTH = 2 * GLA_KEY_WIDTH + GLA_WIDTH
Z_WIDTH = D_MODEL
G_WIDTH = LANES
W_IN_PAD = A_WIDTH + B_WIDTH + C_WIDTH + Z_WIDTH + G_WIDTH
MOD_WIDTH = 3 * D_MODEL

VMEM_LIMIT = 56 * 1024 * 1024


def _sigmoid(x):
    return 1.0 / (1.0 + jnp.exp(-x))


def _gelu_tanh(x):
    return 0.5 * x * (1.0 + jnp.tanh(0.7978845608028654 * (x + 0.044715 * (x * x * x))))


def _dot_nt(a, b):
    return lax.dot_general(a, b, (((1,), (1,)), ((), ())), preferred_element_type=F32)


def _dot_tn(a, b):
    return lax.dot_general(a, b, (((0,), (0,)), ((), ())), preferred_element_type=F32)


MOD_TN = 768


def _mod_kernel(c_ref, w_ref, b_ref, o_ref):
    c = c_ref[...]
    s = (c * _sigmoid(c)).astype(BF16)
    o_ref[0] = jnp.dot(s, w_ref[0].astype(BF16), preferred_element_type=F32) + b_ref[0]


def _modulation(c_pad, w_mod, b_mod):
    return pl.pallas_call(
        _mod_kernel,
        grid=(DEPTH, MOD_WIDTH // MOD_TN),
        in_specs=[
            pl.BlockSpec((8, D_MODEL), lambda l, j: (0, 0)),
            pl.BlockSpec((1, D_MODEL, MOD_TN), lambda l, j: (l, 0, j)),
            pl.BlockSpec((1, 1, MOD_TN), lambda l, j: (l, 0, j)),
        ],
        out_specs=pl.BlockSpec((1, 8, MOD_TN), lambda l, j: (l, 0, j)),
        out_shape=jax.ShapeDtypeStruct((DEPTH, 8, MOD_WIDTH), F32),
        compiler_params=pltpu.CompilerParams(
            dimension_semantics=("arbitrary", "arbitrary"), vmem_limit_bytes=VMEM_LIMIT),
        name="adaln_mod",
    )(c_pad, w_mod, b_mod.reshape(DEPTH, 1, MOD_WIDTH))


ROPE_TM = 1024


def _rope_table_kernel(pos_ref, invf_ref, cos_ref, sin_ref):
    ang = pos_ref[...].astype(F32) * invf_ref[...]
    lane = lax.broadcasted_iota(jnp.int32, (1, LANES), 1) % SWA_HEAD_DIM
    s = jnp.sin(ang)
    cos_ref[...] = jnp.cos(ang)
    sin_ref[...] = jnp.where(lane < ROT_HALF, -s, s)


def _rope_tables(positions):
    half = np.arange(ROT_HALF, dtype=np.float32)
    inv_freq = (np.float32(ROPE_THETA) ** (-(half * np.float32(2.0 / ROT_DIM)))).astype(np.float32)
    lane = np.arange(LANES) % SWA_HEAD_DIM
    invf = np.where(lane < ROT_DIM, inv_freq[lane % ROT_HALF], 0.0).astype(np.float32)[None, :]
    return pl.pallas_call(
        _rope_table_kernel,
        grid=(ROWS // ROPE_TM,),
        in_specs=[
            pl.BlockSpec((ROPE_TM, 1), lambda i: (i, 0)),
            pl.BlockSpec((1, LANES), lambda i: (0, 0)),
        ],
        out_specs=[pl.BlockSpec((ROPE_TM, LANES), lambda i: (i, 0))] * 2,
        out_shape=[jax.ShapeDtypeStruct((ROWS, LANES), F32)] * 2,
        compiler_params=pltpu.CompilerParams(dimension_semantics=("arbitrary",)),
        name="rope_tables",
    )(positions.reshape(ROWS, 1), jnp.asarray(invf))


PREP_TN = 512
MAIN_WIDTH = A_WIDTH + B_WIDTH + C_WIDTH + Z_WIDTH
GATE_COL = A_WIDTH + B_WIDTH + C_WIDTH
IN_PROJ_WIDTH = GATE_COL + GLA_GATE_RANK + Z_WIDTH


def _prep_w_in_kernel(a_ref, b_ref, main_ref, gate_ref):
    j = pl.program_id(1)
    first_z = GATE_COL // PREP_TN
    a = a_ref[0]

    @pl.when(j < first_z)
    def _():
        main_ref[0] = a.astype(BF16)

    @pl.when(j >= first_z)
    def _():
        lane = lax.broadcasted_iota(jnp.int32, (1, PREP_TN), 1)
        keep = PREP_TN - GLA_GATE_RANK
        shifted = jnp.where(lane < keep, pltpu.roll(a, keep, 1), pltpu.roll(b_ref[0], keep, 1))
        main_ref[0] = shifted.astype(BF16)

    @pl.when(j == first_z)
    def _():
        lane = lax.broadcasted_iota(jnp.int32, (1, G_WIDTH), 1)
        gate_ref[0] = jnp.where(lane < GLA_GATE_RANK, a[:, :G_WIDTH], 0.0).astype(BF16)


def _prep_w_in(w_in):
    first_z = GATE_COL // PREP_TN
    last = (IN_PROJ_WIDTH - 1) // PREP_TN
    return pl.pallas_call(
        _prep_w_in_kernel,
        grid=(DEPTH, MAIN_WIDTH // PREP_TN),
        in_specs=[
            pl.BlockSpec((1, D_MODEL, PREP_TN), lambda l, j: (l, 0, j)),
            pl.BlockSpec((1, D_MODEL, PREP_TN), lambda l, j: (l, 0, jnp.clip(j + 1, first_z + 1, last))),
        ],
        out_specs=[
            pl.BlockSpec((1, D_MODEL, PREP_TN), lambda l, j: (l, 0, j)),
            pl.BlockSpec((1, D_MODEL, G_WIDTH), lambda l, j: (l, 0, 0)),
        ],
        out_shape=[jax.ShapeDtypeStruct((DEPTH, D_MODEL, MAIN_WIDTH), BF16),
                   jax.ShapeDtypeStruct((DEPTH, D_MODEL, G_WIDTH), BF16)],
        compiler_params=pltpu.CompilerParams(
            dimension_semantics=("arbitrary", "arbitrary"), vmem_limit_bytes=VMEM_LIMIT),
        name="prep_w_in",
    )(w_in, w_in)


def _cast_kernel(x_ref, o_ref):
    o_ref[...] = x_ref[...].astype(o_ref.dtype)


def _prep_w_out(w_out):
    tn = 1024
    return pl.pallas_call(
        _cast_kernel,
        grid=(DEPTH, D_MODEL // tn),
        in_specs=[pl.BlockSpec((1, D_MODEL, tn), lambda l, j: (l, 0, j))],
        out_specs=pl.BlockSpec((1, D_MODEL, tn), lambda l, j: (l, 0, j)),
        out_shape=jax.ShapeDtypeStruct((DEPTH, D_MODEL, D_MODEL), BF16),
        compiler_params=pltpu.CompilerParams(
            dimension_semantics=("arbitrary", "arbitrary"), vmem_limit_bytes=VMEM_LIMIT),
        name="prep_w_out",
    )(w_out)


INPROJ_TM = 512
INPROJ_CHUNK = 512


def _inproj_kernel(x_ref, scale_ref, shift_ref, g_ref, w_ref, wg_ref,
                   oa_ref, ob_ref, oc_ref, oz_ref, og_ref, h_ref):
    x = x_ref[...]
    ms = jnp.mean(x * x, axis=-1, keepdims=True)
    y = x * lax.rsqrt(ms + EPS) * g_ref[...]
    h_ref[...] = (y * (1.0 + scale_ref[0]) + shift_ref[0]).astype(BF16)

    col = 0
    for o_ref in (oa_ref, ob_ref, oc_ref, oz_ref):
        width = o_ref.shape[1]
        for c0 in range(0, width, INPROJ_CHUNK):
            o_ref[:, c0:c0 + INPROJ_CHUNK] = jnp.dot(
                h_ref[...], w_ref[0, :, col + c0:col + c0 + INPROJ_CHUNK],
                preferred_element_type=F32).astype(o_ref.dtype)
        col += width
    og_ref[...] = jnp.dot(h_ref[...], wg_ref[0], preferred_element_type=F32)


def _inproj(layer, x2, scale, shift, g_pre, w_main, w_gate):
    tiles_per_batch = SEQ // INPROJ_TM
    widths = (A_WIDTH, B_WIDTH, C_WIDTH, Z_WIDTH, G_WIDTH)
    dtypes = (BF16, BF16, BF16, BF16, F32)
    return pl.pallas_call(
        _inproj_kernel,
        grid=(ROWS // INPROJ_TM,),
        in_specs=[
            pl.BlockSpec((INPROJ_TM, D_MODEL), lambda i: (i, 0)),
            pl.BlockSpec((1, 1, D_MODEL), lambda i: (i // tiles_per_batch, 0, 0)),
            pl.BlockSpec((1, 1, D_MODEL), lambda i: (i // tiles_per_batch, 0, 0)),
            pl.BlockSpec((1, D_MODEL), lambda i: (0, 0)),
            pl.BlockSpec((1, D_MODEL, MAIN_WIDTH), lambda i: (layer, 0, 0), pipeline_mode=pl.Buffered(1)),
            pl.BlockSpec((1, D_MODEL, G_WIDTH), lambda i: (layer, 0, 0), pipeline_mode=pl.Buffered(1)),
        ],
        out_specs=[pl.BlockSpec((INPROJ_TM, w), lambda i: (i, 0)) for w in widths],
        out_shape=[jax.ShapeDtypeStruct((ROWS, w), dt) for w, dt in zip(widths, dtypes)],
        scratch_shapes=[pltpu.VMEM((INPROJ_TM, D_MODEL), BF16)],
        compiler_params=pltpu.CompilerParams(
            dimension_semantics=("arbitrary",), vmem_limit_bytes=VMEM_LIMIT),
        name="prenorm_inproj",
    )(x2, scale, shift, g_pre, w_main, w_gate)


def _swa_kernel(sinks_ref, q_ref, k_ref, v_ref, kp_ref, vp_ref,
                cos_ref, sin_ref, cosp_ref, sinp_ref, o_ref, s_ref, p_ref):
    n = pl.program_id(1)
    lane = lax.broadcasted_iota(jnp.int32, (1, LANES), 1) % SWA_HEAD_DIM
    first_half = lane < ROT_HALF

    def rope(t, c, s):
        partner = jnp.where(first_half, pltpu.roll(t, LANES - ROT_HALF, 1), pltpu.roll(t, ROT_HALF, 1))
        return t * c + partner * s

    cos_c, sin_c = cos_ref[...], sin_ref[...]
    cos_b = jnp.concatenate([cosp_ref[...], cos_c], axis=0)
    sin_b = jnp.concatenate([sinp_ref[...], sin_c], axis=0)
    low_half = lax.broadcasted_iota(jnp.int32, (1, LANES), 1) < SWA_HEAD_DIM

    q_scale = SWA_HEAD_DIM ** -0.5
    q_tiles = []
    for t in range(SWA_WIDTH // LANES):
        qt = q_ref[:, t * LANES:(t + 1) * LANES].astype(F32) * q_scale
        q_tiles.append(rope(qt, cos_c, sin_c).astype(BF16))

    k_sel, v_dup = [], []
    for t in range(SWA_KV_WIDTH // LANES):
        cols = slice(t * LANES, (t + 1) * LANES)
        kt = rope(jnp.concatenate([kp_ref[:, cols], k_ref[:, cols]], axis=0).astype(F32), cos_b, sin_b)
        kt_sw = pltpu.roll(kt, SWA_HEAD_DIM, 1)
        vt = jnp.concatenate([vp_ref[:, cols], v_ref[:, cols]], axis=0).astype(F32)
        vt_sw = pltpu.roll(vt, SWA_HEAD_DIM, 1)
        for src_lo, src_hi, v_lo, v_hi in ((kt, kt_sw, vt, vt_sw), (kt_sw, kt, vt_sw, vt)):
            k_sel.append((jnp.where(low_half, src_lo, 0.0).astype(BF16),
                          jnp.where(low_half, 0.0, src_hi).astype(BF16)))
            v_dup.append(jnp.where(low_half, v_lo, v_hi).astype(BF16))

    qi = lax.broadcasted_iota(jnp.int32, (WINDOW, WINDOW), 0)
    kj = lax.broadcasted_iota(jnp.int32, (WINDOW, WINDOW), 1)
    from_prev = kj > qi
    valid = (kj <= qi) | (n > 0)

    heads_per_tile = LANES // SWA_HEAD_DIM
    for h in range(SWA_HEADS):
        t, u = divmod(h, heads_per_tile)
        s2 = _dot_nt(q_tiles[t], k_sel[h // SWA_GROUP][u])
        s_ref[h] = jnp.where(from_prev, s2[:, :WINDOW], s2[:, WINDOW:])
    for h in range(SWA_HEADS):
        s = jnp.where(valid, s_ref[h], -jnp.inf)
        sink = sinks_ref[h]
        m = jnp.maximum(jnp.max(s, axis=-1, keepdims=True), sink)
        p = jnp.exp(s - m)
        denom = jnp.sum(p, axis=-1, keepdims=True) + jnp.exp(sink - m)
        p = (p * (1.0 / denom)).astype(BF16)
        zero = jnp.zeros_like(p)
        p_ref[h, :, :WINDOW] = jnp.where(from_prev, p, zero)
        p_ref[h, :, WINDOW:] = jnp.where(from_prev, zero, p)
    for t in range(SWA_WIDTH // LANES):
        vg = v_dup[(t * heads_per_tile) // SWA_GROUP]
        outs = [jnp.dot(p_ref[t * heads_per_tile + u], vg, preferred_element_type=F32)
                for u in range(heads_per_tile)]
        o_ref[:, t * LANES:(t + 1) * LANES] = jnp.where(low_half, outs[0], outs[1]).astype(o_ref.dtype)


def _swa(sinks, proj_a, cos_t, sin_t):
    nb = SEQ // WINDOW
    kcol = SWA_WIDTH // SWA_KV_WIDTH
    cur = lambda b, n: b * nb + n
    prev = lambda b, n: b * nb + jnp.maximum(n - 1, 0)
    return pl.pallas_call(
        _swa_kernel,
        grid=(BATCH, nb),
        in_specs=[
            pl.BlockSpec(memory_space=pltpu.SMEM),
            pl.BlockSpec((WINDOW, SWA_WIDTH), lambda b, n: (cur(b, n), 0)),
            pl.BlockSpec((WINDOW, SWA_KV_WIDTH), lambda b, n: (cur(b, n), kcol)),
            pl.BlockSpec((WINDOW, SWA_KV_WIDTH), lambda b, n: (cur(b, n), kcol + 1)),
            pl.BlockSpec((WINDOW, SWA_KV_WIDTH), lambda b, n: (prev(b, n), kcol)),
            pl.BlockSpec((WINDOW, SWA_KV_WIDTH), lambda b, n: (prev(b, n), kcol + 1)),
            pl.BlockSpec((WINDOW, LANES), lambda b, n: (cur(b, n), 0)),
            pl.BlockSpec((WINDOW, LANES), lambda b, n: (cur(b, n), 0)),
            pl.BlockSpec((WINDOW, LANES), lambda b, n: (prev(b, n), 0)),
            pl.BlockSpec((WINDOW, LANES), lambda b, n: (prev(b, n), 0)),
        ],
        out_specs=pl.BlockSpec((WINDOW, SWA_WIDTH), lambda b, n: (cur(b, n), 0)),
        out_shape=jax.ShapeDtypeStruct((ROWS, SWA_WIDTH), BF16),
        scratch_shapes=[pltpu.VMEM((SWA_HEADS, WINDOW, WINDOW), F32),
                        pltpu.VMEM((SWA_HEADS, WINDOW, 2 * WINDOW), BF16)],
        compiler_params=pltpu.CompilerParams(dimension_semantics=("arbitrary", "arbitrary")),
        name="swa",
    )(sinks, proj_a, proj_a, proj_a, proj_a, proj_a, cos_t, sin_t, cos_t, sin_t)


def _sg_kernel(u_ref, v_ref, w_ref, bias_ref, lng_ref, lnb_ref, o_ref):
    u = _gelu_tanh(u_ref[...].astype(F32))
    v = _gelu_tanh(v_ref[...].astype(F32))
    mu = jnp.mean(v, axis=-1, keepdims=True)
    vc = v - mu
    var = jnp.mean(vc * vc, axis=-1, keepdims=True)
    vn = (vc * lax.rsqrt(var + EPS) * lng_ref[...] + lnb_ref[...]).astype(BF16)
    t = lax.broadcasted_iota(jnp.int32, (SG_CHUNK, SG_CHUNK), 0)
    s = lax.broadcasted_iota(jnp.int32, (SG_CHUNK, SG_CHUNK), 1)
    causal = t >= s
    parts = []
    for g in range(SG_GROUPS):
        w = jnp.where(causal, w_ref[g], 0.0).astype(BF16)
        parts.append(jnp.dot(w, vn[:, g * SG_GROUP_DIM:(g + 1) * SG_GROUP_DIM],
                             preferred_element_type=F32))
    mixed = jnp.concatenate(parts, axis=1) + bias_ref[...]
    o_ref[...] = (u * mixed).astype(o_ref.dtype)


def _spatial_gating(proj_b, sg_w, bias_tile, ln_g, ln_b):
    return pl.pallas_call(
        _sg_kernel,
        grid=(ROWS // SG_CHUNK,),
        in_specs=[
            pl.BlockSpec((SG_CHUNK, SG_WIDTH), lambda i: (i, 0)),
            pl.BlockSpec((SG_CHUNK, SG_WIDTH), lambda i: (i, 1)),
            pl.BlockSpec((SG_GROUPS, SG_CHUNK, SG_CHUNK), lambda i: (0, 0, 0)),
            pl.BlockSpec((SG_CHUNK, SG_WIDTH), lambda i: (0, 0)),
            pl.BlockSpec((1, SG_WIDTH), lambda i: (0, 0)),
            pl.BlockSpec((1, SG_WIDTH), lambda i: (0, 0)),
        ],
        out_specs=pl.BlockSpec((SG_CHUNK, SG_WIDTH), lambda i: (i, 0)),
        out_shape=jax.ShapeDtypeStruct((ROWS, SG_WIDTH), BF16),
        compiler_params=pltpu.CompilerParams(dimension_semantics=("arbitrary",)),
        name="spatial_gating",
    )(proj_b, proj_b, sg_w, bias_tile, ln_g, ln_b)


def _gla_sum_matrix():
    c = GLA_CHUNK
    mat = np.zeros(((2 + GLA_LEVELS) * c, c), np.float32)
    for t in range(c):
        mat[t, :t + 1] = 1.0
        mat[c + t, t + 1:] = 1.0
        for k in range(GLA_LEVELS):
            m = 1 << k
            r = (t >> (k + 1) << (k + 1)) + m
            row = (2 + k) * c + t
            if (t >> k) & 1:
                mat[row, r + 1:t + 1] = 1.0
            else:
                mat[row, t + 1:r + 1] = 1.0
    return mat


def _split_heads_on_rows(x, low_half):
    zero = jnp.zeros_like(x)
    return jnp.concatenate([jnp.where(low_half, x, zero), jnp.where(low_half, zero, x)], axis=0)


def _gla_kernel(q_ref, k_ref, v_ref, cg_ref, wup_ref, bg_ref, summat_ref, ng_ref, o_ref, st_ref):
    c = GLA_CHUNK
    heads_per_tile = LANES // GLA_DK
    n_tiles = BATCH * GLA_KEY_WIDTH // LANES

    @pl.when(pl.program_id(0) == 0)
    def _():
        st_ref[...] = jnp.zeros_like(st_ref)

    cg = cg_ref[...].reshape(BATCH * c, G_WIDTH).astype(BF16)
    logits = jnp.dot(cg, wup_ref[...], preferred_element_type=F32) + bg_ref[...]
    log_alpha = (jnp.minimum(logits, 0.0) - jnp.log1p(jnp.exp(-jnp.abs(logits)))) * (1.0 / GLA_GATE_TAU)

    hi = log_alpha.astype(BF16)
    r1 = log_alpha - hi.astype(F32)
    mid = r1.astype(BF16)
    lo = (r1 - mid.astype(F32)).astype(BF16)
    pieces = jnp.concatenate(
        [jnp.concatenate([p[b * c:(b + 1) * c] for p in (hi, mid, lo)], axis=0) for b in range(BATCH)],
        axis=1)
    expo = jnp.dot(summat_ref[...], pieces, preferred_element_type=F32)

    q = jnp.concatenate([q_ref[b] for b in range(BATCH)], axis=1).astype(F32) * (GLA_DK ** -0.5)
    k = jnp.concatenate([k_ref[b] for b in range(BATCH)], axis=1).astype(F32)
    b_cum = expo[0:c]
    q_inter = (q * jnp.exp(b_cum)).astype(BF16)
    k_state = (k * jnp.exp(expo[c:2 * c])).astype(BF16)
    decay = jnp.exp(b_cum[c - 1:c, :])

    t_i = lax.broadcasted_iota(jnp.int32, (heads_per_tile * c, c), 0) % c
    s_i = lax.broadcasted_iota(jnp.int32, (heads_per_tile * c, c), 1)
    q_lv, k_lv, masks = [q.astype(BF16)], [k.astype(BF16)], [t_i == s_i]
    for lv in range(GLA_LEVELS):
        f = jnp.exp(expo[(2 + lv) * c:(3 + lv) * c])
        q_lv.append((q * f).astype(BF16))
        k_lv.append((k * f).astype(BF16))
        masks.append(((t_i >> (lv + 1)) == (s_i >> (lv + 1)))
                     & (((t_i >> lv) & 1) == 1) & (((s_i >> lv) & 1) == 0))

    low_half = lax.broadcasted_iota(jnp.int32, (1, LANES), 1) < GLA_DK
    scores = []
    for j in range(n_tiles):
        sl = slice(j * LANES, (j + 1) * LANES)
        a2 = jnp.zeros((heads_per_tile * c, c), F32)
        for ql, kl, mask in zip(q_lv, k_lv, masks):
            a2 = a2 + jnp.where(mask, _dot_nt(_split_heads_on_rows(ql[:, sl], low_half), kl[:, sl]), 0.0)
        scores.append(a2.astype(BF16))

    tiles_per_seq = GLA_KEY_WIDTH // LANES
    for j in range(n_tiles):
        b, pair = divmod(j, tiles_per_seq)
        sl = slice(j * LANES, (j + 1) * LANES)
        st_cols = slice(pair * LANES, (pair + 1) * LANES)
        st = st_ref[b, :, st_cols]
        inter = _dot_nt(_split_heads_on_rows(q_inter[:, sl], low_half), st.astype(BF16))
        updates = []
        for u in range(heads_per_tile):
            h = pair * heads_per_tile + u
            vs = slice(h * GLA_DV, (h + 1) * GLA_DV)
            vh = v_ref[b, :, vs]
            o = jnp.dot(scores[j][u * c:(u + 1) * c], vh, preferred_element_type=F32) + inter[u * c:(u + 1) * c]
            y = o * lax.rsqrt(jnp.mean(o * o, axis=-1, keepdims=True) + EPS) * ng_ref[...]
            o_ref[b, :, vs] = y.astype(o_ref.dtype)
            updates.append(_dot_tn(vh, k_state[:, sl]))
        st_ref[b, :, st_cols] = st * decay[:, sl] + jnp.where(low_half, updates[0], updates[1])


def _gla(proj_c, proj_g, wup_pad, b_gate, summat3, norm_g):
    return pl.pallas_call(
        _gla_kernel,
        grid=(SEQ // GLA_CHUNK,),
        in_specs=[
            pl.BlockSpec((BATCH, GLA_CHUNK, GLA_KEY_WIDTH), lambda c: (0, c, 0)),
            pl.BlockSpec((BATCH, GLA_CHUNK, GLA_KEY_WIDTH), lambda c: (0, c, 1)),
            pl.BlockSpec((BATCH, GLA_CHUNK, GLA_WIDTH), lambda c: (0, c, 1)),
            pl.BlockSpec((BATCH, GLA_CHUNK, G_WIDTH), lambda c: (0, c, 0)),
            pl.BlockSpec((G_WIDTH, GLA_KEY_WIDTH), lambda c: (0, 0)),
            pl.BlockSpec((1, GLA_KEY_WIDTH), lambda c: (0, 0)),
            pl.BlockSpec(((2 + GLA_LEVELS) * GLA_CHUNK, 3 * GLA_CHUNK), lambda c: (0, 0)),
            pl.BlockSpec((1, GLA_DV), lambda c: (0, 0)),
        ],
        out_specs=pl.BlockSpec((BATCH, GLA_CHUNK, GLA_WIDTH), lambda c: (0, c, 0)),
        out_shape=jax.ShapeDtypeStruct((BATCH, SEQ, GLA_WIDTH), BF16),
        scratch_shapes=[pltpu.VMEM((BATCH, GLA_DV, GLA_KEY_WIDTH), F32)],
        compiler_params=pltpu.CompilerParams(dimension_semantics=("arbitrary",)),
        name="gla",
    )(proj_c, proj_c, proj_c, proj_g, wup_pad, b_gate, summat3, norm_g)


OUT_TM = 512


def _outproj_kernel(ya_ref, yb_ref, yc_ref, z_ref, x_ref, gate_ref, g_ref, w_ref, o_ref):
    acc = None
    col = 0
    for y_ref in (ya_ref, yb_ref, yc_ref):
        width = y_ref.shape[1]
        z = z_ref[:, col:col + width].astype(F32)
        y = (y_ref[...].astype(F32) * (z * _sigmoid(z))).astype(BF16)
        part = jnp.dot(y, w_ref[0, col:col + width, :], preferred_element_type=F32)
        acc = part if acc is None else acc + part
        col += width
    normed = acc * lax.rsqrt(jnp.mean(acc * acc, axis=-1, keepdims=True) + EPS) * g_ref[...]
    o_ref[...] = x_ref[...] + gate_ref[0] * normed


def _outproj(layer, y_a, y_b, y_c, z, x2, gate, g_post, w_out_bf):
    tiles_per_batch = SEQ // OUT_TM
    return pl.pallas_call(
        _outproj_kernel,
        grid=(ROWS // OUT_TM,),
        in_specs=[
            pl.BlockSpec((OUT_TM, SWA_WIDTH), lambda i: (i, 0)),
            pl.BlockSpec((OUT_TM, SG_WIDTH), lambda i: (i, 0)),
            pl.BlockSpec((OUT_TM, GLA_WIDTH), lambda i: (i, 0)),
            pl.BlockSpec((OUT_TM, Z_WIDTH), lambda i: (i, 0)),
            pl.BlockSpec((OUT_TM, D_MODEL), lambda i: (i, 0)),
            pl.BlockSpec((1, 1, D_MODEL), lambda i: (i // tiles_per_batch, 0, 0)),
            pl.BlockSpec((1, D_MODEL), lambda i: (0, 0)),
            pl.BlockSpec((1, D_MODEL, D_MODEL), lambda i: (layer, 0, 0), pipeline_mode=pl.Buffered(1)),
        ],
        out_specs=pl.BlockSpec((OUT_TM, D_MODEL), lambda i: (i, 0)),
        out_shape=jax.ShapeDtypeStruct((ROWS, D_MODEL), F32),
        compiler_params=pltpu.CompilerParams(
            dimension_semantics=("arbitrary",), vmem_limit_bytes=VMEM_LIMIT),
        name="gate_outproj_residual",
    )(y_a, y_b, y_c, z, x2, gate, g_post, w_out_bf)


def kernel(x, c, positions, w_mod, b_mod, g_pre, g_post, w_in, w_out, swa_sinks,
           sg_w, sg_b, sg_ln_g, sg_ln_b, gla_w_gate_up, gla_b_gate, gla_norm_g):
    assert x.shape == (BATCH, SEQ, D_MODEL) and w_in.shape[0] == DEPTH

    c_pad = jnp.pad(c, ((0, 8 - BATCH), (0, 0)))
    mod = _modulation(c_pad, w_mod, b_mod)[:, :BATCH, :]
    cos_t, sin_t = _rope_tables(positions)
    summat3 = jnp.asarray(np.tile(_gla_sum_matrix(), (1, 3)), dtype=BF16)

    w_main, w_gate = _prep_w_in(w_in)
    w_out_bf = _prep_w_out(w_out)

    x2 = x.reshape(ROWS, D_MODEL)
    for l in range(DEPTH):
        shift = mod[l, :, 0:D_MODEL].reshape(BATCH, 1, D_MODEL)
        scale = mod[l, :, D_MODEL:2 * D_MODEL].reshape(BATCH, 1, D_MODEL)
        gate = mod[l, :, 2 * D_MODEL:].reshape(BATCH, 1, D_MODEL)
        proj_a, proj_b, proj_c, proj_z, proj_g = _inproj(
            l, x2, scale, shift, g_pre[l].reshape(1, D_MODEL), w_main, w_gate)

        y_a = _swa(swa_sinks[l], proj_a, cos_t, sin_t)
        bias_tile = jnp.repeat(sg_b[l].T, SG_GROUP_DIM, axis=1)
        y_b = _spatial_gating(proj_b, sg_w[l], bias_tile,
                              sg_ln_g[l].reshape(1, SG_WIDTH), sg_ln_b[l].reshape(1, SG_WIDTH))
        wup_pad = jnp.pad(gla_w_gate_up[l], ((0, G_WIDTH - GLA_GATE_RANK), (0, 0))).astype(BF16)
        y_c = _gla(proj_c.reshape(BATCH, SEQ, C_WIDTH), proj_g.reshape(BATCH, SEQ, G_WIDTH), wup_pad,
                   gla_b_gate[l].reshape(1, GLA_KEY_WIDTH), summat3,
                   gla_norm_g[l].reshape(1, GLA_DV)).reshape(ROWS, GLA_WIDTH)

        x2 = _outproj(l, y_a, y_b, y_c, proj_z, x2, gate, g_post[l].reshape(1, D_MODEL), w_out_bf)
    return x2.reshape(BATCH, SEQ, D_MODEL)
```

```python
import functools

import numpy as np
import jax
import jax.numpy as jnp
from jax import lax
from jax.experimental import pallas as pl
from jax.experimental.pallas import tpu as pltpu

F32 = jnp.float32
BF16 = jnp.bfloat16

D_MODEL = 2048
BATCH = 4
SEQ = 2048
DEPTH = 2
EPS = 1e-6
ROWS = BATCH * SEQ

SWA_HEAD_DIM = 64
SWA_HEADS = 16
SWA_KV_HEADS = 4
SWA_GROUP = SWA_HEADS // SWA_KV_HEADS
SWA_WIDTH = SWA_HEADS * SWA_HEAD_DIM
SWA_KV_WIDTH = SWA_KV_HEADS * SWA_HEAD_DIM
WINDOW = 128
ROT_DIM = 16
ROT_HALF = ROT_DIM // 2
ROPE_THETA = 500000.0

SG_WIDTH = 512
SG_GROUPS = 8
SG_GROUP_DIM = 64
SG_CHUNK = 128

GLA_HEADS = 4
GLA_WIDTH = 512
GLA_DV = 128
GLA_DK = 64
GLA_KEY_WIDTH = 256
GLA_GATE_RANK = 16
GLA_GATE_TAU = 16.0
GLA_CHUNK = 64
GLA_LEVELS = 6

LANES = 128
A_WIDTH = SWA_WIDTH + 2 * SWA_KV_WIDTH
B_WIDTH = 2 * SG_WIDTH
C_WIDTH = 2 * GLA_KEY_WIDTH + GLA_WIDTH
Z_WIDTH = D_MODEL
G_WIDTH = LANES
W_IN_PAD = A_WIDTH + B_WIDTH + C_WIDTH + Z_WIDTH + G_WIDTH
MOD_WIDTH = 3 * D_MODEL

VMEM_LIMIT = 56 * 1024 * 1024


def _sigmoid(x):
    return 1.0 / (1.0 + jnp.exp(-x))


def _gelu_tanh(x):
    return 0.5 * x * (1.0 + jnp.tanh(0.7978845608028654 * (x + 0.044715 * (x * x * x))))


def _dot_nt(a, b):
    return lax.dot_general(a, b, (((1,), (1,)), ((), ())), preferred_element_type=F32)


def _dot_tn(a, b):
    return lax.dot_general(a, b, (((0,), (0,)), ((), ())), preferred_element_type=F32)


MOD_TN = 768


def _mod_kernel(c_ref, w_ref, b_ref, o_ref):
    c = c_ref[...]
    s = (c * _sigmoid(c)).astype(BF16)
    o_ref[0] = jnp.dot(s, w_ref[0].astype(BF16), preferred_element_type=F32) + b_ref[0]


def _modulation(c_pad, w_mod, b_mod):
    return pl.pallas_call(
        _mod_kernel,
        grid=(DEPTH, MOD_WIDTH // MOD_TN),
        in_specs=[
            pl.BlockSpec((8, D_MODEL), lambda l, j: (0, 0)),
            pl.BlockSpec((1, D_MODEL, MOD_TN), lambda l, j: (l, 0, j)),
            pl.BlockSpec((1, 1, MOD_TN), lambda l, j: (l, 0, j)),
        ],
        out_specs=pl.BlockSpec((1, 8, MOD_TN), lambda l, j: (l, 0, j)),
        out_shape=jax.ShapeDtypeStruct((DEPTH, 8, MOD_WIDTH), F32),
        compiler_params=pltpu.CompilerParams(
            dimension_semantics=("arbitrary", "arbitrary"), vmem_limit_bytes=VMEM_LIMIT),
        name="adaln_mod",
    )(c_pad, w_mod, b_mod.reshape(DEPTH, 1, MOD_WIDTH))


ROPE_TM = 1024


def _rope_table_kernel(pos_ref, invf_ref, cos_ref, sin_ref):
    ang = pos_ref[...].astype(F32) * invf_ref[...]
    lane = lax.broadcasted_iota(jnp.int32, (1, LANES), 1) % SWA_HEAD_DIM
    s = jnp.sin(ang)
    cos_ref[...] = jnp.cos(ang)
    sin_ref[...] = jnp.where(lane < ROT_HALF, -s, s)


def _rope_tables(positions):
    half = np.arange(ROT_HALF, dtype=np.float32)
    inv_freq = (np.float32(ROPE_THETA) ** (-(half * np.float32(2.0 / ROT_DIM)))).astype(np.float32)
    lane = np.arange(LANES) % SWA_HEAD_DIM
    invf = np.where(lane < ROT_DIM, inv_freq[lane % ROT_HALF], 0.0).astype(np.float32)[None, :]
    return pl.pallas_call(
        _rope_table_kernel,
        grid=(ROWS // ROPE_TM,),
        in_specs=[
            pl.BlockSpec((ROPE_TM, 1), lambda i: (i, 0)),
            pl.BlockSpec((1, LANES), lambda i: (0, 0)),
        ],
        out_specs=[pl.BlockSpec((ROPE_TM, LANES), lambda i: (i, 0))] * 2,
        out_shape=[jax.ShapeDtypeStruct((ROWS, LANES), F32)] * 2,
        compiler_params=pltpu.CompilerParams(dimension_semantics=("arbitrary",)),
        name="rope_tables",
    )(positions.reshape(ROWS, 1), jnp.asarray(invf))


PREP_TN = 512
MAIN_WIDTH = A_WIDTH + B_WIDTH + C_WIDTH + Z_WIDTH
GATE_COL = A_WIDTH + B_WIDTH + C_WIDTH
IN_PROJ_WIDTH = GATE_COL + GLA_GATE_RANK + Z_WIDTH


def _prep_w_in_kernel(a_ref, b_ref, main_ref, gate_ref):
    j = pl.program_id(1)
    first_z = GATE_COL // PREP_TN

    @pl.when(j < first_z)
    def _():
        main_ref[0] = a_ref[0].astype(BF16)

    @pl.when(j >= first_z)
    def _():
        main_ref[0] = jnp.concatenate(
            [a_ref[0, GLA_GATE_RANK:, :], b_ref[0, :GLA_GATE_RANK, :]], axis=0).astype(BF16)

    @pl.when(j == first_z)
    def _():
        gate_ref[0] = jnp.concatenate(
            [a_ref[0, :GLA_GATE_RANK, :], jnp.zeros((G_WIDTH - GLA_GATE_RANK, D_MODEL), F32)],
            axis=0).astype(BF16)


def _prep_w_in(w_in_t):
    first_z = GATE_COL // PREP_TN
    last = (IN_PROJ_WIDTH - 1) // PREP_TN
    return pl.pallas_call(
        _prep_w_in_kernel,
        grid=(DEPTH, MAIN_WIDTH // PREP_TN),
        in_specs=[
            pl.BlockSpec((1, PREP_TN, D_MODEL), lambda l, j: (l, j, 0)),
            pl.BlockSpec((1, PREP_TN, D_MODEL), lambda l, j: (l, jnp.clip(j + 1, first_z + 1, last), 0)),
        ],
        out_specs=[
            pl.BlockSpec((1, PREP_TN, D_MODEL), lambda l, j: (l, j, 0)),
            pl.BlockSpec((1, G_WIDTH, D_MODEL), lambda l, j: (l, 0, 0)),
        ],
        out_shape=[jax.ShapeDtypeStruct((DEPTH, MAIN_WIDTH, D_MODEL), BF16),
                   jax.ShapeDtypeStruct((DEPTH, G_WIDTH, D_MODEL), BF16)],
        compiler_params=pltpu.CompilerParams(
            dimension_semantics=("arbitrary", "arbitrary"), vmem_limit_bytes=VMEM_LIMIT),
        name="prep_w_in",
    )(w_in_t, w_in_t)


def _cast_kernel(x_ref, o_ref):
    o_ref[...] = x_ref[...].astype(o_ref.dtype)


def _prep_w_out(w_out):
    tn = 1024
    return pl.pallas_call(
        _cast_kernel,
        grid=(DEPTH, D_MODEL // tn),
        in_specs=[pl.BlockSpec((1, D_MODEL, tn), lambda l, j: (l, 0, j))],
        out_specs=pl.BlockSpec((1, D_MODEL, tn), lambda l, j: (l, 0, j)),
        out_shape=jax.ShapeDtypeStruct((DEPTH, D_MODEL, D_MODEL), BF16),
        compiler_params=pltpu.CompilerParams(
            dimension_semantics=("arbitrary", "arbitrary"), vmem_limit_bytes=VMEM_LIMIT),
        name="prep_w_out",
    )(w_out)


INPROJ_TM = 512
INPROJ_CHUNK = 512


def _spatial_gating_chunk(u, v, w_bf, bias, ln_g, ln_b):
    u = _gelu_tanh(u)
    v = _gelu_tanh(v)
    mu = jnp.mean(v, axis=-1, keepdims=True)
    vc = v - mu
    var = jnp.mean(vc * vc, axis=-1, keepdims=True)
    vn = (vc * lax.rsqrt(var + EPS) * ln_g + ln_b).astype(BF16)
    parts = [jnp.dot(w_bf[g], vn[:, g * SG_GROUP_DIM:(g + 1) * SG_GROUP_DIM], preferred_element_type=F32)
             for g in range(SG_GROUPS)]
    return u * (jnp.concatenate(parts, axis=1) + bias)


def _inproj_kernel(x_ref, scale_ref, shift_ref, g_ref, w_ref, wg_ref, sgw_ref, sgb_ref, lng_ref, lnb_ref,
                   oa_ref, ob_ref, oc_ref, oz_ref, og_ref, h_ref, uv_ref):
    x = x_ref[...]
    ms = jnp.mean(x * x, axis=-1, keepdims=True)
    y = x * lax.rsqrt(ms + EPS) * g_ref[...]
    h_ref[...] = (y * (1.0 + scale_ref[0]) + shift_ref[0]).astype(BF16)

    def project(o_ref, col, c0):
        r = _dot_nt(h_ref[...], w_ref[0, col + c0:col + c0 + INPROJ_CHUNK, :])
        o_ref[:, c0:c0 + INPROJ_CHUNK] = r.astype(o_ref.dtype)

    for c0 in range(0, B_WIDTH, INPROJ_CHUNK):
        project(uv_ref, A_WIDTH, c0)

    t = lax.broadcasted_iota(jnp.int32, (SG_CHUNK, SG_CHUNK), 0)
    s = lax.broadcasted_iota(jnp.int32, (SG_CHUNK, SG_CHUNK), 1)
    w_bf = [jnp.where(t >= s, sgw_ref[g], 0.0).astype(BF16) for g in range(SG_GROUPS)]

    def mixer_b(r0):
        rows = slice(r0, r0 + SG_CHUNK)
        ob_ref[rows, :] = _spatial_gating_chunk(
            uv_ref[rows, :SG_WIDTH], uv_ref[rows, SG_WIDTH:], w_bf, sgb_ref[...], lng_ref[...],
            lnb_ref[...]).astype(ob_ref.dtype)

    pieces = ([(oa_ref, 0, c0) for c0 in range(0, A_WIDTH, INPROJ_CHUNK)]
              + [(oc_ref, A_WIDTH + B_WIDTH, c0) for c0 in range(0, C_WIDTH, INPROJ_CHUNK)]
              + [(oz_ref, A_WIDTH + B_WIDTH + C_WIDTH, c0) for c0 in range(0, Z_WIDTH, INPROJ_CHUNK)])
    chunk_starts = list(range(0, INPROJ_TM, SG_CHUNK))
    for idx, piece in enumerate(pieces):
        project(*piece)
        if idx < len(chunk_starts):
            mixer_b(chunk_starts[idx])
    og_ref[...] = _dot_nt(h_ref[...], wg_ref[0])


def _inproj(layer, x2, scale, shift, g_pre, w_main, w_gate, sg_w, sg_bias_tile, sg_ln_g, sg_ln_b):
    tiles_per_batch = SEQ // INPROJ_TM
    widths = (A_WIDTH, SG_WIDTH, C_WIDTH, Z_WIDTH, G_WIDTH)
    dtypes = (BF16, BF16, BF16, BF16, F32)
    return pl.pallas_call(
        _inproj_kernel,
        grid=(ROWS // INPROJ_TM,),
        in_specs=[
            pl.BlockSpec((INPROJ_TM, D_MODEL), lambda i: (i, 0)),
            pl.BlockSpec((1, 1, D_MODEL), lambda i: (i // tiles_per_batch, 0, 0)),
            pl.BlockSpec((1, 1, D_MODEL), lambda i: (i // tiles_per_batch, 0, 0)),
            pl.BlockSpec((1, D_MODEL), lambda i: (0, 0)),
            pl.BlockSpec((1, MAIN_WIDTH, D_MODEL), lambda i: (layer, 0, 0), pipeline_mode=pl.Buffered(1)),
            pl.BlockSpec((1, G_WIDTH, D_MODEL), lambda i: (layer, 0, 0), pipeline_mode=pl.Buffered(1)),
            pl.BlockSpec((SG_GROUPS, SG_CHUNK, SG_CHUNK), lambda i: (0, 0, 0)),
            pl.BlockSpec((SG_CHUNK, SG_WIDTH), lambda i: (0, 0)),
            pl.BlockSpec((1, SG_WIDTH), lambda i: (0, 0)),
            pl.BlockSpec((1, SG_WIDTH), lambda i: (0, 0)),
        ],
        out_specs=[pl.BlockSpec((INPROJ_TM, w), lambda i: (i, 0)) for w in widths],
        out_shape=[jax.ShapeDtypeStruct((ROWS, w), dt) for w, dt in zip(widths, dtypes)],
        scratch_shapes=[pltpu.VMEM((INPROJ_TM, D_MODEL), BF16), pltpu.VMEM((INPROJ_TM, B_WIDTH), F32)],
        compiler_params=pltpu.CompilerParams(
            dimension_semantics=("arbitrary",), vmem_limit_bytes=VMEM_LIMIT),
        name="prenorm_inproj",
    )(x2, scale, shift, g_pre, w_main, w_gate, sg_w, sg_bias_tile, sg_ln_g, sg_ln_b)


def _swa_kernel(sinks_ref, q_ref, k_ref, v_ref, kp_ref, vp_ref,
                cos_ref, sin_ref, cosp_ref, sinp_ref, o_ref, s_ref, p_ref):
    n = pl.program_id(1)
    lane = lax.broadcasted_iota(jnp.int32, (1, LANES), 1) % SWA_HEAD_DIM
    first_half = lane < ROT_HALF

    def rope(t, c, s):
        partner = jnp.where(first_half, pltpu.roll(t, LANES - ROT_HALF, 1), pltpu.roll(t, ROT_HALF, 1))
        return t * c + partner * s

    cos_c, sin_c = cos_ref[...], sin_ref[...]
    cos_b = jnp.concatenate([cosp_ref[...], cos_c], axis=0)
    sin_b = jnp.concatenate([sinp_ref[...], sin_c], axis=0)
    low_half = lax.broadcasted_iota(jnp.int32, (1, LANES), 1) < SWA_HEAD_DIM

    q_scale = SWA_HEAD_DIM ** -0.5
    q_tiles = []
    for t in range(SWA_WIDTH // LANES):
        qt = q_ref[:, t * LANES:(t + 1) * LANES].astype(F32) * q_scale
        q_tiles.append(rope(qt, cos_c, sin_c).astype(BF16))

    k_sel, v_dup = [], []
    for t in range(SWA_KV_WIDTH // LANES):
        cols = slice(t * LANES, (t + 1) * LANES)
        kt = rope(jnp.concatenate([kp_ref[:, cols], k_ref[:, cols]], axis=0).astype(F32), cos_b, sin_b)
        kt_sw = pltpu.roll(kt, SWA_HEAD_DIM, 1)
        vt = jnp.concatenate([vp_ref[:, cols], v_ref[:, cols]], axis=0).astype(F32)
        vt_sw = pltpu.roll(vt, SWA_HEAD_DIM, 1)
        for src_lo, src_hi, v_lo, v_hi in ((kt, kt_sw, vt, vt_sw), (kt_sw, kt, vt_sw, vt)):
            k_sel.append((jnp.where(low_half, src_lo, 0.0).astype(BF16),
                          jnp.where(low_half, 0.0, src_hi).astype(BF16)))
            v_dup.append(jnp.where(low_half, v_lo, v_hi).astype(BF16))

    qi = lax.broadcasted_iota(jnp.int32, (WINDOW, WINDOW), 0)
    kj = lax.broadcasted_iota(jnp.int32, (WINDOW, WINDOW), 1)
    from_prev = kj > qi
    valid = (kj <= qi) | (n > 0)

    heads_per_tile = LANES // SWA_HEAD_DIM
    for h in range(SWA_HEADS):
        t, u = divmod(h, heads_per_tile)
        s2 = _dot_nt(q_tiles[t], k_sel[h // SWA_GROUP][u])
        s_ref[h] = jnp.where(from_prev, s2[:, :WINDOW], s2[:, WINDOW:])
    for h in range(SWA_HEADS):
        s = jnp.where(valid, s_ref[h], -jnp.inf)
        sink = sinks_ref[h]
        m = jnp.maximum(jnp.max(s, axis=-1, keepdims=True), sink)
        p = jnp.exp(s - m)
        denom = jnp.sum(p, axis=-1, keepdims=True) + jnp.exp(sink - m)
        p = (p * (1.0 / denom)).astype(BF16)
        zero = jnp.zeros_like(p)
        p_ref[h, :, :WINDOW] = jnp.where(from_prev, p, zero)
        p_ref[h, :, WINDOW:] = jnp.where(from_prev, zero, p)
    for t in range(SWA_WIDTH // LANES):
        vg = v_dup[(t * heads_per_tile) // SWA_GROUP]
        outs = [jnp.dot(p_ref[t * heads_per_tile + u], vg, preferred_element_type=F32)
                for u in range(heads_per_tile)]
        o_ref[:, t * LANES:(t + 1) * LANES] = jnp.where(low_half, outs[0], outs[1]).astype(o_ref.dtype)


def _swa(sinks, proj_a, cos_t, sin_t):
    nb = SEQ // WINDOW
    kcol = SWA_WIDTH // SWA_KV_WIDTH
    cur = lambda b, n: b * nb + n
    prev = lambda b, n: b * nb + jnp.maximum(n - 1, 0)
    return pl.pallas_call(
        _swa_kernel,
        grid=(BATCH, nb),
        in_specs=[
            pl.BlockSpec(memory_space=pltpu.SMEM),
            pl.BlockSpec((WINDOW, SWA_WIDTH), lambda b, n: (cur(b, n), 0)),
            pl.BlockSpec((WINDOW, SWA_KV_WIDTH), lambda b, n: (cur(b, n), kcol)),
            pl.BlockSpec((WINDOW, SWA_KV_WIDTH), lambda b, n: (cur(b, n), kcol + 1)),
            pl.BlockSpec((WINDOW, SWA_KV_WIDTH), lambda b, n: (prev(b, n), kcol)),
            pl.BlockSpec((WINDOW, SWA_KV_WIDTH), lambda b, n: (prev(b, n), kcol + 1)),
            pl.BlockSpec((WINDOW, LANES), lambda b, n: (cur(b, n), 0)),
            pl.BlockSpec((WINDOW, LANES), lambda b, n: (cur(b, n), 0)),
            pl.BlockSpec((WINDOW, LANES), lambda b, n: (prev(b, n), 0)),
            pl.BlockSpec((WINDOW, LANES), lambda b, n: (prev(b, n), 0)),
        ],
        out_specs=pl.BlockSpec((WINDOW, SWA_WIDTH), lambda b, n: (cur(b, n), 0)),
        out_shape=jax.ShapeDtypeStruct((ROWS, SWA_WIDTH), BF16),
        scratch_shapes=[pltpu.VMEM((SWA_HEADS, WINDOW, WINDOW), F32),
                        pltpu.VMEM((SWA_HEADS, WINDOW, 2 * WINDOW), BF16)],
        compiler_params=pltpu.CompilerParams(dimension_semantics=("arbitrary", "arbitrary")),
        name="swa",
    )(sinks, proj_a, proj_a, proj_a, proj_a, proj_a, cos_t, sin_t, cos_t, sin_t)


def _gla_sum_matrix():
    c = GLA_CHUNK
    mat = np.zeros(((2 + GLA_LEVELS) * c, c), np.float32)
    for t in range(c):
        mat[t, :t + 1] = 1.0
        mat[c + t, t + 1:] = 1.0
        for k in range(GLA_LEVELS):
            m = 1 << k
            r = (t >> (k + 1) << (k + 1)) + m
            row = (2 + k) * c + t
            if (t >> k) & 1:
                mat[row, r + 1:t + 1] = 1.0
            else:
                mat[row, t + 1:r + 1] = 1.0
    return mat


def _split_heads_on_rows(x, low_half):
    zero = jnp.zeros_like(x)
    return jnp.concatenate([jnp.where(low_half, x, zero), jnp.where(low_half, zero, x)], axis=0)


def _gla_kernel(q_ref, k_ref, v_ref, cg_ref, wup_ref, bg_ref, summat_ref, ng_ref, o_ref, st_ref):
    c = GLA_CHUNK
    heads_per_tile = LANES // GLA_DK
    n_tiles = BATCH * GLA_KEY_WIDTH // LANES

    @pl.when(pl.program_id(0) == 0)
    def _():
        st_ref[...] = jnp.zeros_like(st_ref)

    cg = cg_ref[...].reshape(BATCH * c, G_WIDTH).astype(BF16)
    logits = jnp.dot(cg, wup_ref[...], preferred_element_type=F32) + bg_ref[...]
    log_alpha = (jnp.minimum(logits, 0.0) - jnp.log1p(jnp.exp(-jnp.abs(logits)))) * (1.0 / GLA_GATE_TAU)

    hi = log_alpha.astype(BF16)
    r1 = log_alpha - hi.astype(F32)
    mid = r1.astype(BF16)
    lo = (r1 - mid.astype(F32)).astype(BF16)
    pieces = jnp.concatenate(
        [jnp.concatenate([p[b * c:(b + 1) * c] for p in (hi, mid, lo)], axis=0) for b in range(BATCH)],
        axis=1)
    expo = jnp.dot(summat_ref[...], pieces, preferred_element_type=F32)

    q = jnp.concatenate([q_ref[b] for b in range(BATCH)], axis=1).astype(F32) * (GLA_DK ** -0.5)
    k = jnp.concatenate([k_ref[b] for b in range(BATCH)], axis=1).astype(F32)
    b_cum = expo[0:c]
    q_inter = (q * jnp.exp(b_cum)).astype(BF16)
    k_state = (k * jnp.exp(expo[c:2 * c])).astype(BF16)
    decay = jnp.exp(b_cum[c - 1:c, :])

    t_i = lax.broadcasted_iota(jnp.int32, (heads_per_tile * c, c), 0) % c
    s_i = lax.broadcasted_iota(jnp.int32, (heads_per_tile * c, c), 1)
    q_lv, k_lv, masks = [q.astype(BF16)], [k.astype(BF16)], [t_i == s_i]
    for lv in range(GLA_LEVELS):
        f = jnp.exp(expo[(2 + lv) * c:(3 + lv) * c])
        q_lv.append((q * f).astype(BF16))
        k_lv.append((k * f).astype(BF16))
        masks.append(((t_i >> (lv + 1)) == (s_i >> (lv + 1)))
                     & (((t_i >> lv) & 1) == 1) & (((s_i >> lv) & 1) == 0))

    low_half = lax.broadcasted_iota(jnp.int32, (1, LANES), 1) < GLA_DK
    scores = []
    for j in range(n_tiles):
        sl = slice(j * LANES, (j + 1) * LANES)
        a2 = jnp.zeros((heads_per_tile * c, c), F32)
        for ql, kl, mask in zip(q_lv, k_lv, masks):
            a2 = a2 + jnp.where(mask, _dot_nt(_split_heads_on_rows(ql[:, sl], low_half), kl[:, sl]), 0.0)
        scores.append(a2.astype(BF16))

    tiles_per_seq = GLA_KEY_WIDTH // LANES
    for j in range(n_tiles):
        b, pair = divmod(j, tiles_per_seq)
        sl = slice(j * LANES, (j + 1) * LANES)
        st_cols = slice(pair * LANES, (pair + 1) * LANES)
        st = st_ref[b, :, st_cols]
        inter = _dot_nt(_split_heads_on_rows(q_inter[:, sl], low_half), st.astype(BF16))
        updates = []
        for u in range(heads_per_tile):
            h = pair * heads_per_tile + u
            vs = slice(h * GLA_DV, (h + 1) * GLA_DV)
            vh = v_ref[b, :, vs]
            o = jnp.dot(scores[j][u * c:(u + 1) * c], vh, preferred_element_type=F32) + inter[u * c:(u + 1) * c]
            y = o * lax.rsqrt(jnp.mean(o * o, axis=-1, keepdims=True) + EPS) * ng_ref[...]
            o_ref[b, :, vs] = y.astype(o_ref.dtype)
            updates.append(_dot_tn(vh, k_state[:, sl]))
        st_ref[b, :, st_cols] = st * decay[:, sl] + jnp.where(low_half, updates[0], updates[1])


def _gla(proj_c, proj_g, wup_pad, b_gate, summat3, norm_g):
    return pl.pallas_call(
        _gla_kernel,
        grid=(SEQ // GLA_CHUNK,),
        in_specs=[
            pl.BlockSpec((BATCH, GLA_CHUNK, GLA_KEY_WIDTH), lambda c: (0, c, 0)),
            pl.BlockSpec((BATCH, GLA_CHUNK, GLA_KEY_WIDTH), lambda c: (0, c, 1)),
            pl.BlockSpec((BATCH, GLA_CHUNK, GLA_WIDTH), lambda c: (0, c, 1)),
            pl.BlockSpec((BATCH, GLA_CHUNK, G_WIDTH), lambda c: (0, c, 0)),
            pl.BlockSpec((G_WIDTH, GLA_KEY_WIDTH), lambda c: (0, 0)),
            pl.BlockSpec((1, GLA_KEY_WIDTH), lambda c: (0, 0)),
            pl.BlockSpec(((2 + GLA_LEVELS) * GLA_CHUNK, 3 * GLA_CHUNK), lambda c: (0, 0)),
            pl.BlockSpec((1, GLA_DV), lambda c: (0, 0)),
        ],
        out_specs=pl.BlockSpec((BATCH, GLA_CHUNK, GLA_WIDTH), lambda c: (0, c, 0)),
        out_shape=jax.ShapeDtypeStruct((BATCH, SEQ, GLA_WIDTH), BF16),
        scratch_shapes=[pltpu.VMEM((BATCH, GLA_DV, GLA_KEY_WIDTH), F32)],
        compiler_params=pltpu.CompilerParams(dimension_semantics=("arbitrary",)),
        name="gla",
    )(proj_c, proj_c, proj_c, proj_g, wup_pad, b_gate, summat3, norm_g)


OUT_TM = 512


def _outproj_kernel(ya_ref, yb_ref, yc_ref, z_ref, x_ref, gate_ref, g_ref, w_ref, o_ref):
    acc = None
    col = 0
    for y_ref in (ya_ref, yb_ref, yc_ref):
        width = y_ref.shape[1]
        z = z_ref[:, col:col + width].astype(F32)
        y = (y_ref[...].astype(F32) * (z * _sigmoid(z))).astype(BF16)
        part = jnp.dot(y, w_ref[0, col:col + width, :], preferred_element_type=F32)
        acc = part if acc is None else acc + part
        col += width
    normed = acc * lax.rsqrt(jnp.mean(acc * acc, axis=-1, keepdims=True) + EPS) * g_ref[...]
    o_ref[...] = x_ref[...] + gate_ref[0] * normed


def _outproj(layer, y_a, y_b, y_c, z, x2, gate, g_post, w_out_bf):
    tiles_per_batch = SEQ // OUT_TM
    return pl.pallas_call(
        _outproj_kernel,
        grid=(ROWS // OUT_TM,),
        in_specs=[
            pl.BlockSpec((OUT_TM, SWA_WIDTH), lambda i: (i, 0)),
            pl.BlockSpec((OUT_TM, SG_WIDTH), lambda i: (i, 0)),
            pl.BlockSpec((OUT_TM, GLA_WIDTH), lambda i: (i, 0)),
            pl.BlockSpec((OUT_TM, Z_WIDTH), lambda i: (i, 0)),
            pl.BlockSpec((OUT_TM, D_MODEL), lambda i: (i, 0)),
            pl.BlockSpec((1, 1, D_MODEL), lambda i: (i // tiles_per_batch, 0, 0)),
            pl.BlockSpec((1, D_MODEL), lambda i: (0, 0)),
            pl.BlockSpec((1, D_MODEL, D_MODEL), lambda i: (layer, 0, 0), pipeline_mode=pl.Buffered(1)),
        ],
        out_specs=pl.BlockSpec((OUT_TM, D_MODEL), lambda i: (i, 0)),
        out_shape=jax.ShapeDtypeStruct((ROWS, D_MODEL), F32),
        compiler_params=pltpu.CompilerParams(
            dimension_semantics=("arbitrary",), vmem_limit_bytes=VMEM_LIMIT),
        name="gate_outproj_residual",
    )(y_a, y_b, y_c, z, x2, gate, g_post, w_out_bf)


def kernel(x, c, positions, w_mod, b_mod, g_pre, g_post, w_in, w_out, swa_sinks,
           sg_w, sg_b, sg_ln_g, sg_ln_b, gla_w_gate_up, gla_b_gate, gla_norm_g):
    assert x.shape == (BATCH, SEQ, D_MODEL) and w_in.shape[0] == DEPTH

    c_pad = jnp.pad(c, ((0, 8 - BATCH), (0, 0)))
    mod = _modulation(c_pad, w_mod, b_mod)[:, :BATCH, :]
    cos_t, sin_t = _rope_tables(positions)
    summat3 = jnp.asarray(np.tile(_gla_sum_matrix(), (1, 3)), dtype=BF16)

    w_main, w_gate = _prep_w_in(jnp.swapaxes(w_in, 1, 2))
    w_out_bf = _prep_w_out(w_out)

    x2 = x.reshape(ROWS, D_MODEL)
    for l in range(DEPTH):
        shift = mod[l, :, 0:D_MODEL].reshape(BATCH, 1, D_MODEL)
        scale = mod[l, :, D_MODEL:2 * D_MODEL].reshape(BATCH, 1, D_MODEL)
        gate = mod[l, :, 2 * D_MODEL:].reshape(BATCH, 1, D_MODEL)
        bias_tile = jnp.repeat(sg_b[l].T, SG_GROUP_DIM, axis=1)
        proj_a, y_b, proj_c, proj_z, proj_g = _inproj(
            l, x2, scale, shift, g_pre[l].reshape(1, D_MODEL), w_main, w_gate, sg_w[l], bias_tile,
            sg_ln_g[l].reshape(1, SG_WIDTH), sg_ln_b[l].reshape(1, SG_WIDTH))

        y_a = _swa(swa_sinks[l], proj_a, cos_t, sin_t)
        wup_pad = jnp.pad(gla_w_gate_up[l], ((0, G_WIDTH - GLA_GATE_RANK), (0, 0))).astype(BF16)
        y_c = _gla(proj_c.reshape(BATCH, SEQ, C_WIDTH), proj_g.reshape(BATCH, SEQ, G_WIDTH), wup_pad,
                   gla_b_gate[l].reshape(1, GLA_KEY_WIDTH), summat3,
                   gla_norm_g[l].reshape(1, GLA_DV)).reshape(ROWS, GLA_WIDTH)

        x2 = _outproj(l, y_a, y_b, y_c, proj_z, x2, gate, g_post[l].reshape(1, D_MODEL), w_out_bf)
    return x2.reshape(BATCH, SEQ, D_MODEL)
```

```python
import functools

import numpy as np
import jax
import jax.numpy as jnp
from jax import lax
from jax.experimental import pallas as pl
from jax.experimental.pallas import tpu as pltpu

F32 = jnp.float32
BF16 = jnp.bfloat16

D_MODEL = 2048
BATCH = 4
SEQ = 2048
DEPTH = 2
EPS = 1e-6
ROWS = BATCH * SEQ

SWA_HEAD_DIM = 64
SWA_HEADS = 16
SWA_KV_HEADS = 4
SWA_GROUP = SWA_HEADS // SWA_KV_HEADS
SWA_WIDTH = SWA_HEADS * SWA_HEAD_DIM
SWA_KV_WIDTH = SWA_KV_HEADS * SWA_HEAD_DIM
WINDOW = 128
ROT_DIM = 16
ROT_HALF = ROT_DIM // 2
ROPE_THETA = 500000.0

SG_WIDTH = 512
SG_GROUPS = 8
SG_GROUP_DIM = 64
SG_CHUNK = 128

GLA_HEADS = 4
GLA_WIDTH = 512
GLA_DV = 128
GLA_DK = 64
GLA_KEY_WIDTH = 256
GLA_GATE_RANK = 16
GLA_GATE_TAU = 16.0
GLA_CHUNK = 64
GLA_LEVELS = 6

LANES = 128
A_WIDTH = SWA_WIDTH + 2 * SWA_KV_WIDTH
B_WIDTH = 2 * SG_WIDTH
C_WIDTH = 2 * GLA_KEY_WIDTH + GLA_WIDTH
Z_WIDTH = D_MODEL
G_WIDTH = LANES
W_IN_PAD = A_WIDTH + B_WIDTH + C_WIDTH + Z_WIDTH + G_WIDTH
MOD_WIDTH = 3 * D_MODEL

VMEM_LIMIT = 56 * 1024 * 1024
FRONT_VMEM_LIMIT = 60 * 1024 * 1024


def _sigmoid(x):
    return 1.0 / (1.0 + jnp.exp(-x))


def _gelu_tanh(x):
    return 0.5 * x * (1.0 + jnp.tanh(0.7978845608028654 * (x + 0.044715 * (x * x * x))))


def _dot_nt(a, b):
    return lax.dot_general(a, b, (((1,), (1,)), ((), ())), preferred_element_type=F32)


def _dot_tn(a, b):
    return lax.dot_general(a, b, (((0,), (0,)), ((), ())), preferred_element_type=F32)


MOD_TN = 768


def _mod_kernel(c_ref, w_ref, b_ref, o_ref):
    c = c_ref[...]
    s = (c * _sigmoid(c)).astype(BF16)
    o_ref[0] = jnp.dot(s, w_ref[0].astype(BF16), preferred_element_type=F32) + b_ref[0]


def _modulation(c_pad, w_mod, b_mod):
    return pl.pallas_call(
        _mod_kernel,
        grid=(DEPTH, MOD_WIDTH // MOD_TN),
        in_specs=[
            pl.BlockSpec((8, D_MODEL), lambda l, j: (0, 0)),
            pl.BlockSpec((1, D_MODEL, MOD_TN), lambda l, j: (l, 0, j)),
            pl.BlockSpec((1, 1, MOD_TN), lambda l, j: (l, 0, j)),
        ],
        out_specs=pl.BlockSpec((1, 8, MOD_TN), lambda l, j: (l, 0, j)),
        out_shape=jax.ShapeDtypeStruct((DEPTH, 8, MOD_WIDTH), F32),
        compiler_params=pltpu.CompilerParams(
            dimension_semantics=("arbitrary", "arbitrary"), vmem_limit_bytes=VMEM_LIMIT),
        name="adaln_mod",
    )(c_pad, w_mod, b_mod.reshape(DEPTH, 1, MOD_WIDTH))


ROPE_TM = 1024


def _rope_table_kernel(pos_ref, invf_ref, cos_ref, sin_ref):
    ang = pos_ref[...].astype(F32) * invf_ref[...]
    lane = lax.broadcasted_iota(jnp.int32, (1, LANES), 1) % SWA_HEAD_DIM
    s = jnp.sin(ang)
    cos_ref[...] = jnp.cos(ang)
    sin_ref[...] = jnp.where(lane < ROT_HALF, -s, s)


def _rope_tables(positions):
    half = np.arange(ROT_HALF, dtype=np.float32)
    inv_freq = (np.float32(ROPE_THETA) ** (-(half * np.float32(2.0 / ROT_DIM)))).astype(np.float32)
    lane = np.arange(LANES) % SWA_HEAD_DIM
    invf = np.where(lane < ROT_DIM, inv_freq[lane % ROT_HALF], 0.0).astype(np.float32)[None, :]
    return pl.pallas_call(
        _rope_table_kernel,
        grid=(ROWS // ROPE_TM,),
        in_specs=[
            pl.BlockSpec((ROPE_TM, 1), lambda i: (i, 0)),
            pl.BlockSpec((1, LANES), lambda i: (0, 0)),
        ],
        out_specs=[pl.BlockSpec((ROPE_TM, LANES), lambda i: (i, 0))] * 2,
        out_shape=[jax.ShapeDtypeStruct((ROWS, LANES), F32)] * 2,
        compiler_params=pltpu.CompilerParams(dimension_semantics=("arbitrary",)),
        name="rope_tables",
    )(positions.reshape(ROWS, 1), jnp.asarray(invf))


PREP_TN = 512
MAIN_WIDTH = A_WIDTH + B_WIDTH + C_WIDTH + Z_WIDTH
GATE_COL = A_WIDTH + B_WIDTH + C_WIDTH
IN_PROJ_WIDTH = GATE_COL + GLA_GATE_RANK + Z_WIDTH


def _prep_w_in_kernel(a_ref, b_ref, main_ref, gate_ref):
    j = pl.program_id(1)
    first_z = GATE_COL // PREP_TN

    @pl.when(j < first_z)
    def _():
        main_ref[0] = a_ref[0].astype(BF16)

    @pl.when(j >= first_z)
    def _():
        main_ref[0] = jnp.concatenate(
            [a_ref[0, GLA_GATE_RANK:, :], b_ref[0, :GLA_GATE_RANK, :]], axis=0).astype(BF16)

    @pl.when(j == first_z)
    def _():
        gate_ref[0] = jnp.concatenate(
            [a_ref[0, :GLA_GATE_RANK, :], jnp.zeros((G_WIDTH - GLA_GATE_RANK, D_MODEL), F32)],
            axis=0).astype(BF16)


def _prep_w_in(w_in_t):
    first_z = GATE_COL // PREP_TN
    last = (IN_PROJ_WIDTH - 1) // PREP_TN
    return pl.pallas_call(
        _prep_w_in_kernel,
        grid=(DEPTH, MAIN_WIDTH // PREP_TN),
        in_specs=[
            pl.BlockSpec((1, PREP_TN, D_MODEL), lambda l, j: (l, j, 0)),
            pl.BlockSpec((1, PREP_TN, D_MODEL), lambda l, j: (l, jnp.clip(j + 1, first_z + 1, last), 0)),
        ],
        out_specs=[
            pl.BlockSpec((1, PREP_TN, D_MODEL), lambda l, j: (l, j, 0)),
            pl.BlockSpec((1, G_WIDTH, D_MODEL), lambda l, j: (l, 0, 0)),
        ],
        out_shape=[jax.ShapeDtypeStruct((DEPTH, MAIN_WIDTH, D_MODEL), BF16),
                   jax.ShapeDtypeStruct((DEPTH, G_WIDTH, D_MODEL), BF16)],
        compiler_params=pltpu.CompilerParams(
            dimension_semantics=("arbitrary", "arbitrary"), vmem_limit_bytes=VMEM_LIMIT),
        name="prep_w_in",
    )(w_in_t, w_in_t)


def _cast_kernel(x_ref, o_ref):
    o_ref[...] = x_ref[...].astype(o_ref.dtype)


def _prep_w_out(w_out):
    tn = 1024
    return pl.pallas_call(
        _cast_kernel,
        grid=(DEPTH, D_MODEL // tn),
        in_specs=[pl.BlockSpec((1, D_MODEL, tn), lambda l, j: (l, 0, j))],
        out_specs=pl.BlockSpec((1, D_MODEL, tn), lambda l, j: (l, 0, j)),
        out_shape=jax.ShapeDtypeStruct((DEPTH, D_MODEL, D_MODEL), BF16),
        compiler_params=pltpu.CompilerParams(
            dimension_semantics=("arbitrary", "arbitrary"), vmem_limit_bytes=VMEM_LIMIT),
        name="prep_w_out",
    )(w_out)


INPROJ_TM = 512
INPROJ_CHUNK = 512


def _spatial_gating_chunk(u, v, w_bf, bias, ln_g, ln_b):
    u = _gelu_tanh(u)
    v = _gelu_tanh(v)
    mu = jnp.mean(v, axis=-1, keepdims=True)
    vc = v - mu
    var = jnp.mean(vc * vc, axis=-1, keepdims=True)
    vn = (vc * lax.rsqrt(var + EPS) * ln_g + ln_b).astype(BF16)
    parts = [jnp.dot(w_bf[g], vn[:, g * SG_GROUP_DIM:(g + 1) * SG_GROUP_DIM], preferred_element_type=F32)
             for g in range(SG_GROUPS)]
    return u * (jnp.concatenate(parts, axis=1) + bias)


def _swa_pieces(tile_has_prev, a_ref, cos_ref, sin_ref, sinks_ref, ya_ref,
                kband_ref, vband_ref, qr_ref, ksel_ref, vdup_ref, s_ref, p_ref):
    heads_per_tile = LANES // SWA_HEAD_DIM
    n_blocks = INPROJ_TM // WINDOW
    lane = lax.broadcasted_iota(jnp.int32, (1, LANES), 1)
    first_half = (lane % SWA_HEAD_DIM) < ROT_HALF
    low_half = lane < SWA_HEAD_DIM
    qi = lax.broadcasted_iota(jnp.int32, (WINDOW, WINDOW), 0)
    kj = lax.broadcasted_iota(jnp.int32, (WINDOW, WINDOW), 1)
    from_prev = kj > qi

    def rope(t):
        partner = jnp.where(first_half, pltpu.roll(t, LANES - ROT_HALF, 1), pltpu.roll(t, ROT_HALF, 1))
        return t * cos_ref[...] + partner * sin_ref[...]

    steps = []

    def carry():
        kband_ref[0:WINDOW, :] = kband_ref[INPROJ_TM:INPROJ_TM + WINDOW, :]
        vband_ref[0:WINDOW, :] = vband_ref[INPROJ_TM:INPROJ_TM + WINDOW, :]
    steps.append(carry)

    def stage_k(t):
        cols = slice(t * LANES, (t + 1) * LANES)
        kband_ref[WINDOW:, cols] = rope(a_ref[:, SWA_WIDTH + t * LANES:SWA_WIDTH + (t + 1) * LANES])
        vband_ref[WINDOW:, cols] = a_ref[:, SWA_WIDTH + SWA_KV_WIDTH + t * LANES:
                                         SWA_WIDTH + SWA_KV_WIDTH + (t + 1) * LANES]
    for t in range(SWA_KV_WIDTH // LANES):
        steps.append(functools.partial(stage_k, t))

    def stage_q(t):
        cols = slice(t * LANES, (t + 1) * LANES)
        qr_ref[:, cols] = rope(a_ref[:, cols] * (SWA_HEAD_DIM ** -0.5)).astype(BF16)
    for t in range(SWA_WIDTH // LANES):
        steps.append(functools.partial(stage_q, t))

    def stage_kv(blk, t):
        rows = slice(blk * WINDOW, (blk + 2) * WINDOW)
        cols = slice(t * LANES, (t + 1) * LANES)
        kt = kband_ref[rows, cols]
        vt = vband_ref[rows, cols]
        kt_sw = pltpu.roll(kt, SWA_HEAD_DIM, 1)
        vt_sw = pltpu.roll(vt, SWA_HEAD_DIM, 1)
        for u, (k_lo, k_hi, v_lo, v_hi) in enumerate(((kt, kt_sw, vt, vt_sw), (kt_sw, kt, vt_sw, vt))):
            g = t * heads_per_tile + u
            ksel_ref[2 * g] = jnp.where(low_half, k_lo, 0.0).astype(BF16)
            ksel_ref[2 * g + 1] = jnp.where(low_half, 0.0, k_hi).astype(BF16)
            vdup_ref[g] = jnp.where(low_half, v_lo, v_hi).astype(BF16)

    def scores(blk, h):
        t, u = divmod(h, heads_per_tile)
        q = qr_ref[blk * WINDOW:(blk + 1) * WINDOW, t * LANES:(t + 1) * LANES]
        s2 = _dot_nt(q, ksel_ref[2 * (h // SWA_GROUP) + u])
        s_ref[h] = jnp.where(from_prev, s2[:, :WINDOW], s2[:, WINDOW:])

    def softmax(blk, h):
        s = s_ref[h]
        if blk == 0:
            s = jnp.where(kj <= qi + tile_has_prev * WINDOW, s, -jnp.inf)
        sink = sinks_ref[h]
        m = jnp.maximum(jnp.max(s, axis=-1, keepdims=True), sink)
        p = jnp.exp(s - m)
        denom = jnp.sum(p, axis=-1, keepdims=True) + jnp.exp(sink - m)
        p = (p * (1.0 / denom)).astype(BF16)
        zero = jnp.zeros_like(p)
        p_ref[h, :, :WINDOW] = jnp.where(from_prev, p, zero)
        p_ref[h, :, WINDOW:] = jnp.where(from_prev, zero, p)

    def values(blk, t):
        vg = vdup_ref[(t * heads_per_tile) // SWA_GROUP]
        outs = [jnp.dot(p_ref[t * heads_per_tile + u], vg, preferred_element_type=F32)
                for u in range(heads_per_tile)]
        ya_ref[blk * WINDOW:(blk + 1) * WINDOW, t * LANES:(t + 1) * LANES] = (
            jnp.where(low_half, outs[0], outs[1]).astype(ya_ref.dtype))

    for blk in range(n_blocks):
        for t in range(SWA_KV_WIDTH // LANES):
            steps.append(functools.partial(stage_kv, blk, t))
        for h in range(SWA_HEADS):
            steps.append(functools.partial(scores, blk, h))
        for h in range(SWA_HEADS):
            steps.append(functools.partial(softmax, blk, h))
        for t in range(SWA_WIDTH // LANES):
            steps.append(functools.partial(values, blk, t))
    return steps


def _interleave(main_steps, side_steps):
    done = 0
    for idx, step in enumerate(main_steps):
        step()
        upto = (idx + 1) * len(side_steps) // len(main_steps)
        for side in side_steps[done:upto]:
            side()
        done = upto


def _front_kernel(sinks_ref, x_ref, scale_ref, shift_ref, g_ref, w_ref, wg_ref,
                  sgw_ref, sgb_ref, lng_ref, lnb_ref, cos_ref, sin_ref,
                  ya_ref, yb_ref, oc_ref, oz_ref, og_ref,
                  h_ref, uv_ref, a_ref, kband_ref, vband_ref, qr_ref, ksel_ref, vdup_ref, s_ref, p_ref):
    i = pl.program_id(0)

    @pl.when(i == 0)
    def _():
        kband_ref[...] = jnp.zeros_like(kband_ref)
        vband_ref[...] = jnp.zeros_like(vband_ref)

    x = x_ref[...]
    ms = jnp.mean(x * x, axis=-1, keepdims=True)
    y = x * lax.rsqrt(ms + EPS) * g_ref[...]
    h_ref[...] = (y * (1.0 + scale_ref[0]) + shift_ref[0]).astype(BF16)

    def project(o_ref, col, c0, width):
        r = _dot_nt(h_ref[...], w_ref[0, col + c0:col + c0 + width, :])
        o_ref[:, c0:c0 + width] = r.astype(o_ref.dtype)

    def projection_steps(o_ref, col):
        return [functools.partial(project, o_ref, col, c0, INPROJ_CHUNK)
                for c0 in range(0, o_ref.shape[1], INPROJ_CHUNK)]

    for step in projection_steps(uv_ref, A_WIDTH):
        step()

    t = lax.broadcasted_iota(jnp.int32, (SG_CHUNK, SG_CHUNK), 0)
    s = lax.broadcasted_iota(jnp.int32, (SG_CHUNK, SG_CHUNK), 1)
    w_bf = [jnp.where(t >= s, sgw_ref[g], 0.0).astype(BF16) for g in range(SG_GROUPS)]

    def mixer_b(r0):
        rows = slice(r0, r0 + SG_CHUNK)
        yb_ref[rows, :] = _spatial_gating_chunk(
            uv_ref[rows, :SG_WIDTH], uv_ref[rows, SG_WIDTH:], w_bf, sgb_ref[...], lng_ref[...],
            lnb_ref[...]).astype(yb_ref.dtype)

    _interleave(projection_steps(a_ref, 0),
                [functools.partial(mixer_b, r0) for r0 in range(0, INPROJ_TM, SG_CHUNK)])

    tile_has_prev = ((i % (SEQ // INPROJ_TM)) != 0).astype(jnp.int32)
    swa_steps = _swa_pieces(tile_has_prev, a_ref, cos_ref, sin_ref, sinks_ref, ya_ref,
                            kband_ref, vband_ref, qr_ref, ksel_ref, vdup_ref, s_ref, p_ref)

    def gate_rank():
        og_ref[...] = _dot_nt(h_ref[...], wg_ref[0])

    _interleave(projection_steps(oc_ref, A_WIDTH + B_WIDTH)
                + projection_steps(oz_ref, A_WIDTH + B_WIDTH + C_WIDTH) + [gate_rank],
                swa_steps)


def _front(layer, x2, scale, shift, g_pre, w_main, w_gate, sg_w, sg_bias_tile, sg_ln_g, sg_ln_b,
           sinks, cos_t, sin_t):
    tiles_per_batch = SEQ // INPROJ_TM
    widths = (SWA_WIDTH, SG_WIDTH, C_WIDTH, Z_WIDTH, G_WIDTH)
    dtypes = (BF16, BF16, BF16, BF16, F32)
    const = pl.Buffered(1)
    return pl.pallas_call(
        _front_kernel,
        grid=(ROWS // INPROJ_TM,),
        in_specs=[
            pl.BlockSpec(memory_space=pltpu.SMEM),
            pl.BlockSpec((INPROJ_TM, D_MODEL), lambda i: (i, 0)),
            pl.BlockSpec((1, 1, D_MODEL), lambda i: (i // tiles_per_batch, 0, 0)),
            pl.BlockSpec((1, 1, D_MODEL), lambda i: (i // tiles_per_batch, 0, 0)),
            pl.BlockSpec((1, D_MODEL), lambda i: (0, 0)),
            pl.BlockSpec((1, MAIN_WIDTH, D_MODEL), lambda i: (layer, 0, 0), pipeline_mode=const),
            pl.BlockSpec((1, G_WIDTH, D_MODEL), lambda i: (layer, 0, 0), pipeline_mode=const),
            pl.BlockSpec((SG_GROUPS, SG_CHUNK, SG_CHUNK), lambda i: (0, 0, 0), pipeline_mode=const),
            pl.BlockSpec((SG_CHUNK, SG_WIDTH), lambda i: (0, 0), pipeline_mode=const),
            pl.BlockSpec((1, SG_WIDTH), lambda i: (0, 0)),
            pl.BlockSpec((1, SG_WIDTH), lambda i: (0, 0)),
            pl.BlockSpec((INPROJ_TM, LANES), lambda i: (i, 0)),
            pl.BlockSpec((INPROJ_TM, LANES), lambda i: (i, 0)),
        ],
        out_specs=[pl.BlockSpec((INPROJ_TM, w), lambda i: (i, 0)) for w in widths],
        out_shape=[jax.ShapeDtypeStruct((ROWS, w), dt) for w, dt in zip(widths, dtypes)],
        scratch_shapes=[
            pltpu.VMEM((INPROJ_TM, D_MODEL), BF16),
            pltpu.VMEM((INPROJ_TM, B_WIDTH), F32),
            pltpu.VMEM((INPROJ_TM, A_WIDTH), F32),
            pltpu.VMEM((INPROJ_TM + WINDOW, SWA_KV_WIDTH), F32),
            pltpu.VMEM((INPROJ_TM + WINDOW, SWA_KV_WIDTH), F32),
            pltpu.VMEM((INPROJ_TM, SWA_WIDTH), BF16),
            pltpu.VMEM((2 * SWA_KV_HEADS, 2 * WINDOW, LANES), BF16),
            pltpu.VMEM((SWA_KV_HEADS, 2 * WINDOW, LANES), BF16),
            pltpu.VMEM((SWA_HEADS, WINDOW, WINDOW), F32),
            pltpu.VMEM((SWA_HEADS, WINDOW, 2 * WINDOW), BF16),
        ],
        compiler_params=pltpu.CompilerParams(
            dimension_semantics=("arbitrary",), vmem_limit_bytes=FRONT_VMEM_LIMIT),
        name="front",
    )(sinks, x2, scale, shift, g_pre, w_main, w_gate, sg_w, sg_bias_tile, sg_ln_g, sg_ln_b, cos_t, sin_t)


def _gla_sum_matrix():
    c = GLA_CHUNK
    mat = np.zeros(((2 + GLA_LEVELS) * c, c), np.float32)
    for t in range(c):
        mat[t, :t + 1] = 1.0
        mat[c + t, t + 1:] = 1.0
        for k in range(GLA_LEVELS):
            m = 1 << k
            r = (t >> (k + 1) << (k + 1)) + m
            row = (2 + k) * c + t
            if (t >> k) & 1:
                mat[row, r + 1:t + 1] = 1.0
            else:
                mat[row, t + 1:r + 1] = 1.0
    return mat


def _split_heads_on_rows(x, low_half):
    zero = jnp.zeros_like(x)
    return jnp.concatenate([jnp.where(low_half, x, zero), jnp.where(low_half, zero, x)], axis=0)


def _gla_kernel(q_ref, k_ref, v_ref, cg_ref, wup_ref, bg_ref, summat_ref, ng_ref, o_ref, st_ref):
    c = GLA_CHUNK
    heads_per_tile = LANES // GLA_DK
    n_tiles = BATCH * GLA_KEY_WIDTH // LANES

    @pl.when(pl.program_id(0) == 0)
    def _():
        st_ref[...] = jnp.zeros_like(st_ref)

    cg = cg_ref[...].reshape(BATCH * c, G_WIDTH).astype(BF16)
    logits = jnp.dot(cg, wup_ref[...], preferred_element_type=F32) + bg_ref[...]
    log_alpha = (jnp.minimum(logits, 0.0) - jnp.log1p(jnp.exp(-jnp.abs(logits)))) * (1.0 / GLA_GATE_TAU)

    hi = log_alpha.astype(BF16)
    r1 = log_alpha - hi.astype(F32)
    mid = r1.astype(BF16)
    lo = (r1 - mid.astype(F32)).astype(BF16)
    pieces = jnp.concatenate(
        [jnp.concatenate([p[b * c:(b + 1) * c] for p in (hi, mid, lo)], axis=0) for b in range(BATCH)],
        axis=1)
    expo = jnp.dot(summat_ref[...], pieces, preferred_element_type=F32)

    q = jnp.concatenate([q_ref[b] for b in range(BATCH)], axis=1).astype(F32) * (GLA_DK ** -0.5)
    k = jnp.concatenate([k_ref[b] for b in range(BATCH)], axis=1).astype(F32)
    b_cum = expo[0:c]
    q_inter = (q * jnp.exp(b_cum)).astype(BF16)
    k_state = (k * jnp.exp(expo[c:2 * c])).astype(BF16)
    decay = jnp.exp(b_cum[c - 1:c, :])

    t_i = lax.broadcasted_iota(jnp.int32, (heads_per_tile * c, c), 0) % c
    s_i = lax.broadcasted_iota(jnp.int32, (heads_per_tile * c, c), 1)
    q_lv, k_lv, masks = [q.astype(BF16)], [k.astype(BF16)], [t_i == s_i]
    for lv in range(GLA_LEVELS):
        f = jnp.exp(expo[(2 + lv) * c:(3 + lv) * c])
        q_lv.append((q * f).astype(BF16))
        k_lv.append((k * f).astype(BF16))
        masks.append(((t_i >> (lv + 1)) == (s_i >> (lv + 1)))
                     & (((t_i >> lv) & 1) == 1) & (((s_i >> lv) & 1) == 0))

    low_half = lax.broadcasted_iota(jnp.int32, (1, LANES), 1) < GLA_DK
    scores = []
    for j in range(n_tiles):
        sl = slice(j * LANES, (j + 1) * LANES)
        a2 = jnp.zeros((heads_per_tile * c, c), F32)
        for ql, kl, mask in zip(q_lv, k_lv, masks):
            a2 = a2 + jnp.where(mask, _dot_nt(_split_heads_on_rows(ql[:, sl], low_half), kl[:, sl]), 0.0)
        scores.append(a2.astype(BF16))

    tiles_per_seq = GLA_KEY_WIDTH // LANES
    for j in range(n_tiles):
        b, pair = divmod(j, tiles_per_seq)
        sl = slice(j * LANES, (j + 1) * LANES)
        st_cols = slice(pair * LANES, (pair + 1) * LANES)
        st = st_ref[b, :, st_cols]
        inter = _dot_nt(_split_heads_on_rows(q_inter[:, sl], low_half), st.astype(BF16))
        updates = []
        for u in range(heads_per_tile):
            h = pair * heads_per_tile + u
            vs = slice(h * GLA_DV, (h + 1) * GLA_DV)
            vh = v_ref[b, :, vs]
            o = jnp.dot(scores[j][u * c:(u + 1) * c], vh, preferred_element_type=F32) + inter[u * c:(u + 1) * c]
            y = o * lax.rsqrt(jnp.mean(o * o, axis=-1, keepdims=True) + EPS) * ng_ref[...]
            o_ref[b, :, vs] = y.astype(o_ref.dtype)
            updates.append(_dot_tn(vh, k_state[:, sl]))
        st_ref[b, :, st_cols] = st * decay[:, sl] + jnp.where(low_half, updates[0], updates[1])


def _gla(proj_c, proj_g, wup_pad, b_gate, summat3, norm_g):
    return pl.pallas_call(
        _gla_kernel,
        grid=(SEQ // GLA_CHUNK,),
        in_specs=[
            pl.BlockSpec((BATCH, GLA_CHUNK, GLA_KEY_WIDTH), lambda c: (0, c, 0)),
            pl.BlockSpec((BATCH, GLA_CHUNK, GLA_KEY_WIDTH), lambda c: (0, c, 1)),
            pl.BlockSpec((BATCH, GLA_CHUNK, GLA_WIDTH), lambda c: (0, c, 1)),
            pl.BlockSpec((BATCH, GLA_CHUNK, G_WIDTH), lambda c: (0, c, 0)),
            pl.BlockSpec((G_WIDTH, GLA_KEY_WIDTH), lambda c: (0, 0)),
            pl.BlockSpec((1, GLA_KEY_WIDTH), lambda c: (0, 0)),
            pl.BlockSpec(((2 + GLA_LEVELS) * GLA_CHUNK, 3 * GLA_CHUNK), lambda c: (0, 0)),
            pl.BlockSpec((1, GLA_DV), lambda c: (0, 0)),
        ],
        out_specs=pl.BlockSpec((BATCH, GLA_CHUNK, GLA_WIDTH), lambda c: (0, c, 0)),
        out_shape=jax.ShapeDtypeStruct((BATCH, SEQ, GLA_WIDTH), BF16),
        scratch_shapes=[pltpu.VMEM((BATCH, GLA_DV, GLA_KEY_WIDTH), F32)],
        compiler_params=pltpu.CompilerParams(dimension_semantics=("arbitrary",)),
        name="gla",
    )(proj_c, proj_c, proj_c, proj_g, wup_pad, b_gate, summat3, norm_g)


OUT_TM = 512


def _outproj_kernel(ya_ref, yb_ref, yc_ref, z_ref, x_ref, gate_ref, g_ref, w_ref, o_ref):
    acc = None
    col = 0
    for y_ref in (ya_ref, yb_ref, yc_ref):
        width = y_ref.shape[1]
        z = z_ref[:, col:col + width].astype(F32)
        y = (y_ref[...].astype(F32) * (z * _sigmoid(z))).astype(BF16)
        part = jnp.dot(y, w_ref[0, col:col + width, :], preferred_element_type=F32)
        acc = part if acc is None else acc + part
        col += width
    normed = acc * lax.rsqrt(jnp.mean(acc * acc, axis=-1, keepdims=True) + EPS) * g_ref[...]
    o_ref[...] = x_ref[...] + gate_ref[0] * normed


def _outproj(layer, y_a, y_b, y_c, z, x2, gate, g_post, w_out_bf):
    tiles_per_batch = SEQ // OUT_TM
    return pl.pallas_call(
        _outproj_kernel,
        grid=(ROWS // OUT_TM,),
        in_specs=[
            pl.BlockSpec((OUT_TM, SWA_WIDTH), lambda i: (i, 0)),
            pl.BlockSpec((OUT_TM, SG_WIDTH), lambda i: (i, 0)),
            pl.BlockSpec((OUT_TM, GLA_WIDTH), lambda i: (i, 0)),
            pl.BlockSpec((OUT_TM, Z_WIDTH), lambda i: (i, 0)),
            pl.BlockSpec((OUT_TM, D_MODEL), lambda i: (i, 0)),
            pl.BlockSpec((1, 1, D_MODEL), lambda i: (i // tiles_per_batch, 0, 0)),
            pl.BlockSpec((1, D_MODEL), lambda i: (0, 0)),
            pl.BlockSpec((1, D_MODEL, D_MODEL), lambda i: (layer, 0, 0), pipeline_mode=pl.Buffered(1)),
        ],
        out_specs=pl.BlockSpec((OUT_TM, D_MODEL), lambda i: (i, 0)),
        out_shape=jax.ShapeDtypeStruct((ROWS, D_MODEL), F32),
        compiler_params=pltpu.CompilerParams(
            dimension_semantics=("arbitrary",), vmem_limit_bytes=VMEM_LIMIT),
        name="gate_outproj_residual",
    )(y_a, y_b, y_c, z, x2, gate, g_post, w_out_bf)


def kernel(x, c, positions, w_mod, b_mod, g_pre, g_post, w_in, w_out, swa_sinks,
           sg_w, sg_b, sg_ln_g, sg_ln_b, gla_w_gate_up, gla_b_gate, gla_norm_g):
    assert x.shape == (BATCH, SEQ, D_MODEL) and w_in.shape[0] == DEPTH

    c_pad = jnp.pad(c, ((0, 8 - BATCH), (0, 0)))
    mod = _modulation(c_pad, w_mod, b_mod)[:, :BATCH, :]
    cos_t, sin_t = _rope_tables(positions)
    summat3 = jnp.asarray(np.tile(_gla_sum_matrix(), (1, 3)), dtype=BF16)

    w_main, w_gate = _prep_w_in(jnp.swapaxes(w_in, 1, 2))
    w_out_bf = _prep_w_out(w_out)

    x2 = x.reshape(ROWS, D_MODEL)
    for l in range(DEPTH):
        shift = mod[l, :, 0:D_MODEL].reshape(BATCH, 1, D_MODEL)
        scale = mod[l, :, D_MODEL:2 * D_MODEL].reshape(BATCH, 1, D_MODEL)
        gate = mod[l, :, 2 * D_MODEL:].reshape(BATCH, 1, D_MODEL)
        bias_tile = jnp.repeat(sg_b[l].T, SG_GROUP_DIM, axis=1)
        y_a, y_b, proj_c, proj_z, proj_g = _front(
            l, x2, scale, shift, g_pre[l].reshape(1, D_MODEL), w_main, w_gate, sg_w[l], bias_tile,
            sg_ln_g[l].reshape(1, SG_WIDTH), sg_ln_b[l].reshape(1, SG_WIDTH), swa_sinks[l], cos_t, sin_t)
        wup_pad = jnp.pad(gla_w_gate_up[l], ((0, G_WIDTH - GLA_GATE_RANK), (0, 0))).astype(BF16)
        y_c = _gla(proj_c.reshape(BATCH, SEQ, C_WIDTH), proj_g.reshape(BATCH, SEQ, G_WIDTH), wup_pad,
                   gla_b_gate[l].reshape(1, GLA_KEY_WIDTH), summat3,
                   gla_norm_g[l].reshape(1, GLA_DV)).reshape(ROWS, GLA_WIDTH)

        x2 = _outproj(l, y_a, y_b, y_c, proj_z, x2, gate, g_post[l].reshape(1, D_MODEL), w_out_bf)
    return x2.reshape(BATCH, SEQ, D_MODEL)
```

```python
import functools

import numpy as np
import jax
import jax.numpy as jnp
from jax import lax
from jax.experimental import pallas as pl
from jax.experimental.pallas import tpu as pltpu

F32 = jnp.float32
BF16 = jnp.bfloat16

D_MODEL = 2048
BATCH = 4
SEQ = 2048
DEPTH = 2
EPS = 1e-6
ROWS = BATCH * SEQ

SWA_HEAD_DIM = 64
SWA_HEADS = 16
SWA_KV_HEADS = 4
SWA_GROUP = SWA_HEADS // SWA_KV_HEADS
SWA_WIDTH = SWA_HEADS * SWA_HEAD_DIM
SWA_KV_WIDTH = SWA_KV_HEADS * SWA_HEAD_DIM
WINDOW = 128
ROT_DIM = 16
ROT_HALF = ROT_DIM // 2
ROPE_THETA = 500000.0

SG_WIDTH = 512
SG_GROUPS = 8
SG_GROUP_DIM = 64
SG_CHUNK = 128

GLA_HEADS = 4
GLA_WIDTH = 512
GLA_DV = 128
GLA_DK = 64
GLA_KEY_WIDTH = 256
GLA_GATE_RANK = 16
GLA_GATE_TAU = 16.0
GLA_CHUNK = 128
GLA_LEVELS = 7
GLA_SAFE_DECAY = 40.0

LANES = 128
A_WIDTH = SWA_WIDTH + 2 * SWA_KV_WIDTH
B_WIDTH = 2 * SG_WIDTH
C_WIDTH = 2 * GLA_KEY_WIDTH + GLA_WIDTH
Z_WIDTH = D_MODEL
G_WIDTH = LANES
W_IN_PAD = A_WIDTH + B_WIDTH + C_WIDTH + Z_WIDTH + G_WIDTH
MOD_WIDTH = 3 * D_MODEL

VMEM_LIMIT = 56 * 1024 * 1024
FRONT_VMEM_LIMIT = 60 * 1024 * 1024


def _sigmoid(x):
    return 1.0 / (1.0 + jnp.exp(-x))


def _gelu_tanh(x):
    return 0.5 * x * (1.0 + jnp.tanh(0.7978845608028654 * (x + 0.044715 * (x * x * x))))


def _dot_nt(a, b):
    return lax.dot_general(a, b, (((1,), (1,)), ((), ())), preferred_element_type=F32)


def _dot_tn(a, b):
    return lax.dot_general(a, b, (((0,), (0,)), ((), ())), preferred_element_type=F32)


MOD_TN = 768


def _mod_kernel(c_ref, w_ref, b_ref, o_ref):
    c = c_ref[...]
    s = (c * _sigmoid(c)).astype(BF16)
    o_ref[0] = jnp.dot(s, w_ref[0].astype(BF16), preferred_element_type=F32) + b_ref[0]


def _modulation(c_pad, w_mod, b_mod):
    return pl.pallas_call(
        _mod_kernel,
        grid=(DEPTH, MOD_WIDTH // MOD_TN),
        in_specs=[
            pl.BlockSpec((8, D_MODEL), lambda l, j: (0, 0)),
            pl.BlockSpec((1, D_MODEL, MOD_TN), lambda l, j: (l, 0, j)),
            pl.BlockSpec((1, 1, MOD_TN), lambda l, j: (l, 0, j)),
        ],
        out_specs=pl.BlockSpec((1, 8, MOD_TN), lambda l, j: (l, 0, j)),
        out_shape=jax.ShapeDtypeStruct((DEPTH, 8, MOD_WIDTH), F32),
        compiler_params=pltpu.CompilerParams(
            dimension_semantics=("arbitrary", "arbitrary"), vmem_limit_bytes=VMEM_LIMIT),
        name="adaln_mod",
    )(c_pad, w_mod, b_mod.reshape(DEPTH, 1, MOD_WIDTH))


ROPE_TM = 1024


def _rope_table_kernel(pos_ref, invf_ref, cos_ref, sin_ref):
    ang = pos_ref[...].astype(F32) * invf_ref[...]
    lane = lax.broadcasted_iota(jnp.int32, (1, LANES), 1) % SWA_HEAD_DIM
    s = jnp.sin(ang)
    cos_ref[...] = jnp.cos(ang)
    sin_ref[...] = jnp.where(lane < ROT_HALF, -s, s)


def _rope_tables(positions):
    half = np.arange(ROT_HALF, dtype=np.float32)
    inv_freq = (np.float32(ROPE_THETA) ** (-(half * np.float32(2.0 / ROT_DIM)))).astype(np.float32)
    lane = np.arange(LANES) % SWA_HEAD_DIM
    invf = np.where(lane < ROT_DIM, inv_freq[lane % ROT_HALF], 0.0).astype(np.float32)[None, :]
    return pl.pallas_call(
        _rope_table_kernel,
        grid=(ROWS // ROPE_TM,),
        in_specs=[
            pl.BlockSpec((ROPE_TM, 1), lambda i: (i, 0)),
            pl.BlockSpec((1, LANES), lambda i: (0, 0)),
        ],
        out_specs=[pl.BlockSpec((ROPE_TM, LANES), lambda i: (i, 0))] * 2,
        out_shape=[jax.ShapeDtypeStruct((ROWS, LANES), F32)] * 2,
        compiler_params=pltpu.CompilerParams(dimension_semantics=("arbitrary",)),
        name="rope_tables",
    )(positions.reshape(ROWS, 1), jnp.asarray(invf))


PREP_TN = 512
MAIN_WIDTH = A_WIDTH + B_WIDTH + C_WIDTH + Z_WIDTH
GATE_COL = A_WIDTH + B_WIDTH + C_WIDTH
IN_PROJ_WIDTH = GATE_COL + GLA_GATE_RANK + Z_WIDTH


def _prep_w_in_kernel(a_ref, b_ref, main_ref, gate_ref):
    j = pl.program_id(1)
    first_z = GATE_COL // PREP_TN

    @pl.when(j < first_z)
    def _():
        main_ref[0] = a_ref[0].astype(BF16)

    @pl.when(j >= first_z)
    def _():
        main_ref[0] = jnp.concatenate(
            [a_ref[0, GLA_GATE_RANK:, :], b_ref[0, :GLA_GATE_RANK, :]], axis=0).astype(BF16)

    @pl.when(j == first_z)
    def _():
        gate_ref[0] = jnp.concatenate(
            [a_ref[0, :GLA_GATE_RANK, :], jnp.zeros((G_WIDTH - GLA_GATE_RANK, D_MODEL), F32)],
            axis=0).astype(BF16)


def _prep_w_in(w_in_t):
    first_z = GATE_COL // PREP_TN
    last = (IN_PROJ_WIDTH - 1) // PREP_TN
    return pl.pallas_call(
        _prep_w_in_kernel,
        grid=(DEPTH, MAIN_WIDTH // PREP_TN),
        in_specs=[
            pl.BlockSpec((1, PREP_TN, D_MODEL), lambda l, j: (l, j, 0)),
            pl.BlockSpec((1, PREP_TN, D_MODEL), lambda l, j: (l, jnp.clip(j + 1, first_z + 1, last), 0)),
        ],
        out_specs=[
            pl.BlockSpec((1, PREP_TN, D_MODEL), lambda l, j: (l, j, 0)),
            pl.BlockSpec((1, G_WIDTH, D_MODEL), lambda l, j: (l, 0, 0)),
        ],
        out_shape=[jax.ShapeDtypeStruct((DEPTH, MAIN_WIDTH, D_MODEL), BF16),
                   jax.ShapeDtypeStruct((DEPTH, G_WIDTH, D_MODEL), BF16)],
        compiler_params=pltpu.CompilerParams(
            dimension_semantics=("arbitrary", "arbitrary"), vmem_limit_bytes=VMEM_LIMIT),
        name="prep_w_in",
    )(w_in_t, w_in_t)


def _cast_kernel(x_ref, o_ref):
    o_ref[...] = x_ref[...].astype(o_ref.dtype)


def _prep_w_out(w_out):
    tn = 1024
    return pl.pallas_call(
        _cast_kernel,
        grid=(DEPTH, D_MODEL // tn),
        in_specs=[pl.BlockSpec((1, D_MODEL, tn), lambda l, j: (l, 0, j))],
        out_specs=pl.BlockSpec((1, D_MODEL, tn), lambda l, j: (l, 0, j)),
        out_shape=jax.ShapeDtypeStruct((DEPTH, D_MODEL, D_MODEL), BF16),
        compiler_params=pltpu.CompilerParams(
            dimension_semantics=("arbitrary", "arbitrary"), vmem_limit_bytes=VMEM_LIMIT),
        name="prep_w_out",
    )(w_out)


INPROJ_TM = 512
INPROJ_CHUNK = 512


def _spatial_gating_chunk(u, v, w_bf, bias, ln_g, ln_b):
    u = _gelu_tanh(u)
    v = _gelu_tanh(v)
    mu = jnp.mean(v, axis=-1, keepdims=True)
    vc = v - mu
    var = jnp.mean(vc * vc, axis=-1, keepdims=True)
    vn = (vc * lax.rsqrt(var + EPS) * ln_g + ln_b).astype(BF16)
    parts = [jnp.dot(w_bf[g], vn[:, g * SG_GROUP_DIM:(g + 1) * SG_GROUP_DIM], preferred_element_type=F32)
             for g in range(SG_GROUPS)]
    return u * (jnp.concatenate(parts, axis=1) + bias)


def _swa_pieces(tile_has_prev, a_ref, cos_ref, sin_ref, sinks_ref, ya_ref,
                kband_ref, vband_ref, qr_ref, ksel_ref, vdup_ref, s_ref, p_ref):
    heads_per_tile = LANES // SWA_HEAD_DIM
    n_blocks = INPROJ_TM // WINDOW
    lane = lax.broadcasted_iota(jnp.int32, (1, LANES), 1)
    first_half = (lane % SWA_HEAD_DIM) < ROT_HALF
    low_half = lane < SWA_HEAD_DIM
    qi = lax.broadcasted_iota(jnp.int32, (WINDOW, WINDOW), 0)
    kj = lax.broadcasted_iota(jnp.int32, (WINDOW, WINDOW), 1)
    from_prev = kj > qi

    def rope(t):
        partner = jnp.where(first_half, pltpu.roll(t, LANES - ROT_HALF, 1), pltpu.roll(t, ROT_HALF, 1))
        return t * cos_ref[...] + partner * sin_ref[...]

    steps = []

    def carry():
        kband_ref[0:WINDOW, :] = kband_ref[INPROJ_TM:INPROJ_TM + WINDOW, :]
        vband_ref[0:WINDOW, :] = vband_ref[INPROJ_TM:INPROJ_TM + WINDOW, :]
    steps.append(carry)

    def stage_k(t):
        cols = slice(t * LANES, (t + 1) * LANES)
        kband_ref[WINDOW:, cols] = rope(a_ref[:, SWA_WIDTH + t * LANES:SWA_WIDTH + (t + 1) * LANES])
        vband_ref[WINDOW:, cols] = a_ref[:, SWA_WIDTH + SWA_KV_WIDTH + t * LANES:
                                         SWA_WIDTH + SWA_KV_WIDTH + (t + 1) * LANES]
    for t in range(SWA_KV_WIDTH // LANES):
        steps.append(functools.partial(stage_k, t))

    def stage_q(t):
        cols = slice(t * LANES, (t + 1) * LANES)
        qr_ref[:, cols] = rope(a_ref[:, cols] * (SWA_HEAD_DIM ** -0.5)).astype(BF16)
    for t in range(SWA_WIDTH // LANES):
        steps.append(functools.partial(stage_q, t))

    def stage_kv(blk, t):
        rows = slice(blk * WINDOW, (blk + 2) * WINDOW)
        cols = slice(t * LANES, (t + 1) * LANES)
        kt = kband_ref[rows, cols]
        vt = vband_ref[rows, cols]
        kt_sw = pltpu.roll(kt, SWA_HEAD_DIM, 1)
        vt_sw = pltpu.roll(vt, SWA_HEAD_DIM, 1)
        for u, (k_lo, k_hi, v_lo, v_hi) in enumerate(((kt, kt_sw, vt, vt_sw), (kt_sw, kt, vt_sw, vt))):
            g = t * heads_per_tile + u
            ksel_ref[2 * g] = jnp.where(low_half, k_lo, 0.0).astype(BF16)
            ksel_ref[2 * g + 1] = jnp.where(low_half, 0.0, k_hi).astype(BF16)
            vdup_ref[g] = jnp.where(low_half, v_lo, v_hi).astype(BF16)

    def scores(blk, h):
        t, u = divmod(h, heads_per_tile)
        q = qr_ref[blk * WINDOW:(blk + 1) * WINDOW, t * LANES:(t + 1) * LANES]
        s2 = _dot_nt(q, ksel_ref[2 * (h // SWA_GROUP) + u])
        s_ref[h] = jnp.where(from_prev, s2[:, :WINDOW], s2[:, WINDOW:])

    def softmax(blk, h):
        s = s_ref[h]
        if blk == 0:
            s = jnp.where(kj <= qi + tile_has_prev * WINDOW, s, -jnp.inf)
        sink = sinks_ref[h]
        m = jnp.maximum(jnp.max(s, axis=-1, keepdims=True), sink)
        p = jnp.exp(s - m)
        denom = jnp.sum(p, axis=-1, keepdims=True) + jnp.exp(sink - m)
        p = (p * (1.0 / denom)).astype(BF16)
        zero = jnp.zeros_like(p)
        p_ref[h, :, :WINDOW] = jnp.where(from_prev, p, zero)
        p_ref[h, :, WINDOW:] = jnp.where(from_prev, zero, p)

    def values(blk, t):
        vg = vdup_ref[(t * heads_per_tile) // SWA_GROUP]
        outs = [jnp.dot(p_ref[t * heads_per_tile + u], vg, preferred_element_type=F32)
                for u in range(heads_per_tile)]
        ya_ref[blk * WINDOW:(blk + 1) * WINDOW, t * LANES:(t + 1) * LANES] = (
            jnp.where(low_half, outs[0], outs[1]).astype(ya_ref.dtype))

    for blk in range(n_blocks):
        for t in range(SWA_KV_WIDTH // LANES):
            steps.append(functools.partial(stage_kv, blk, t))
        for h in range(SWA_HEADS):
            steps.append(functools.partial(scores, blk, h))
        for h in range(SWA_HEADS):
            steps.append(functools.partial(softmax, blk, h))
        for t in range(SWA_WIDTH // LANES):
            steps.append(functools.partial(values, blk, t))
    return steps


def _interleave(main_steps, side_steps):
    done = 0
    for idx, step in enumerate(main_steps):
        step()
        upto = (idx + 1) * len(side_steps) // len(main_steps)
        for side in side_steps[done:upto]:
            side()
        done = upto


def _front_kernel(sinks_ref, x_ref, scale_ref, shift_ref, g_ref, w_ref, wg_ref,
                  sgw_ref, sgb_ref, lng_ref, lnb_ref, cos_ref, sin_ref,
                  ya_ref, yb_ref, oc_ref, oz_ref, og_ref,
                  h_ref, uv_ref, a_ref, kband_ref, vband_ref, qr_ref, ksel_ref, vdup_ref, s_ref, p_ref):
    i = pl.program_id(0)

    @pl.when(i == 0)
    def _():
        kband_ref[...] = jnp.zeros_like(kband_ref)
        vband_ref[...] = jnp.zeros_like(vband_ref)

    x = x_ref[...]
    ms = jnp.mean(x * x, axis=-1, keepdims=True)
    y = x * lax.rsqrt(ms + EPS) * g_ref[...]
    h_ref[...] = (y * (1.0 + scale_ref[0]) + shift_ref[0]).astype(BF16)

    def project(o_ref, col, c0, width):
        r = _dot_nt(h_ref[...], w_ref[0, col + c0:col + c0 + width, :])
        o_ref[:, c0:c0 + width] = r.astype(o_ref.dtype)

    def projection_steps(o_ref, col):
        return [functools.partial(project, o_ref, col, c0, INPROJ_CHUNK)
                for c0 in range(0, o_ref.shape[1], INPROJ_CHUNK)]

    for step in projection_steps(uv_ref, A_WIDTH):
        step()

    t = lax.broadcasted_iota(jnp.int32, (SG_CHUNK, SG_CHUNK), 0)
    s = lax.broadcasted_iota(jnp.int32, (SG_CHUNK, SG_CHUNK), 1)
    w_bf = [jnp.where(t >= s, sgw_ref[g], 0.0).astype(BF16) for g in range(SG_GROUPS)]

    def mixer_b(r0):
        rows = slice(r0, r0 + SG_CHUNK)
        yb_ref[rows, :] = _spatial_gating_chunk(
            uv_ref[rows, :SG_WIDTH], uv_ref[rows, SG_WIDTH:], w_bf, sgb_ref[...], lng_ref[...],
            lnb_ref[...]).astype(yb_ref.dtype)

    _interleave(projection_steps(a_ref, 0),
                [functools.partial(mixer_b, r0) for r0 in range(0, INPROJ_TM, SG_CHUNK)])

    tile_has_prev = ((i % (SEQ // INPROJ_TM)) != 0).astype(jnp.int32)
    swa_steps = _swa_pieces(tile_has_prev, a_ref, cos_ref, sin_ref, sinks_ref, ya_ref,
                            kband_ref, vband_ref, qr_ref, ksel_ref, vdup_ref, s_ref, p_ref)

    def gate_rank():
        og_ref[...] = _dot_nt(h_ref[...], wg_ref[0])

    _interleave(projection_steps(oc_ref, A_WIDTH + B_WIDTH)
                + projection_steps(oz_ref, A_WIDTH + B_WIDTH + C_WIDTH) + [gate_rank],
                swa_steps)


def _front(layer, x2, scale, shift, g_pre, w_main, w_gate, sg_w, sg_bias_tile, sg_ln_g, sg_ln_b,
           sinks, cos_t, sin_t):
    tiles_per_batch = SEQ // INPROJ_TM
    widths = (SWA_WIDTH, SG_WIDTH, C_WIDTH, Z_WIDTH, G_WIDTH)
    dtypes = (BF16, BF16, BF16, BF16, F32)
    const = pl.Buffered(1)
    return pl.pallas_call(
        _front_kernel,
        grid=(ROWS // INPROJ_TM,),
        in_specs=[
            pl.BlockSpec(memory_space=pltpu.SMEM),
            pl.BlockSpec((INPROJ_TM, D_MODEL), lambda i: (i, 0)),
            pl.BlockSpec((1, 1, D_MODEL), lambda i: (i // tiles_per_batch, 0, 0)),
            pl.BlockSpec((1, 1, D_MODEL), lambda i: (i // tiles_per_batch, 0, 0)),
            pl.BlockSpec((1, D_MODEL), lambda i: (0, 0)),
            pl.BlockSpec((1, MAIN_WIDTH, D_MODEL), lambda i: (layer, 0, 0), pipeline_mode=const),
            pl.BlockSpec((1, G_WIDTH, D_MODEL), lambda i: (layer, 0, 0), pipeline_mode=const),
            pl.BlockSpec((SG_GROUPS, SG_CHUNK, SG_CHUNK), lambda i: (0, 0, 0), pipeline_mode=const),
            pl.BlockSpec((SG_CHUNK, SG_WIDTH), lambda i: (0, 0), pipeline_mode=const),
            pl.BlockSpec((1, SG_WIDTH), lambda i: (0, 0)),
            pl.BlockSpec((1, SG_WIDTH), lambda i: (0, 0)),
            pl.BlockSpec((INPROJ_TM, LANES), lambda i: (i, 0)),
            pl.BlockSpec((INPROJ_TM, LANES), lambda i: (i, 0)),
        ],
        out_specs=[pl.BlockSpec((INPROJ_TM, w), lambda i: (i, 0)) for w in widths],
        out_shape=[jax.ShapeDtypeStruct((ROWS, w), dt) for w, dt in zip(widths, dtypes)],
        scratch_shapes=[
            pltpu.VMEM((INPROJ_TM, D_MODEL), BF16),
            pltpu.VMEM((INPROJ_TM, B_WIDTH), F32),
            pltpu.VMEM((INPROJ_TM, A_WIDTH), F32),
            pltpu.VMEM((INPROJ_TM + WINDOW, SWA_KV_WIDTH), F32),
            pltpu.VMEM((INPROJ_TM + WINDOW, SWA_KV_WIDTH), F32),
            pltpu.VMEM((INPROJ_TM, SWA_WIDTH), BF16),
            pltpu.VMEM((2 * SWA_KV_HEADS, 2 * WINDOW, LANES), BF16),
            pltpu.VMEM((SWA_KV_HEADS, 2 * WINDOW, LANES), BF16),
            pltpu.VMEM((SWA_HEADS, WINDOW, WINDOW), F32),
            pltpu.VMEM((SWA_HEADS, WINDOW, 2 * WINDOW), BF16),
        ],
        compiler_params=pltpu.CompilerParams(
            dimension_semantics=("arbitrary",), vmem_limit_bytes=FRONT_VMEM_LIMIT),
        name="front",
    )(sinks, x2, scale, shift, g_pre, w_main, w_gate, sg_w, sg_bias_tile, sg_ln_g, sg_ln_b, cos_t, sin_t)


def _gla_sum_matrix():
    c = GLA_CHUNK
    mat = np.zeros(((2 + GLA_LEVELS) * c, c), np.float32)
    for t in range(c):
        mat[t, :t + 1] = 1.0
        mat[c + t, t + 1:] = 1.0
        for k in range(GLA_LEVELS):
            m = 1 << k
            r = (t >> (k + 1) << (k + 1)) + m
            row = (2 + k) * c + t
            if (t >> k) & 1:
                mat[row, r + 1:t + 1] = 1.0
            else:
                mat[row, t + 1:r + 1] = 1.0
    return mat


def _split_heads_on_rows(x, low_half):
    zero = jnp.zeros_like(x)
    return jnp.concatenate([jnp.where(low_half, x, zero), jnp.where(low_half, zero, x)], axis=0)


def _gla_kernel(q_ref, k_ref, v_ref, cg_ref, wup_ref, bg_ref, summat_ref, ng_ref, o_ref, st_ref, sc_ref):
    c = GLA_CHUNK
    heads_per_tile = LANES // GLA_DK
    n_tiles = BATCH * GLA_KEY_WIDTH // LANES

    @pl.when(pl.program_id(0) == 0)
    def _():
        st_ref[...] = jnp.zeros_like(st_ref)

    cg = cg_ref[...].reshape(BATCH * c, G_WIDTH).astype(BF16)
    logits = jnp.dot(cg, wup_ref[...], preferred_element_type=F32) + bg_ref[...]
    log_alpha = (jnp.minimum(logits, 0.0) - jnp.log1p(jnp.exp(-jnp.abs(logits)))) * (1.0 / GLA_GATE_TAU)

    hi = log_alpha.astype(BF16)
    r1 = log_alpha - hi.astype(F32)
    mid = r1.astype(BF16)
    lo = (r1 - mid.astype(F32)).astype(BF16)
    pieces = jnp.concatenate(
        [jnp.concatenate([p[b * c:(b + 1) * c] for p in (hi, mid, lo)], axis=0) for b in range(BATCH)],
        axis=1)
    expo = jnp.dot(summat_ref[0:2 * c, :], pieces, preferred_element_type=F32)

    q = jnp.concatenate([q_ref[b] for b in range(BATCH)], axis=1).astype(F32) * (GLA_DK ** -0.5)
    k = jnp.concatenate([k_ref[b] for b in range(BATCH)], axis=1).astype(F32)
    b_cum = expo[0:c]
    q_inter = (q * jnp.exp(b_cum)).astype(BF16)
    k_state = (k * jnp.exp(expo[c:2 * c])).astype(BF16)
    b_last = b_cum[c - 1:c, :]
    decay = jnp.exp(b_last)

    t_i = lax.broadcasted_iota(jnp.int32, (heads_per_tile * c, c), 0) % c
    s_i = lax.broadcasted_iota(jnp.int32, (heads_per_tile * c, c), 1)
    low_half = lax.broadcasted_iota(jnp.int32, (1, LANES), 1) < GLA_DK

    safe = jnp.min(b_last) >= -GLA_SAFE_DECAY

    @pl.when(safe)
    def _():
        k_grown = (k * jnp.exp(-b_cum)).astype(BF16)
        for j in range(n_tiles):
            sl = slice(j * LANES, (j + 1) * LANES)
            a2 = _dot_nt(_split_heads_on_rows(q_inter[:, sl], low_half), k_grown[:, sl])
            sc_ref[j] = jnp.where(s_i <= t_i, a2, 0.0).astype(BF16)

    @pl.when(jnp.logical_not(safe))
    def _():
        lv_expo = jnp.dot(summat_ref[2 * c:, :], pieces, preferred_element_type=F32)
        q_lv, k_lv, masks = [q.astype(BF16)], [k.astype(BF16)], [t_i == s_i]
        for lv in range(GLA_LEVELS):
            f = jnp.exp(lv_expo[lv * c:(lv + 1) * c])
            q_lv.append((q * f).astype(BF16))
            k_lv.append((k * f).astype(BF16))
            masks.append(((t_i >> (lv + 1)) == (s_i >> (lv + 1)))
                         & (((t_i >> lv) & 1) == 1) & (((s_i >> lv) & 1) == 0))
        for j in range(n_tiles):
            sl = slice(j * LANES, (j + 1) * LANES)
            a2 = jnp.zeros((heads_per_tile * c, c), F32)
            for ql, kl, mask in zip(q_lv, k_lv, masks):
                a2 = a2 + jnp.where(mask, _dot_nt(_split_heads_on_rows(ql[:, sl], low_half), kl[:, sl]), 0.0)
            sc_ref[j] = a2.astype(BF16)

    scores = [sc_ref[j] for j in range(n_tiles)]
    tiles_per_seq = GLA_KEY_WIDTH // LANES
    for j in range(n_tiles):
        b, pair = divmod(j, tiles_per_seq)
        sl = slice(j * LANES, (j + 1) * LANES)
        st_cols = slice(pair * LANES, (pair + 1) * LANES)
        st = st_ref[b, :, st_cols]
        inter = _dot_nt(_split_heads_on_rows(q_inter[:, sl], low_half), st.astype(BF16))
        updates = []
        for u in range(heads_per_tile):
            h = pair * heads_per_tile + u
            vs = slice(h * GLA_DV, (h + 1) * GLA_DV)
            vh = v_ref[b, :, vs]
            o = jnp.dot(scores[j][u * c:(u + 1) * c], vh, preferred_element_type=F32) + inter[u * c:(u + 1) * c]
            y = o * lax.rsqrt(jnp.mean(o * o, axis=-1, keepdims=True) + EPS) * ng_ref[...]
            o_ref[b, :, vs] = y.astype(o_ref.dtype)
            updates.append(_dot_tn(vh, k_state[:, sl]))
        st_ref[b, :, st_cols] = st * decay[:, sl] + jnp.where(low_half, updates[0], updates[1])


def _gla(proj_c, proj_g, wup_pad, b_gate, summat3, norm_g):
    return pl.pallas_call(
        _gla_kernel,
        grid=(SEQ // GLA_CHUNK,),
        in_specs=[
            pl.BlockSpec((BATCH, GLA_CHUNK, GLA_KEY_WIDTH), lambda c: (0, c, 0)),
            pl.BlockSpec((BATCH, GLA_CHUNK, GLA_KEY_WIDTH), lambda c: (0, c, 1)),
            pl.BlockSpec((BATCH, GLA_CHUNK, GLA_WIDTH), lambda c: (0, c, 1)),
            pl.BlockSpec((BATCH, GLA_CHUNK, G_WIDTH), lambda c: (0, c, 0)),
            pl.BlockSpec((G_WIDTH, GLA_KEY_WIDTH), lambda c: (0, 0)),
            pl.BlockSpec((1, GLA_KEY_WIDTH), lambda c: (0, 0)),
            pl.BlockSpec(((2 + GLA_LEVELS) * GLA_CHUNK, 3 * GLA_CHUNK), lambda c: (0, 0)),
            pl.BlockSpec((1, GLA_DV), lambda c: (0, 0)),
        ],
        out_specs=pl.BlockSpec((BATCH, GLA_CHUNK, GLA_WIDTH), lambda c: (0, c, 0)),
        out_shape=jax.ShapeDtypeStruct((BATCH, SEQ, GLA_WIDTH), BF16),
        scratch_shapes=[pltpu.VMEM((BATCH, GLA_DV, GLA_KEY_WIDTH), F32),
                        pltpu.VMEM((BATCH * GLA_KEY_WIDTH // LANES, 2 * GLA_CHUNK, GLA_CHUNK), BF16)],
        compiler_params=pltpu.CompilerParams(dimension_semantics=("arbitrary",)),
        name="gla",
    )(proj_c, proj_c, proj_c, proj_g, wup_pad, b_gate, summat3, norm_g)


OUT_TM = 512


def _outproj_kernel(ya_ref, yb_ref, yc_ref, z_ref, x_ref, gate_ref, g_ref, w_ref, o_ref):
    acc = None
    col = 0
    for y_ref in (ya_ref, yb_ref, yc_ref):
        width = y_ref.shape[1]
        z = z_ref[:, col:col + width].astype(F32)
        y = (y_ref[...].astype(F32) * (z * _sigmoid(z))).astype(BF16)
        part = jnp.dot(y, w_ref[0, col:col + width, :], preferred_element_type=F32)
        acc = part if acc is None else acc + part
        col += width
    normed = acc * lax.rsqrt(jnp.mean(acc * acc, axis=-1, keepdims=True) + EPS) * g_ref[...]
    o_ref[...] = x_ref[...] + gate_ref[0] * normed


def _outproj(layer, y_a, y_b, y_c, z, x2, gate, g_post, w_out_bf):
    tiles_per_batch = SEQ // OUT_TM
    return pl.pallas_call(
        _outproj_kernel,
        grid=(ROWS // OUT_TM,),
        in_specs=[
            pl.BlockSpec((OUT_TM, SWA_WIDTH), lambda i: (i, 0)),
            pl.BlockSpec((OUT_TM, SG_WIDTH), lambda i: (i, 0)),
            pl.BlockSpec((OUT_TM, GLA_WIDTH), lambda i: (i, 0)),
            pl.BlockSpec((OUT_TM, Z_WIDTH), lambda i: (i, 0)),
            pl.BlockSpec((OUT_TM, D_MODEL), lambda i: (i, 0)),
            pl.BlockSpec((1, 1, D_MODEL), lambda i: (i // tiles_per_batch, 0, 0)),
            pl.BlockSpec((1, D_MODEL), lambda i: (0, 0)),
            pl.BlockSpec((1, D_MODEL, D_MODEL), lambda i: (layer, 0, 0), pipeline_mode=pl.Buffered(1)),
        ],
        out_specs=pl.BlockSpec((OUT_TM, D_MODEL), lambda i: (i, 0)),
        out_shape=jax.ShapeDtypeStruct((ROWS, D_MODEL), F32),
        compiler_params=pltpu.CompilerParams(
            dimension_semantics=("arbitrary",), vmem_limit_bytes=VMEM_LIMIT),
        name="gate_outproj_residual",
    )(y_a, y_b, y_c, z, x2, gate, g_post, w_out_bf)


def kernel(x, c, positions, w_mod, b_mod, g_pre, g_post, w_in, w_out, swa_sinks,
           sg_w, sg_b, sg_ln_g, sg_ln_b, gla_w_gate_up, gla_b_gate, gla_norm_g):
    assert x.shape == (BATCH, SEQ, D_MODEL) and w_in.shape[0] == DEPTH

    c_pad = jnp.pad(c, ((0, 8 - BATCH), (0, 0)))
    mod = _modulation(c_pad, w_mod, b_mod)[:, :BATCH, :]
    cos_t, sin_t = _rope_tables(positions)
    summat3 = jnp.asarray(np.tile(_gla_sum_matrix(), (1, 3)), dtype=BF16)

    w_main, w_gate = _prep_w_in(jnp.swapaxes(w_in, 1, 2))
    w_out_bf = _prep_w_out(w_out)

    x2 = x.reshape(ROWS, D_MODEL)
    for l in range(DEPTH):
        shift = mod[l, :, 0:D_MODEL].reshape(BATCH, 1, D_MODEL)
        scale = mod[l, :, D_MODEL:2 * D_MODEL].reshape(BATCH, 1, D_MODEL)
        gate = mod[l, :, 2 * D_MODEL:].reshape(BATCH, 1, D_MODEL)
        bias_tile = jnp.repeat(sg_b[l].T, SG_GROUP_DIM, axis=1)
        y_a, y_b, proj_c, proj_z, proj_g = _front(
            l, x2, scale, shift, g_pre[l].reshape(1, D_MODEL), w_main, w_gate, sg_w[l], bias_tile,
            sg_ln_g[l].reshape(1, SG_WIDTH), sg_ln_b[l].reshape(1, SG_WIDTH), swa_sinks[l], cos_t, sin_t)
        wup_pad = jnp.pad(gla_w_gate_up[l], ((0, G_WIDTH - GLA_GATE_RANK), (0, 0))).astype(BF16)
        y_c = _gla(proj_c.reshape(BATCH, SEQ, C_WIDTH), proj_g.reshape(BATCH, SEQ, G_WIDTH), wup_pad,
                   gla_b_gate[l].reshape(1, GLA_KEY_WIDTH), summat3,
                   gla_norm_g[l].reshape(1, GLA_DV)).reshape(ROWS, GLA_WIDTH)

        x2 = _outproj(l, y_a, y_b, y_c, proj_z, x2, gate, g_post[l].reshape(1, D_MODEL), w_out_bf)
    return x2.reshape(BATCH, SEQ, D_MODEL)
```

```python
import functools

import numpy as np
import jax
import jax.numpy as jnp
from jax import lax
from jax.experimental import pallas as pl
from jax.experimental.pallas import tpu as pltpu

F32 = jnp.float32
BF16 = jnp.bfloat16

D_MODEL = 2048
BATCH = 4
SEQ = 2048
DEPTH = 2
EPS = 1e-6
ROWS = BATCH * SEQ

SWA_HEAD_DIM = 64
SWA_HEADS = 16
SWA_KV_HEADS = 4
SWA_GROUP = SWA_HEADS // SWA_KV_HEADS
SWA_WIDTH = SWA_HEADS * SWA_HEAD_DIM
SWA_KV_WIDTH = SWA_KV_HEADS * SWA_HEAD_DIM
WINDOW = 128
ROT_DIM = 16
ROT_HALF = ROT_DIM // 2
ROPE_THETA = 500000.0

SG_WIDTH = 512
SG_GROUPS = 8
SG_GROUP_DIM = 64
SG_CHUNK = 128

GLA_HEADS = 4
GLA_WIDTH = 512
GLA_DV = 128
GLA_DK = 64
GLA_KEY_WIDTH = 256
GLA_GATE_RANK = 16
GLA_GATE_TAU = 16.0
GLA_CHUNK = 128
GLA_LEVELS = 7
GLA_SAFE_DECAY = 40.0

LANES = 128
A_WIDTH = SWA_WIDTH + 2 * SWA_KV_WIDTH
B_WIDTH = 2 * SG_WIDTH
C_WIDTH = 2 * GLA_KEY_WIDTH + GLA_WIDTH
Z_WIDTH = D_MODEL
G_WIDTH = LANES
W_IN_PAD = A_WIDTH + B_WIDTH + C_WIDTH + Z_WIDTH + G_WIDTH
MOD_WIDTH = 3 * D_MODEL

VMEM_LIMIT = 56 * 1024 * 1024
FRONT_VMEM_LIMIT = 60 * 1024 * 1024


def _sigmoid(x):
    return 1.0 / (1.0 + jnp.exp(-x))


def _gelu_tanh(x):
    return 0.5 * x * (1.0 + jnp.tanh(0.7978845608028654 * (x + 0.044715 * (x * x * x))))


def _dot_nt(a, b):
    return lax.dot_general(a, b, (((1,), (1,)), ((), ())), preferred_element_type=F32)


def _dot_tn(a, b):
    return lax.dot_general(a, b, (((0,), (0,)), ((), ())), preferred_element_type=F32)


MOD_TN = 768


def _mod_kernel(c_ref, w_ref, b_ref, o_ref):
    c = c_ref[...]
    s = (c * _sigmoid(c)).astype(BF16)
    o_ref[0] = jnp.dot(s, w_ref[0].astype(BF16), preferred_element_type=F32) + b_ref[0]


def _modulation(c_pad, w_mod, b_mod):
    return pl.pallas_call(
        _mod_kernel,
        grid=(DEPTH, MOD_WIDTH // MOD_TN),
        in_specs=[
            pl.BlockSpec((8, D_MODEL), lambda l, j: (0, 0)),
            pl.BlockSpec((1, D_MODEL, MOD_TN), lambda l, j: (l, 0, j)),
            pl.BlockSpec((1, 1, MOD_TN), lambda l, j: (l, 0, j)),
        ],
        out_specs=pl.BlockSpec((1, 8, MOD_TN), lambda l, j: (l, 0, j)),
        out_shape=jax.ShapeDtypeStruct((DEPTH, 8, MOD_WIDTH), F32),
        compiler_params=pltpu.CompilerParams(
            dimension_semantics=("arbitrary", "arbitrary"), vmem_limit_bytes=VMEM_LIMIT),
        name="adaln_mod",
    )(c_pad, w_mod, b_mod.reshape(DEPTH, 1, MOD_WIDTH))


ROPE_TM = 1024


def _rope_table_kernel(pos_ref, invf_ref, cos_ref, sin_ref):
    ang = pos_ref[...].astype(F32) * invf_ref[...]
    lane = lax.broadcasted_iota(jnp.int32, (1, LANES), 1) % SWA_HEAD_DIM
    s = jnp.sin(ang)
    cos_ref[...] = jnp.cos(ang)
    sin_ref[...] = jnp.where(lane < ROT_HALF, -s, s)


def _rope_tables(positions):
    half = np.arange(ROT_HALF, dtype=np.float32)
    inv_freq = (np.float32(ROPE_THETA) ** (-(half * np.float32(2.0 / ROT_DIM)))).astype(np.float32)
    lane = np.arange(LANES) % SWA_HEAD_DIM
    invf = np.where(lane < ROT_DIM, inv_freq[lane % ROT_HALF], 0.0).astype(np.float32)[None, :]
    return pl.pallas_call(
        _rope_table_kernel,
        grid=(ROWS // ROPE_TM,),
        in_specs=[
            pl.BlockSpec((ROPE_TM, 1), lambda i: (i, 0)),
            pl.BlockSpec((1, LANES), lambda i: (0, 0)),
        ],
        out_specs=[pl.BlockSpec((ROPE_TM, LANES), lambda i: (i, 0))] * 2,
        out_shape=[jax.ShapeDtypeStruct((ROWS, LANES), F32)] * 2,
        compiler_params=pltpu.CompilerParams(dimension_semantics=("arbitrary",)),
        name="rope_tables",
    )(positions.reshape(ROWS, 1), jnp.asarray(invf))


PREP_TN = 512
MAIN_WIDTH = A_WIDTH + B_WIDTH + C_WIDTH + Z_WIDTH
GATE_COL = A_WIDTH + B_WIDTH + C_WIDTH
IN_PROJ_WIDTH = GATE_COL + GLA_GATE_RANK + Z_WIDTH


def _prep_w_in_kernel(a_ref, b_ref, main_ref, gate_ref):
    j = pl.program_id(1)
    first_z = GATE_COL // PREP_TN

    @pl.when(j < first_z)
    def _():
        main_ref[0] = a_ref[0].astype(BF16)

    @pl.when(j >= first_z)
    def _():
        main_ref[0] = jnp.concatenate(
            [a_ref[0, GLA_GATE_RANK:, :], b_ref[0, :GLA_GATE_RANK, :]], axis=0).astype(BF16)

    @pl.when(j == first_z)
    def _():
        gate_ref[0] = jnp.concatenate(
            [a_ref[0, :GLA_GATE_RANK, :], jnp.zeros((G_WIDTH - GLA_GATE_RANK, D_MODEL), F32)],
            axis=0).astype(BF16)


def _prep_w_in(w_in_t):
    first_z = GATE_COL // PREP_TN
    last = (IN_PROJ_WIDTH - 1) // PREP_TN
    return pl.pallas_call(
        _prep_w_in_kernel,
        grid=(DEPTH, MAIN_WIDTH // PREP_TN),
        in_specs=[
            pl.BlockSpec((1, PREP_TN, D_MODEL), lambda l, j: (l, j, 0)),
            pl.BlockSpec((1, PREP_TN, D_MODEL), lambda l, j: (l, jnp.clip(j + 1, first_z + 1, last), 0)),
        ],
        out_specs=[
            pl.BlockSpec((1, PREP_TN, D_MODEL), lambda l, j: (l, j, 0)),
            pl.BlockSpec((1, G_WIDTH, D_MODEL), lambda l, j: (l, 0, 0)),
        ],
        out_shape=[jax.ShapeDtypeStruct((DEPTH, MAIN_WIDTH, D_MODEL), BF16),
                   jax.ShapeDtypeStruct((DEPTH, G_WIDTH, D_MODEL), BF16)],
        compiler_params=pltpu.CompilerParams(
            dimension_semantics=("arbitrary", "arbitrary"), vmem_limit_bytes=VMEM_LIMIT),
        name="prep_w_in",
    )(w_in_t, w_in_t)


def _cast_kernel(x_ref, o_ref):
    o_ref[...] = x_ref[...].astype(o_ref.dtype)


def _prep_w_out(w_out):
    tn = 1024
    return pl.pallas_call(
        _cast_kernel,
        grid=(DEPTH, D_MODEL // tn),
        in_specs=[pl.BlockSpec((1, D_MODEL, tn), lambda l, j: (l, 0, j))],
        out_specs=pl.BlockSpec((1, D_MODEL, tn), lambda l, j: (l, 0, j)),
        out_shape=jax.ShapeDtypeStruct((DEPTH, D_MODEL, D_MODEL), BF16),
        compiler_params=pltpu.CompilerParams(
            dimension_semantics=("arbitrary", "arbitrary"), vmem_limit_bytes=VMEM_LIMIT),
        name="prep_w_out",
    )(w_out)


INPROJ_TM = 512
INPROJ_CHUNK = 512


def _spatial_gating_chunk(u, v, w_bf, bias, ln_g, ln_b):
    u = _gelu_tanh(u)
    v = _gelu_tanh(v)
    mu = jnp.mean(v, axis=-1, keepdims=True)
    vc = v - mu
    var = jnp.mean(vc * vc, axis=-1, keepdims=True)
    vn = (vc * lax.rsqrt(var + EPS) * ln_g + ln_b).astype(BF16)
    parts = [jnp.dot(w_bf[g], vn[:, g * SG_GROUP_DIM:(g + 1) * SG_GROUP_DIM], preferred_element_type=F32)
             for g in range(SG_GROUPS)]
    return u * (jnp.concatenate(parts, axis=1) + bias)


def _swa_pieces(tile_has_prev, a_ref, cos_ref, sin_ref, sinks_ref, ya_ref,
                kband_ref, vband_ref, qr_ref, ksel_ref, vdup_ref, s_ref, p_ref):
    heads_per_tile = LANES // SWA_HEAD_DIM
    n_blocks = INPROJ_TM // WINDOW
    lane = lax.broadcasted_iota(jnp.int32, (1, LANES), 1)
    first_half = (lane % SWA_HEAD_DIM) < ROT_HALF
    low_half = lane < SWA_HEAD_DIM
    qi = lax.broadcasted_iota(jnp.int32, (WINDOW, WINDOW), 0)
    kj = lax.broadcasted_iota(jnp.int32, (WINDOW, WINDOW), 1)
    from_prev = kj > qi

    def rope(t):
        partner = jnp.where(first_half, pltpu.roll(t, LANES - ROT_HALF, 1), pltpu.roll(t, ROT_HALF, 1))
        return t * cos_ref[...] + partner * sin_ref[...]

    steps = []

    def carry():
        kband_ref[0:WINDOW, :] = kband_ref[INPROJ_TM:INPROJ_TM + WINDOW, :]
        vband_ref[0:WINDOW, :] = vband_ref[INPROJ_TM:INPROJ_TM + WINDOW, :]
    steps.append(carry)

    def stage_k(t):
        cols = slice(t * LANES, (t + 1) * LANES)
        kband_ref[WINDOW:, cols] = rope(a_ref[:, SWA_WIDTH + t * LANES:SWA_WIDTH + (t + 1) * LANES])
        vband_ref[WINDOW:, cols] = a_ref[:, SWA_WIDTH + SWA_KV_WIDTH + t * LANES:
                                         SWA_WIDTH + SWA_KV_WIDTH + (t + 1) * LANES]
    for t in range(SWA_KV_WIDTH // LANES):
        steps.append(functools.partial(stage_k, t))

    def stage_q(t):
        cols = slice(t * LANES, (t + 1) * LANES)
        qr_ref[:, cols] = rope(a_ref[:, cols] * (SWA_HEAD_DIM ** -0.5)).astype(BF16)
    for t in range(SWA_WIDTH // LANES):
        steps.append(functools.partial(stage_q, t))

    def stage_kv(blk, t):
        rows = slice(blk * WINDOW, (blk + 2) * WINDOW)
        cols = slice(t * LANES, (t + 1) * LANES)
        kt = kband_ref[rows, cols]
        vt = vband_ref[rows, cols]
        kt_sw = pltpu.roll(kt, SWA_HEAD_DIM, 1)
        vt_sw = pltpu.roll(vt, SWA_HEAD_DIM, 1)
        for u, (k_lo, k_hi, v_lo, v_hi) in enumerate(((kt, kt_sw, vt, vt_sw), (kt_sw, kt, vt_sw, vt))):
            g = t * heads_per_tile + u
            ksel_ref[2 * g] = jnp.where(low_half, k_lo, 0.0).astype(BF16)
            ksel_ref[2 * g + 1] = jnp.where(low_half, 0.0, k_hi).astype(BF16)
            vdup_ref[g] = jnp.where(low_half, v_lo, v_hi).astype(BF16)

    def scores(blk, h):
        t, u = divmod(h, heads_per_tile)
        q = qr_ref[blk * WINDOW:(blk + 1) * WINDOW, t * LANES:(t + 1) * LANES]
        s2 = _dot_nt(q, ksel_ref[2 * (h // SWA_GROUP) + u])
        s_ref[h] = jnp.where(from_prev, s2[:, :WINDOW], s2[:, WINDOW:])

    def softmax(blk, h):
        s = s_ref[h]
        if blk == 0:
            s = jnp.where(kj <= qi + tile_has_prev * WINDOW, s, -jnp.inf)
        sink = sinks_ref[h]
        m = jnp.maximum(jnp.max(s, axis=-1, keepdims=True), sink)
        p = jnp.exp(s - m)
        denom = jnp.sum(p, axis=-1, keepdims=True) + jnp.exp(sink - m)
        p = (p * (1.0 / denom)).astype(BF16)
        zero = jnp.zeros_like(p)
        p_ref[h, :, :WINDOW] = jnp.where(from_prev, p, zero)
        p_ref[h, :, WINDOW:] = jnp.where(from_prev, zero, p)

    def values(blk, t):
        vg = vdup_ref[(t * heads_per_tile) // SWA_GROUP]
        outs = [jnp.dot(p_ref[t * heads_per_tile + u], vg, preferred_element_type=F32)
                for u in range(heads_per_tile)]
        ya_ref[blk * WINDOW:(blk + 1) * WINDOW, t * LANES:(t + 1) * LANES] = (
            jnp.where(low_half, outs[0], outs[1]).astype(ya_ref.dtype))

    for blk in range(n_blocks):
        for t in range(SWA_KV_WIDTH // LANES):
            steps.append(functools.partial(stage_kv, blk, t))
        for h in range(SWA_HEADS):
            steps.append(functools.partial(scores, blk, h))
        for h in range(SWA_HEADS):
            steps.append(functools.partial(softmax, blk, h))
        for t in range(SWA_WIDTH // LANES):
            steps.append(functools.partial(values, blk, t))
    return steps


def _interleave(main_steps, side_steps):
    done = 0
    for idx, step in enumerate(main_steps):
        step()
        upto = (idx + 1) * len(side_steps) // len(main_steps)
        for side in side_steps[done:upto]:
            side()
        done = upto


def _front_kernel(sinks_ref, x_ref, scale_ref, shift_ref, g_ref, w_ref, wg_ref,
                  sgw_ref, sgb_ref, lng_ref, lnb_ref, cos_ref, sin_ref,
                  ya_ref, yb_ref, oc_ref, oz_ref, og_ref,
                  h_ref, uv_ref, a_ref, kband_ref, vband_ref, qr_ref, ksel_ref, vdup_ref, s_ref, p_ref):
    i = pl.program_id(0)

    @pl.when(i == 0)
    def _():
        kband_ref[...] = jnp.zeros_like(kband_ref)
        vband_ref[...] = jnp.zeros_like(vband_ref)

    x = x_ref[...]
    ms = jnp.mean(x * x, axis=-1, keepdims=True)
    y = x * lax.rsqrt(ms + EPS) * g_ref[...]
    h_ref[...] = (y * (1.0 + scale_ref[0]) + shift_ref[0]).astype(BF16)

    def project(o_ref, col, c0, width, post=None):
        r = _dot_nt(h_ref[...], w_ref[0, col + c0:col + c0 + width, :])
        if post is not None:
            r = post(r)
        o_ref[:, c0:c0 + width] = r.astype(o_ref.dtype)

    def projection_steps(o_ref, col, post=None):
        return [functools.partial(project, o_ref, col, c0, INPROJ_CHUNK, post)
                for c0 in range(0, o_ref.shape[1], INPROJ_CHUNK)]

    for step in projection_steps(uv_ref, A_WIDTH):
        step()

    t = lax.broadcasted_iota(jnp.int32, (SG_CHUNK, SG_CHUNK), 0)
    s = lax.broadcasted_iota(jnp.int32, (SG_CHUNK, SG_CHUNK), 1)
    w_bf = [jnp.where(t >= s, sgw_ref[g], 0.0).astype(BF16) for g in range(SG_GROUPS)]

    def mixer_b(r0):
        rows = slice(r0, r0 + SG_CHUNK)
        yb_ref[rows, :] = _spatial_gating_chunk(
            uv_ref[rows, :SG_WIDTH], uv_ref[rows, SG_WIDTH:], w_bf, sgb_ref[...], lng_ref[...],
            lnb_ref[...]).astype(yb_ref.dtype)

    _interleave(projection_steps(a_ref, 0),
                [functools.partial(mixer_b, r0) for r0 in range(0, INPROJ_TM, SG_CHUNK)])

    tile_has_prev = ((i % (SEQ // INPROJ_TM)) != 0).astype(jnp.int32)
    swa_steps = _swa_pieces(tile_has_prev, a_ref, cos_ref, sin_ref, sinks_ref, ya_ref,
                            kband_ref, vband_ref, qr_ref, ksel_ref, vdup_ref, s_ref, p_ref)

    def gate_rank():
        og_ref[...] = _dot_nt(h_ref[...], wg_ref[0])

    silu = lambda z: z * _sigmoid(z)
    _interleave(projection_steps(oc_ref, A_WIDTH + B_WIDTH)
                + projection_steps(oz_ref, A_WIDTH + B_WIDTH + C_WIDTH, silu) + [gate_rank],
                swa_steps)


def _front(layer, x2, scale, shift, g_pre, w_main, w_gate, sg_w, sg_bias_tile, sg_ln_g, sg_ln_b,
           sinks, cos_t, sin_t):
    tiles_per_batch = SEQ // INPROJ_TM
    widths = (SWA_WIDTH, SG_WIDTH, C_WIDTH, Z_WIDTH, G_WIDTH)
    dtypes = (BF16, BF16, BF16, BF16, F32)
    const = pl.Buffered(1)
    return pl.pallas_call(
        _front_kernel,
        grid=(ROWS // INPROJ_TM,),
        in_specs=[
            pl.BlockSpec(memory_space=pltpu.SMEM),
            pl.BlockSpec((INPROJ_TM, D_MODEL), lambda i: (i, 0)),
            pl.BlockSpec((1, 1, D_MODEL), lambda i: (i // tiles_per_batch, 0, 0)),
            pl.BlockSpec((1, 1, D_MODEL), lambda i: (i // tiles_per_batch, 0, 0)),
            pl.BlockSpec((1, D_MODEL), lambda i: (0, 0)),
            pl.BlockSpec((1, MAIN_WIDTH, D_MODEL), lambda i: (layer, 0, 0), pipeline_mode=const),
            pl.BlockSpec((1, G_WIDTH, D_MODEL), lambda i: (layer, 0, 0), pipeline_mode=const),
            pl.BlockSpec((SG_GROUPS, SG_CHUNK, SG_CHUNK), lambda i: (0, 0, 0), pipeline_mode=const),
            pl.BlockSpec((SG_CHUNK, SG_WIDTH), lambda i: (0, 0), pipeline_mode=const),
            pl.BlockSpec((1, SG_WIDTH), lambda i: (0, 0)),
            pl.BlockSpec((1, SG_WIDTH), lambda i: (0, 0)),
            pl.BlockSpec((INPROJ_TM, LANES), lambda i: (i, 0)),
            pl.BlockSpec((INPROJ_TM, LANES), lambda i: (i, 0)),
        ],
        out_specs=[pl.BlockSpec((INPROJ_TM, w), lambda i: (i, 0)) for w in widths],
        out_shape=[jax.ShapeDtypeStruct((ROWS, w), dt) for w, dt in zip(widths, dtypes)],
        scratch_shapes=[
            pltpu.VMEM((INPROJ_TM, D_MODEL), BF16),
            pltpu.VMEM((INPROJ_TM, B_WIDTH), F32),
            pltpu.VMEM((INPROJ_TM, A_WIDTH), F32),
            pltpu.VMEM((INPROJ_TM + WINDOW, SWA_KV_WIDTH), F32),
            pltpu.VMEM((INPROJ_TM + WINDOW, SWA_KV_WIDTH), F32),
            pltpu.VMEM((INPROJ_TM, SWA_WIDTH), BF16),
            pltpu.VMEM((2 * SWA_KV_HEADS, 2 * WINDOW, LANES), BF16),
            pltpu.VMEM((SWA_KV_HEADS, 2 * WINDOW, LANES), BF16),
            pltpu.VMEM((SWA_HEADS, WINDOW, WINDOW), F32),
            pltpu.VMEM((SWA_HEADS, WINDOW, 2 * WINDOW), BF16),
        ],
        compiler_params=pltpu.CompilerParams(
            dimension_semantics=("arbitrary",), vmem_limit_bytes=FRONT_VMEM_LIMIT),
        name="front",
    )(sinks, x2, scale, shift, g_pre, w_main, w_gate, sg_w, sg_bias_tile, sg_ln_g, sg_ln_b, cos_t, sin_t)


def _gla_sum_matrix():
    c = GLA_CHUNK
    mat = np.zeros(((2 + GLA_LEVELS) * c, c), np.float32)
    for t in range(c):
        mat[t, :t + 1] = 1.0
        mat[c + t, t + 1:] = 1.0
        for k in range(GLA_LEVELS):
            m = 1 << k
            r = (t >> (k + 1) << (k + 1)) + m
            row = (2 + k) * c + t
            if (t >> k) & 1:
                mat[row, r + 1:t + 1] = 1.0
            else:
                mat[row, t + 1:r + 1] = 1.0
    return mat


def _split_heads_on_rows(x, low_half):
    zero = jnp.zeros_like(x)
    return jnp.concatenate([jnp.where(low_half, x, zero), jnp.where(low_half, zero, x)], axis=0)


def _gla_kernel(q_ref, k_ref, v_ref, cg_ref, wup_ref, bg_ref, summat_ref, ng_ref, o_ref, st_ref, sc_ref):
    c = GLA_CHUNK
    heads_per_tile = LANES // GLA_DK
    n_tiles = BATCH * GLA_KEY_WIDTH // LANES

    @pl.when(pl.program_id(0) == 0)
    def _():
        st_ref[...] = jnp.zeros_like(st_ref)

    cg = cg_ref[...].reshape(BATCH * c, G_WIDTH).astype(BF16)
    logits = jnp.dot(cg, wup_ref[...], preferred_element_type=F32) + bg_ref[...]
    log_alpha = (jnp.minimum(logits, 0.0) - jnp.log1p(jnp.exp(-jnp.abs(logits)))) * (1.0 / GLA_GATE_TAU)

    hi = log_alpha.astype(BF16)
    r1 = log_alpha - hi.astype(F32)
    mid = r1.astype(BF16)
    lo = (r1 - mid.astype(F32)).astype(BF16)
    pieces = jnp.concatenate(
        [jnp.concatenate([p[b * c:(b + 1) * c] for p in (hi, mid, lo)], axis=0) for b in range(BATCH)],
        axis=1)
    expo = jnp.dot(summat_ref[0:2 * c, :], pieces, preferred_element_type=F32)

    q = jnp.concatenate([q_ref[b] for b in range(BATCH)], axis=1).astype(F32) * (GLA_DK ** -0.5)
    k = jnp.concatenate([k_ref[b] for b in range(BATCH)], axis=1).astype(F32)
    b_cum = expo[0:c]
    q_inter = (q * jnp.exp(b_cum)).astype(BF16)
    k_state = (k * jnp.exp(expo[c:2 * c])).astype(BF16)
    b_last = b_cum[c - 1:c, :]
    decay = jnp.exp(b_last)

    t_i = lax.broadcasted_iota(jnp.int32, (heads_per_tile * c, c), 0) % c
    s_i = lax.broadcasted_iota(jnp.int32, (heads_per_tile * c, c), 1)
    low_half = lax.broadcasted_iota(jnp.int32, (1, LANES), 1) < GLA_DK

    safe = jnp.min(b_last) >= -GLA_SAFE_DECAY

    @pl.when(safe)
    def _():
        k_grown = (k * jnp.exp(-b_cum)).astype(BF16)
        for j in range(n_tiles):
            sl = slice(j * LANES, (j + 1) * LANES)
            a2 = _dot_nt(_split_heads_on_rows(q_inter[:, sl], low_half), k_grown[:, sl])
            sc_ref[j] = jnp.where(s_i <= t_i, a2, 0.0).astype(BF16)

    @pl.when(jnp.logical_not(safe))
    def _():
        lv_expo = jnp.dot(summat_ref[2 * c:, :], pieces, preferred_element_type=F32)
        q_lv, k_lv, masks = [q.astype(BF16)], [k.astype(BF16)], [t_i == s_i]
        for lv in range(GLA_LEVELS):
            f = jnp.exp(lv_expo[lv * c:(lv + 1) * c])
            q_lv.append((q * f).astype(BF16))
            k_lv.append((k * f).astype(BF16))
            masks.append(((t_i >> (lv + 1)) == (s_i >> (lv + 1)))
                         & (((t_i >> lv) & 1) == 1) & (((s_i >> lv) & 1) == 0))
        for j in range(n_tiles):
            sl = slice(j * LANES, (j + 1) * LANES)
            a2 = jnp.zeros((heads_per_tile * c, c), F32)
            for ql, kl, mask in zip(q_lv, k_lv, masks):
                a2 = a2 + jnp.where(mask, _dot_nt(_split_heads_on_rows(ql[:, sl], low_half), kl[:, sl]), 0.0)
            sc_ref[j] = a2.astype(BF16)

    scores = [sc_ref[j] for j in range(n_tiles)]
    tiles_per_seq = GLA_KEY_WIDTH // LANES
    for j in range(n_tiles):
        b, pair = divmod(j, tiles_per_seq)
        sl = slice(j * LANES, (j + 1) * LANES)
        st_cols = slice(pair * LANES, (pair + 1) * LANES)
        st = st_ref[b, :, st_cols]
        inter = _dot_nt(_split_heads_on_rows(q_inter[:, sl], low_half), st.astype(BF16))
        updates = []
        for u in range(heads_per_tile):
            h = pair * heads_per_tile + u
            vs = slice(h * GLA_DV, (h + 1) * GLA_DV)
            vh = v_ref[b, :, vs]
            o = jnp.dot(scores[j][u * c:(u + 1) * c], vh, preferred_element_type=F32) + inter[u * c:(u + 1) * c]
            y = o * lax.rsqrt(jnp.mean(o * o, axis=-1, keepdims=True) + EPS) * ng_ref[...]
            o_ref[b, :, vs] = y.astype(o_ref.dtype)
            updates.append(_dot_tn(vh, k_state[:, sl]))
        st_ref[b, :, st_cols] = st * decay[:, sl] + jnp.where(low_half, updates[0], updates[1])


def _gla(proj_c, proj_g, wup_pad, b_gate, summat3, norm_g):
    return pl.pallas_call(
        _gla_kernel,
        grid=(SEQ // GLA_CHUNK,),
        in_specs=[
            pl.BlockSpec((BATCH, GLA_CHUNK, GLA_KEY_WIDTH), lambda c: (0, c, 0)),
            pl.BlockSpec((BATCH, GLA_CHUNK, GLA_KEY_WIDTH), lambda c: (0, c, 1)),
            pl.BlockSpec((BATCH, GLA_CHUNK, GLA_WIDTH), lambda c: (0, c, 1)),
            pl.BlockSpec((BATCH, GLA_CHUNK, G_WIDTH), lambda c: (0, c, 0)),
            pl.BlockSpec((G_WIDTH, GLA_KEY_WIDTH), lambda c: (0, 0)),
            pl.BlockSpec((1, GLA_KEY_WIDTH), lambda c: (0, 0)),
            pl.BlockSpec(((2 + GLA_LEVELS) * GLA_CHUNK, 3 * GLA_CHUNK), lambda c: (0, 0)),
            pl.BlockSpec((1, GLA_DV), lambda c: (0, 0)),
        ],
        out_specs=pl.BlockSpec((BATCH, GLA_CHUNK, GLA_WIDTH), lambda c: (0, c, 0)),
        out_shape=jax.ShapeDtypeStruct((BATCH, SEQ, GLA_WIDTH), BF16),
        scratch_shapes=[pltpu.VMEM((BATCH, GLA_DV, GLA_KEY_WIDTH), F32),
                        pltpu.VMEM((BATCH * GLA_KEY_WIDTH // LANES, 2 * GLA_CHUNK, GLA_CHUNK), BF16)],
        compiler_params=pltpu.CompilerParams(dimension_semantics=("arbitrary",)),
        name="gla",
    )(proj_c, proj_c, proj_c, proj_g, wup_pad, b_gate, summat3, norm_g)


OUT_TM = 512
OUT_TN = 256


def _outproj_kernel(ya_ref, yb_ref, yc_ref, sz_ref, x_ref, gate_ref, g_ref, w_ref, o_ref,
                    yg_ref, ssq_ref, inv_ref, *acc_refs):
    i = pl.program_id(0)
    n_tiles = ROWS // OUT_TM

    @pl.when(i == 0)
    def _():
        for acc_ref in acc_refs:
            acc_ref[...] = jnp.zeros_like(acc_ref)
        ssq_ref[...] = jnp.zeros_like(ssq_ref)

    inv_ref[...] = lax.rsqrt(ssq_ref[...] * (1.0 / D_MODEL) + EPS)
    ssq_ref[...] = jnp.zeros_like(ssq_ref)
    out_gain = g_ref[...] * gate_ref[0]

    def finalise(blk):
        cols = slice(blk * OUT_TN, (blk + 1) * OUT_TN)
        o_ref[:, cols] = x_ref[:, cols] + acc_refs[blk][...] * inv_ref[:, 0:1] * out_gain[:, cols]

    def matmul(blk):
        r = jnp.dot(yg_ref[...], w_ref[0, :, blk * OUT_TN:(blk + 1) * OUT_TN], preferred_element_type=F32)
        acc_refs[blk][...] = r
        ssq_ref[...] += jnp.sum(r * r, axis=-1, keepdims=True)

    @pl.when(i < n_tiles)
    def _():
        col = 0
        for y_ref in (ya_ref, yb_ref, yc_ref):
            width = y_ref.shape[1]
            yg_ref[:, col:col + width] = y_ref[...] * sz_ref[:, col:col + width]
            col += width
        for blk in range(len(acc_refs)):
            finalise(blk)
            matmul(blk)

    @pl.when(i == n_tiles)
    def _():
        for blk in range(len(acc_refs)):
            finalise(blk)


def _outproj(layer, y_a, y_b, y_c, silu_z, x2, gate, g_post, w_out_bf):
    tiles_per_batch = SEQ // OUT_TM
    n_tiles = ROWS // OUT_TM
    mixed = lambda i: jnp.minimum(i, n_tiles - 1)
    done = lambda i: jnp.maximum(i - 1, 0)
    return pl.pallas_call(
        _outproj_kernel,
        grid=(n_tiles + 1,),
        in_specs=[
            pl.BlockSpec((OUT_TM, SWA_WIDTH), lambda i: (mixed(i), 0)),
            pl.BlockSpec((OUT_TM, SG_WIDTH), lambda i: (mixed(i), 0)),
            pl.BlockSpec((OUT_TM, GLA_WIDTH), lambda i: (mixed(i), 0)),
            pl.BlockSpec((OUT_TM, Z_WIDTH), lambda i: (mixed(i), 0)),
            pl.BlockSpec((OUT_TM, D_MODEL), lambda i: (done(i), 0)),
            pl.BlockSpec((1, 1, D_MODEL), lambda i: (done(i) // tiles_per_batch, 0, 0)),
            pl.BlockSpec((1, D_MODEL), lambda i: (0, 0)),
            pl.BlockSpec((1, D_MODEL, D_MODEL), lambda i: (layer, 0, 0), pipeline_mode=pl.Buffered(1)),
        ],
        out_specs=pl.BlockSpec((OUT_TM, D_MODEL), lambda i: (done(i), 0)),
        out_shape=jax.ShapeDtypeStruct((ROWS, D_MODEL), F32),
        scratch_shapes=(
            [pltpu.VMEM((OUT_TM, D_MODEL), BF16),
             pltpu.VMEM((OUT_TM, LANES), F32),
             pltpu.VMEM((OUT_TM, LANES), F32)]
            + [pltpu.VMEM((OUT_TM, OUT_TN), F32)] * (D_MODEL // OUT_TN)),
        compiler_params=pltpu.CompilerParams(
            dimension_semantics=("arbitrary",), vmem_limit_bytes=VMEM_LIMIT),
        name="gate_outproj_residual",
    )(y_a, y_b, y_c, silu_z, x2, gate, g_post, w_out_bf)


def kernel(x, c, positions, w_mod, b_mod, g_pre, g_post, w_in, w_out, swa_sinks,
           sg_w, sg_b, sg_ln_g, sg_ln_b, gla_w_gate_up, gla_b_gate, gla_norm_g):
    assert x.shape == (BATCH, SEQ, D_MODEL) and w_in.shape[0] == DEPTH

    c_pad = jnp.pad(c, ((0, 8 - BATCH), (0, 0)))
    mod = _modulation(c_pad, w_mod, b_mod)[:, :BATCH, :]
    cos_t, sin_t = _rope_tables(positions)
    summat3 = jnp.asarray(np.tile(_gla_sum_matrix(), (1, 3)), dtype=BF16)

    w_main, w_gate = _prep_w_in(jnp.swapaxes(w_in, 1, 2))
    w_out_bf = _prep_w_out(w_out)

    x2 = x.reshape(ROWS, D_MODEL)
    for l in range(DEPTH):
        shift = mod[l, :, 0:D_MODEL].reshape(BATCH, 1, D_MODEL)
        scale = mod[l, :, D_MODEL:2 * D_MODEL].reshape(BATCH, 1, D_MODEL)
        gate = mod[l, :, 2 * D_MODEL:].reshape(BATCH, 1, D_MODEL)
        bias_tile = jnp.repeat(sg_b[l].T, SG_GROUP_DIM, axis=1)
        y_a, y_b, proj_c, silu_z, proj_g = _front(
            l, x2, scale, shift, g_pre[l].reshape(1, D_MODEL), w_main, w_gate, sg_w[l], bias_tile,
            sg_ln_g[l].reshape(1, SG_WIDTH), sg_ln_b[l].reshape(1, SG_WIDTH), swa_sinks[l], cos_t, sin_t)
        wup_pad = jnp.pad(gla_w_gate_up[l], ((0, G_WIDTH - GLA_GATE_RANK), (0, 0))).astype(BF16)
        y_c = _gla(proj_c.reshape(BATCH, SEQ, C_WIDTH), proj_g.reshape(BATCH, SEQ, G_WIDTH), wup_pad,
                   gla_b_gate[l].reshape(1, GLA_KEY_WIDTH), summat3,
                   gla_norm_g[l].reshape(1, GLA_DV)).reshape(ROWS, GLA_WIDTH)

        x2 = _outproj(l, y_a, y_b, y_c, silu_z, x2, gate, g_post[l].reshape(1, D_MODEL), w_out_bf)
    return x2.reshape(BATCH, SEQ, D_MODEL)
```

```python
import functools

import numpy as np
import jax
import jax.numpy as jnp
from jax import lax
from jax.experimental import pallas as pl
from jax.experimental.pallas import tpu as pltpu

F32 = jnp.float32
BF16 = jnp.bfloat16

D_MODEL = 2048
BATCH = 4
SEQ = 2048
DEPTH = 2
EPS = 1e-6
ROWS = BATCH * SEQ

SWA_HEAD_DIM = 64
SWA_HEADS = 16
SWA_KV_HEADS = 4
SWA_GROUP = SWA_HEADS // SWA_KV_HEADS
SWA_WIDTH = SWA_HEADS * SWA_HEAD_DIM
SWA_KV_WIDTH = SWA_KV_HEADS * SWA_HEAD_DIM
WINDOW = 128
ROT_DIM = 16
ROT_HALF = ROT_DIM // 2
ROPE_THETA = 500000.0

SG_WIDTH = 512
SG_GROUPS = 8
SG_GROUP_DIM = 64
SG_CHUNK = 128

GLA_HEADS = 4
GLA_WIDTH = 512
GLA_DV = 128
GLA_DK = 64
GLA_KEY_WIDTH = 256
GLA_GATE_RANK = 16
GLA_GATE_TAU = 16.0
GLA_CHUNK = 128
GLA_LEVELS = 7
GLA_SAFE_DECAY = 40.0

LANES = 128
A_WIDTH = SWA_WIDTH + 2 * SWA_KV_WIDTH
B_WIDTH = 2 * SG_WIDTH
C_WIDTH = 2 * GLA_KEY_WIDTH + GLA_WIDTH
Z_WIDTH = D_MODEL
G_WIDTH = LANES
W_IN_PAD = A_WIDTH + B_WIDTH + C_WIDTH + Z_WIDTH + G_WIDTH
MOD_WIDTH = 3 * D_MODEL

VMEM_LIMIT = 56 * 1024 * 1024
FRONT_VMEM_LIMIT = 60 * 1024 * 1024


def _sigmoid(x):
    return 1.0 / (1.0 + jnp.exp(-x))


def _gelu_tanh(x):
    return 0.5 * x * (1.0 + jnp.tanh(0.7978845608028654 * (x + 0.044715 * (x * x * x))))


def _dot_nt(a, b):
    return lax.dot_general(a, b, (((1,), (1,)), ((), ())), preferred_element_type=F32)


def _dot_tn(a, b):
    return lax.dot_general(a, b, (((0,), (0,)), ((), ())), preferred_element_type=F32)


MOD_TN = 768


def _mod_kernel(c_ref, w_ref, b_ref, o_ref):
    c = c_ref[...]
    s = (c * _sigmoid(c)).astype(BF16)
    o_ref[0] = jnp.dot(s, w_ref[0].astype(BF16), preferred_element_type=F32) + b_ref[0]


def _modulation(c_pad, w_mod, b_mod):
    return pl.pallas_call(
        _mod_kernel,
        grid=(DEPTH, MOD_WIDTH // MOD_TN),
        in_specs=[
            pl.BlockSpec((8, D_MODEL), lambda l, j: (0, 0)),
            pl.BlockSpec((1, D_MODEL, MOD_TN), lambda l, j: (l, 0, j)),
            pl.BlockSpec((1, 1, MOD_TN), lambda l, j: (l, 0, j)),
        ],
        out_specs=pl.BlockSpec((1, 8, MOD_TN), lambda l, j: (l, 0, j)),
        out_shape=jax.ShapeDtypeStruct((DEPTH, 8, MOD_WIDTH), F32),
        compiler_params=pltpu.CompilerParams(
            dimension_semantics=("arbitrary", "arbitrary"), vmem_limit_bytes=VMEM_LIMIT),
        name="adaln_mod",
    )(c_pad, w_mod, b_mod.reshape(DEPTH, 1, MOD_WIDTH))


ROPE_TM = 1024


def _rope_table_kernel(pos_ref, invf_ref, cos_ref, sin_ref):
    ang = pos_ref[...].astype(F32) * invf_ref[...]
    lane = lax.broadcasted_iota(jnp.int32, (1, LANES), 1) % SWA_HEAD_DIM
    s = jnp.sin(ang)
    cos_ref[...] = jnp.cos(ang)
    sin_ref[...] = jnp.where(lane < ROT_HALF, -s, s)


def _rope_tables(positions):
    half = np.arange(ROT_HALF, dtype=np.float32)
    inv_freq = (np.float32(ROPE_THETA) ** (-(half * np.float32(2.0 / ROT_DIM)))).astype(np.float32)
    lane = np.arange(LANES) % SWA_HEAD_DIM
    invf = np.where(lane < ROT_DIM, inv_freq[lane % ROT_HALF], 0.0).astype(np.float32)[None, :]
    return pl.pallas_call(
        _rope_table_kernel,
        grid=(ROWS // ROPE_TM,),
        in_specs=[
            pl.BlockSpec((ROPE_TM, 1), lambda i: (i, 0)),
            pl.BlockSpec((1, LANES), lambda i: (0, 0)),
        ],
        out_specs=[pl.BlockSpec((ROPE_TM, LANES), lambda i: (i, 0))] * 2,
        out_shape=[jax.ShapeDtypeStruct((ROWS, LANES), F32)] * 2,
        compiler_params=pltpu.CompilerParams(dimension_semantics=("arbitrary",)),
        name="rope_tables",
    )(positions.reshape(ROWS, 1), jnp.asarray(invf))


PREP_TN = 512
MAIN_WIDTH = A_WIDTH + B_WIDTH + C_WIDTH + Z_WIDTH
GATE_COL = A_WIDTH + B_WIDTH + C_WIDTH
IN_PROJ_WIDTH = GATE_COL + GLA_GATE_RANK + Z_WIDTH


def _prep_w_in_kernel(a_ref, b_ref, main_ref, gate_ref):
    j = pl.program_id(1)
    first_z = GATE_COL // PREP_TN

    @pl.when(j < first_z)
    def _():
        main_ref[0] = a_ref[0].astype(BF16)

    @pl.when(j >= first_z)
    def _():
        main_ref[0] = jnp.concatenate(
            [a_ref[0, GLA_GATE_RANK:, :], b_ref[0, :GLA_GATE_RANK, :]], axis=0).astype(BF16)

    @pl.when(j == first_z)
    def _():
        gate_ref[0] = jnp.concatenate(
            [a_ref[0, :GLA_GATE_RANK, :], jnp.zeros((G_WIDTH - GLA_GATE_RANK, D_MODEL), F32)],
            axis=0).astype(BF16)


def _prep_w_in(w_in_t):
    first_z = GATE_COL // PREP_TN
    last = (IN_PROJ_WIDTH - 1) // PREP_TN
    return pl.pallas_call(
        _prep_w_in_kernel,
        grid=(1, MAIN_WIDTH // PREP_TN),
        in_specs=[
            pl.BlockSpec((1, PREP_TN, D_MODEL), lambda l, j: (l, j, 0)),
            pl.BlockSpec((1, PREP_TN, D_MODEL), lambda l, j: (l, jnp.clip(j + 1, first_z + 1, last), 0)),
        ],
        out_specs=[
            pl.BlockSpec((1, PREP_TN, D_MODEL), lambda l, j: (l, j, 0)),
            pl.BlockSpec((1, G_WIDTH, D_MODEL), lambda l, j: (l, 0, 0)),
        ],
        out_shape=[jax.ShapeDtypeStruct((1, MAIN_WIDTH, D_MODEL), BF16),
                   jax.ShapeDtypeStruct((1, G_WIDTH, D_MODEL), BF16)],
        compiler_params=pltpu.CompilerParams(
            dimension_semantics=("arbitrary", "arbitrary"), vmem_limit_bytes=VMEM_LIMIT),
        name="prep_w_in",
    )(w_in_t, w_in_t)


def _cast_kernel(x_ref, o_ref):
    o_ref[...] = x_ref[...].astype(o_ref.dtype)


def _prep_w_out(w_out):
    tn = 1024
    return pl.pallas_call(
        _cast_kernel,
        grid=(1, D_MODEL // tn),
        in_specs=[pl.BlockSpec((1, D_MODEL, tn), lambda l, j: (l, 0, j))],
        out_specs=pl.BlockSpec((1, D_MODEL, tn), lambda l, j: (l, 0, j)),
        out_shape=jax.ShapeDtypeStruct((1, D_MODEL, D_MODEL), BF16),
        compiler_params=pltpu.CompilerParams(
            dimension_semantics=("arbitrary", "arbitrary"), vmem_limit_bytes=VMEM_LIMIT),
        name="prep_w_out",
    )(w_out)


INPROJ_TM = 512
INPROJ_CHUNK = 512


def _spatial_gating_chunk(u, v, w_bf, bias, ln_g, ln_b):
    u = _gelu_tanh(u)
    v = _gelu_tanh(v)
    mu = jnp.mean(v, axis=-1, keepdims=True)
    vc = v - mu
    var = jnp.mean(vc * vc, axis=-1, keepdims=True)
    vn = (vc * lax.rsqrt(var + EPS) * ln_g + ln_b).astype(BF16)
    parts = [jnp.dot(w_bf[g], vn[:, g * SG_GROUP_DIM:(g + 1) * SG_GROUP_DIM], preferred_element_type=F32)
             for g in range(SG_GROUPS)]
    return u * (jnp.concatenate(parts, axis=1) + bias)


def _swa_pieces(tile_has_prev, a_ref, cos_ref, sin_ref, sinks_ref, ya_ref,
                kband_ref, vband_ref, qr_ref, ksel_ref, vdup_ref, s_ref, p_ref):
    heads_per_tile = LANES // SWA_HEAD_DIM
    n_blocks = INPROJ_TM // WINDOW
    lane = lax.broadcasted_iota(jnp.int32, (1, LANES), 1)
    first_half = (lane % SWA_HEAD_DIM) < ROT_HALF
    low_half = lane < SWA_HEAD_DIM
    qi = lax.broadcasted_iota(jnp.int32, (WINDOW, WINDOW), 0)
    kj = lax.broadcasted_iota(jnp.int32, (WINDOW, WINDOW), 1)
    from_prev = kj > qi

    def rope(t):
        partner = jnp.where(first_half, pltpu.roll(t, LANES - ROT_HALF, 1), pltpu.roll(t, ROT_HALF, 1))
        return t * cos_ref[...] + partner * sin_ref[...]

    steps = []

    def carry():
        kband_ref[0:WINDOW, :] = kband_ref[INPROJ_TM:INPROJ_TM + WINDOW, :]
        vband_ref[0:WINDOW, :] = vband_ref[INPROJ_TM:INPROJ_TM + WINDOW, :]
    steps.append(carry)

    def stage_k(t):
        cols = slice(t * LANES, (t + 1) * LANES)
        kband_ref[WINDOW:, cols] = rope(a_ref[:, SWA_WIDTH + t * LANES:SWA_WIDTH + (t + 1) * LANES])
        vband_ref[WINDOW:, cols] = a_ref[:, SWA_WIDTH + SWA_KV_WIDTH + t * LANES:
                                         SWA_WIDTH + SWA_KV_WIDTH + (t + 1) * LANES]
    for t in range(SWA_KV_WIDTH // LANES):
        steps.append(functools.partial(stage_k, t))

    def stage_q(t):
        cols = slice(t * LANES, (t + 1) * LANES)
        qr_ref[:, cols] = rope(a_ref[:, cols] * (SWA_HEAD_DIM ** -0.5)).astype(BF16)
    for t in range(SWA_WIDTH // LANES):
        steps.append(functools.partial(stage_q, t))

    def stage_kv(blk, t):
        rows = slice(blk * WINDOW, (blk + 2) * WINDOW)
        cols = slice(t * LANES, (t + 1) * LANES)
        kt = kband_ref[rows, cols]
        vt = vband_ref[rows, cols]
        kt_sw = pltpu.roll(kt, SWA_HEAD_DIM, 1)
        vt_sw = pltpu.roll(vt, SWA_HEAD_DIM, 1)
        for u, (k_lo, k_hi, v_lo, v_hi) in enumerate(((kt, kt_sw, vt, vt_sw), (kt_sw, kt, vt_sw, vt))):
            g = t * heads_per_tile + u
            ksel_ref[2 * g] = jnp.where(low_half, k_lo, 0.0).astype(BF16)
            ksel_ref[2 * g + 1] = jnp.where(low_half, 0.0, k_hi).astype(BF16)
            vdup_ref[g] = jnp.where(low_half, v_lo, v_hi).astype(BF16)

    def scores(blk, h):
        t, u = divmod(h, heads_per_tile)
        q = qr_ref[blk * WINDOW:(blk + 1) * WINDOW, t * LANES:(t + 1) * LANES]
        s2 = _dot_nt(q, ksel_ref[2 * (h // SWA_GROUP) + u])
        s_ref[h] = jnp.where(from_prev, s2[:, :WINDOW], s2[:, WINDOW:])

    def softmax(blk, h):
        s = s_ref[h]
        if blk == 0:
            s = jnp.where(kj <= qi + tile_has_prev * WINDOW, s, -jnp.inf)
        sink = sinks_ref[h]
        m = jnp.maximum(jnp.max(s, axis=-1, keepdims=True), sink)
        p = jnp.exp(s - m)
        denom = jnp.sum(p, axis=-1, keepdims=True) + jnp.exp(sink - m)
        p = (p * (1.0 / denom)).astype(BF16)
        zero = jnp.zeros_like(p)
        p_ref[h, :, :WINDOW] = jnp.where(from_prev, p, zero)
        p_ref[h, :, WINDOW:] = jnp.where(from_prev, zero, p)

    def values(blk, t):
        vg = vdup_ref[(t * heads_per_tile) // SWA_GROUP]
        outs = [jnp.dot(p_ref[t * heads_per_tile + u], vg, preferred_element_type=F32)
                for u in range(heads_per_tile)]
        ya_ref[blk * WINDOW:(blk + 1) * WINDOW, t * LANES:(t + 1) * LANES] = (
            jnp.where(low_half, outs[0], outs[1]).astype(ya_ref.dtype))

    for blk in range(n_blocks):
        for t in range(SWA_KV_WIDTH // LANES):
            steps.append(functools.partial(stage_kv, blk, t))
        for h in range(SWA_HEADS):
            steps.append(functools.partial(scores, blk, h))
        for h in range(SWA_HEADS):
            steps.append(functools.partial(softmax, blk, h))
        for t in range(SWA_WIDTH // LANES):
            steps.append(functools.partial(values, blk, t))
    return steps


def _interleave(main_steps, side_steps):
    done = 0
    for idx, step in enumerate(main_steps):
        step()
        upto = (idx + 1) * len(side_steps) // len(main_steps)
        for side in side_steps[done:upto]:
            side()
        done = upto


def _front_kernel(sinks_ref, x_ref, scale_ref, shift_ref, g_ref, w_ref, wg_ref,
                  sgw_ref, sgb_ref, lng_ref, lnb_ref, cos_ref, sin_ref,
                  ya_ref, yb_ref, oc_ref, oz_ref, og_ref,
                  h_ref, uv_ref, a_ref, kband_ref, vband_ref, qr_ref, ksel_ref, vdup_ref, s_ref, p_ref):
    i = pl.program_id(0)

    @pl.when(i == 0)
    def _():
        kband_ref[...] = jnp.zeros_like(kband_ref)
        vband_ref[...] = jnp.zeros_like(vband_ref)

    x = x_ref[...]
    ms = jnp.mean(x * x, axis=-1, keepdims=True)
    y = x * lax.rsqrt(ms + EPS) * g_ref[...]
    h_ref[...] = (y * (1.0 + scale_ref[0]) + shift_ref[0]).astype(BF16)

    def project(o_ref, col, c0, width, post=None):
        r = _dot_nt(h_ref[...], w_ref[0, col + c0:col + c0 + width, :])
        if post is not None:
            r = post(r)
        o_ref[:, c0:c0 + width] = r.astype(o_ref.dtype)

    def projection_steps(o_ref, col, post=None):
        return [functools.partial(project, o_ref, col, c0, INPROJ_CHUNK, post)
                for c0 in range(0, o_ref.shape[1], INPROJ_CHUNK)]

    for step in projection_steps(uv_ref, A_WIDTH):
        step()

    t = lax.broadcasted_iota(jnp.int32, (SG_CHUNK, SG_CHUNK), 0)
    s = lax.broadcasted_iota(jnp.int32, (SG_CHUNK, SG_CHUNK), 1)
    w_bf = [jnp.where(t >= s, sgw_ref[g], 0.0).astype(BF16) for g in range(SG_GROUPS)]

    def mixer_b(r0):
        rows = slice(r0, r0 + SG_CHUNK)
        yb_ref[rows, :] = _spatial_gating_chunk(
            uv_ref[rows, :SG_WIDTH], uv_ref[rows, SG_WIDTH:], w_bf, sgb_ref[...], lng_ref[...],
            lnb_ref[...]).astype(yb_ref.dtype)

    _interleave(projection_steps(a_ref, 0),
                [functools.partial(mixer_b, r0) for r0 in range(0, INPROJ_TM, SG_CHUNK)])

    tile_has_prev = ((i % (SEQ // INPROJ_TM)) != 0).astype(jnp.int32)
    swa_steps = _swa_pieces(tile_has_prev, a_ref, cos_ref, sin_ref, sinks_ref, ya_ref,
                            kband_ref, vband_ref, qr_ref, ksel_ref, vdup_ref, s_ref, p_ref)

    def gate_rank():
        og_ref[...] = _dot_nt(h_ref[...], wg_ref[0])

    silu = lambda z: z * (0.5 * jnp.tanh(0.5 * z) + 0.5)
    _interleave(projection_steps(oc_ref, A_WIDTH + B_WIDTH)
                + projection_steps(oz_ref, A_WIDTH + B_WIDTH + C_WIDTH, silu) + [gate_rank],
                swa_steps)


def _front(layer, x2, scale, shift, g_pre, w_main, w_gate, sg_w, sg_bias_tile, sg_ln_g, sg_ln_b,
           sinks, cos_t, sin_t):
    tiles_per_batch = SEQ // INPROJ_TM
    widths = (SWA_WIDTH, SG_WIDTH, C_WIDTH, Z_WIDTH, G_WIDTH)
    dtypes = (BF16, BF16, BF16, BF16, F32)
    const = pl.Buffered(1)
    return pl.pallas_call(
        _front_kernel,
        grid=(ROWS // INPROJ_TM,),
        in_specs=[
            pl.BlockSpec(memory_space=pltpu.SMEM),
            pl.BlockSpec((INPROJ_TM, D_MODEL), lambda i: (i, 0)),
            pl.BlockSpec((1, 1, D_MODEL), lambda i: (i // tiles_per_batch, 0, 0)),
            pl.BlockSpec((1, 1, D_MODEL), lambda i: (i // tiles_per_batch, 0, 0)),
            pl.BlockSpec((1, D_MODEL), lambda i: (0, 0)),
            pl.BlockSpec((1, MAIN_WIDTH, D_MODEL), lambda i: (layer, 0, 0), pipeline_mode=const),
            pl.BlockSpec((1, G_WIDTH, D_MODEL), lambda i: (layer, 0, 0), pipeline_mode=const),
            pl.BlockSpec((SG_GROUPS, SG_CHUNK, SG_CHUNK), lambda i: (0, 0, 0), pipeline_mode=const),
            pl.BlockSpec((SG_CHUNK, SG_WIDTH), lambda i: (0, 0), pipeline_mode=const),
            pl.BlockSpec((1, SG_WIDTH), lambda i: (0, 0)),
            pl.BlockSpec((1, SG_WIDTH), lambda i: (0, 0)),
            pl.BlockSpec((INPROJ_TM, LANES), lambda i: (i, 0)),
            pl.BlockSpec((INPROJ_TM, LANES), lambda i: (i, 0)),
        ],
        out_specs=[pl.BlockSpec((INPROJ_TM, w), lambda i: (i, 0)) for w in widths],
        out_shape=[jax.ShapeDtypeStruct((ROWS, w), dt) for w, dt in zip(widths, dtypes)],
        scratch_shapes=[
            pltpu.VMEM((INPROJ_TM, D_MODEL), BF16),
            pltpu.VMEM((INPROJ_TM, B_WIDTH), F32),
            pltpu.VMEM((INPROJ_TM, A_WIDTH), F32),
            pltpu.VMEM((INPROJ_TM + WINDOW, SWA_KV_WIDTH), F32),
            pltpu.VMEM((INPROJ_TM + WINDOW, SWA_KV_WIDTH), F32),
            pltpu.VMEM((INPROJ_TM, SWA_WIDTH), BF16),
            pltpu.VMEM((2 * SWA_KV_HEADS, 2 * WINDOW, LANES), BF16),
            pltpu.VMEM((SWA_KV_HEADS, 2 * WINDOW, LANES), BF16),
            pltpu.VMEM((SWA_HEADS, WINDOW, WINDOW), F32),
            pltpu.VMEM((SWA_HEADS, WINDOW, 2 * WINDOW), BF16),
        ],
        compiler_params=pltpu.CompilerParams(
            dimension_semantics=("arbitrary",), vmem_limit_bytes=FRONT_VMEM_LIMIT),
        name="front",
    )(sinks, x2, scale, shift, g_pre, w_main, w_gate, sg_w, sg_bias_tile, sg_ln_g, sg_ln_b, cos_t, sin_t)


def _gla_sum_matrix():
    c = GLA_CHUNK
    mat = np.zeros(((2 + GLA_LEVELS) * c, c), np.float32)
    for t in range(c):
        mat[t, :t + 1] = 1.0
        mat[c + t, t + 1:] = 1.0
        for k in range(GLA_LEVELS):
            m = 1 << k
            r = (t >> (k + 1) << (k + 1)) + m
            row = (2 + k) * c + t
            if (t >> k) & 1:
                mat[row, r + 1:t + 1] = 1.0
            else:
                mat[row, t + 1:r + 1] = 1.0
    return mat


def _split_heads_on_rows(x, low_half):
    zero = jnp.zeros_like(x)
    return jnp.concatenate([jnp.where(low_half, x, zero), jnp.where(low_half, zero, x)], axis=0)


def _gla_kernel(q_ref, k_ref, v_ref, cg_ref, wup_ref, bg_ref, summat_ref, ng_ref, o_ref, st_ref, sc_ref):
    c = GLA_CHUNK
    heads_per_tile = LANES // GLA_DK
    n_tiles = BATCH * GLA_KEY_WIDTH // LANES

    @pl.when(pl.program_id(0) == 0)
    def _():
        st_ref[...] = jnp.zeros_like(st_ref)

    cg = cg_ref[...].reshape(BATCH * c, G_WIDTH).astype(BF16)
    logits = jnp.dot(cg, wup_ref[...], preferred_element_type=F32) + bg_ref[...]
    log_alpha = (jnp.minimum(logits, 0.0) - jnp.log1p(jnp.exp(-jnp.abs(logits)))) * (1.0 / GLA_GATE_TAU)

    hi = log_alpha.astype(BF16)
    r1 = log_alpha - hi.astype(F32)
    mid = r1.astype(BF16)
    lo = (r1 - mid.astype(F32)).astype(BF16)
    pieces = jnp.concatenate(
        [jnp.concatenate([p[b * c:(b + 1) * c] for p in (hi, mid, lo)], axis=0) for b in range(BATCH)],
        axis=1)
    expo = jnp.dot(summat_ref[0:2 * c, :], pieces, preferred_element_type=F32)

    q = jnp.concatenate([q_ref[b] for b in range(BATCH)], axis=1).astype(F32) * (GLA_DK ** -0.5)
    k = jnp.concatenate([k_ref[b] for b in range(BATCH)], axis=1).astype(F32)
    b_cum = expo[0:c]
    q_inter = (q * jnp.exp(b_cum)).astype(BF16)
    k_state = (k * jnp.exp(expo[c:2 * c])).astype(BF16)
    b_last = b_cum[c - 1:c, :]
    decay = jnp.exp(b_last)

    t_i = lax.broadcasted_iota(jnp.int32, (heads_per_tile * c, c), 0) % c
    s_i = lax.broadcasted_iota(jnp.int32, (heads_per_tile * c, c), 1)
    low_half = lax.broadcasted_iota(jnp.int32, (1, LANES), 1) < GLA_DK

    safe = jnp.min(b_last) >= -GLA_SAFE_DECAY

    @pl.when(safe)
    def _():
        k_grown = (k * jnp.exp(-b_cum)).astype(BF16)
        for j in range(n_tiles):
            sl = slice(j * LANES, (j + 1) * LANES)
            a2 = _dot_nt(_split_heads_on_rows(q_inter[:, sl], low_half), k_grown[:, sl])
            sc_ref[j] = jnp.where(s_i <= t_i, a2, 0.0).astype(BF16)

    @pl.when(jnp.logical_not(safe))
    def _():
        lv_expo = jnp.dot(summat_ref[2 * c:, :], pieces, preferred_element_type=F32)
        q_lv, k_lv, masks = [q.astype(BF16)], [k.astype(BF16)], [t_i == s_i]
        for lv in range(GLA_LEVELS):
            f = jnp.exp(lv_expo[lv * c:(lv + 1) * c])
            q_lv.append((q * f).astype(BF16))
            k_lv.append((k * f).astype(BF16))
            masks.append(((t_i >> (lv + 1)) == (s_i >> (lv + 1)))
                         & (((t_i >> lv) & 1) == 1) & (((s_i >> lv) & 1) == 0))
        for j in range(n_tiles):
            sl = slice(j * LANES, (j + 1) * LANES)
            a2 = jnp.zeros((heads_per_tile * c, c), F32)
            for ql, kl, mask in zip(q_lv, k_lv, masks):
                a2 = a2 + jnp.where(mask, _dot_nt(_split_heads_on_rows(ql[:, sl], low_half), kl[:, sl]), 0.0)
            sc_ref[j] = a2.astype(BF16)

    scores = [sc_ref[j] for j in range(n_tiles)]
    tiles_per_seq = GLA_KEY_WIDTH // LANES
    for j in range(n_tiles):
        b, pair = divmod(j, tiles_per_seq)
        sl = slice(j * LANES, (j + 1) * LANES)
        st_cols = slice(pair * LANES, (pair + 1) * LANES)
        st = st_ref[b, :, st_cols]
        inter = _dot_nt(_split_heads_on_rows(q_inter[:, sl], low_half), st.astype(BF16))
        updates = []
        for u in range(heads_per_tile):
            h = pair * heads_per_tile + u
            vs = slice(h * GLA_DV, (h + 1) * GLA_DV)
            vh = v_ref[b, :, vs]
            o = jnp.dot(scores[j][u * c:(u + 1) * c], vh, preferred_element_type=F32) + inter[u * c:(u + 1) * c]
            y = o * lax.rsqrt(jnp.mean(o * o, axis=-1, keepdims=True) + EPS) * ng_ref[...]
            o_ref[b, :, vs] = y.astype(o_ref.dtype)
            updates.append(_dot_tn(vh, k_state[:, sl]))
        st_ref[b, :, st_cols] = st * decay[:, sl] + jnp.where(low_half, updates[0], updates[1])


GLA_STEPS = SEQ // GLA_CHUNK
HOST_IN_ROWS = MAIN_WIDTH // GLA_STEPS
HOST_OUT_ROWS = D_MODEL // GLA_STEPS


def _gla_hosting_kernel(q_ref, k_ref, v_ref, cg_ref, wup_ref, bg_ref, summat_ref, ng_ref,
                        wa_ref, wb_ref, wo_ref, o_ref, main_ref, gate_ref, wout_ref, st_ref, sc_ref):
    _gla_kernel(q_ref, k_ref, v_ref, cg_ref, wup_ref, bg_ref, summat_ref, ng_ref, o_ref, st_ref, sc_ref)

    c = pl.program_id(0)
    mixed_blk, split = divmod(GATE_COL, HOST_IN_ROWS)
    tail = lambda lo: jnp.concatenate([wa_ref[0, lo + GLA_GATE_RANK:, :], wb_ref[0, :GLA_GATE_RANK, :]], axis=0)

    @pl.when(c < mixed_blk)
    def _():
        main_ref[0] = wa_ref[0].astype(BF16)

    @pl.when(c == mixed_blk)
    def _():
        main_ref[0] = jnp.concatenate([wa_ref[0, :split, :], tail(split)], axis=0).astype(BF16)
        gate_ref[0] = jnp.concatenate(
            [wa_ref[0, split:split + GLA_GATE_RANK, :], jnp.zeros((G_WIDTH - GLA_GATE_RANK, D_MODEL), F32)],
            axis=0).astype(BF16)

    @pl.when(c > mixed_blk)
    def _():
        main_ref[0] = tail(0).astype(BF16)

    wout_ref[0] = wo_ref[0].astype(BF16)


def _gla(proj_c, proj_g, wup_pad, b_gate, summat3, norm_g, next_layer=None, w_in_t=None, w_out=None):
    in_specs = [
        pl.BlockSpec((BATCH, GLA_CHUNK, GLA_KEY_WIDTH), lambda c: (0, c, 0)),
        pl.BlockSpec((BATCH, GLA_CHUNK, GLA_KEY_WIDTH), lambda c: (0, c, 1)),
        pl.BlockSpec((BATCH, GLA_CHUNK, GLA_WIDTH), lambda c: (0, c, 1)),
        pl.BlockSpec((BATCH, GLA_CHUNK, G_WIDTH), lambda c: (0, c, 0)),
        pl.BlockSpec((G_WIDTH, GLA_KEY_WIDTH), lambda c: (0, 0)),
        pl.BlockSpec((1, GLA_KEY_WIDTH), lambda c: (0, 0)),
        pl.BlockSpec(((2 + GLA_LEVELS) * GLA_CHUNK, 3 * GLA_CHUNK), lambda c: (0, 0)),
        pl.BlockSpec((1, GLA_DV), lambda c: (0, 0)),
    ]
    out_specs = [pl.BlockSpec((BATCH, GLA_CHUNK, GLA_WIDTH), lambda c: (0, c, 0))]
    out_shape = [jax.ShapeDtypeStruct((BATCH, SEQ, GLA_WIDTH), BF16)]
    operands = [proj_c, proj_c, proj_c, proj_g, wup_pad, b_gate, summat3, norm_g]
    body = _gla_kernel
    if next_layer is not None:
        mixed_blk = GATE_COL // HOST_IN_ROWS
        last = (IN_PROJ_WIDTH - 1) // HOST_IN_ROWS
        in_specs += [
            pl.BlockSpec((1, HOST_IN_ROWS, D_MODEL), lambda c: (next_layer, c, 0)),
            pl.BlockSpec((1, HOST_IN_ROWS, D_MODEL),
                         lambda c: (next_layer, jnp.clip(c + 1, mixed_blk + 1, last), 0)),
            pl.BlockSpec((1, HOST_OUT_ROWS, D_MODEL), lambda c: (next_layer, c, 0)),
        ]
        out_specs += [
            pl.BlockSpec((1, HOST_IN_ROWS, D_MODEL), lambda c: (0, c, 0)),
            pl.BlockSpec((1, G_WIDTH, D_MODEL), lambda c: (0, 0, 0)),
            pl.BlockSpec((1, HOST_OUT_ROWS, D_MODEL), lambda c: (0, c, 0)),
        ]
        out_shape += [jax.ShapeDtypeStruct((1, MAIN_WIDTH, D_MODEL), BF16),
                      jax.ShapeDtypeStruct((1, G_WIDTH, D_MODEL), BF16),
                      jax.ShapeDtypeStruct((1, D_MODEL, D_MODEL), BF16)]
        operands += [w_in_t, w_in_t, w_out]
        body = _gla_hosting_kernel
    return pl.pallas_call(
        body,
        grid=(GLA_STEPS,),
        in_specs=in_specs,
        out_specs=out_specs,
        out_shape=out_shape,
        scratch_shapes=[pltpu.VMEM((BATCH, GLA_DV, GLA_KEY_WIDTH), F32),
                        pltpu.VMEM((BATCH * GLA_KEY_WIDTH // LANES, 2 * GLA_CHUNK, GLA_CHUNK), BF16)],
        compiler_params=pltpu.CompilerParams(
            dimension_semantics=("arbitrary",), vmem_limit_bytes=VMEM_LIMIT),
        name="gla",
    )(*operands)


OUT_TM = 512
OUT_TN = 256


def _outproj_kernel(ya_ref, yb_ref, yc_ref, sz_ref, x_ref, gate_ref, g_ref, w_ref, o_ref,
                    yg_ref, ssq_ref, inv_ref, *acc_refs):
    i = pl.program_id(0)
    n_tiles = ROWS // OUT_TM

    @pl.when(i == 0)
    def _():
        for acc_ref in acc_refs:
            acc_ref[...] = jnp.zeros_like(acc_ref)
        ssq_ref[...] = jnp.zeros_like(ssq_ref)

    inv_ref[...] = lax.rsqrt(ssq_ref[...] * (1.0 / D_MODEL) + EPS)
    ssq_ref[...] = jnp.zeros_like(ssq_ref)
    out_gain = g_ref[...] * gate_ref[0]

    def finalise(blk):
        cols = slice(blk * OUT_TN, (blk + 1) * OUT_TN)
        o_ref[:, cols] = x_ref[:, cols] + acc_refs[blk][...] * inv_ref[:, 0:1] * out_gain[:, cols]

    def matmul(blk):
        r = jnp.dot(yg_ref[...], w_ref[0, :, blk * OUT_TN:(blk + 1) * OUT_TN], preferred_element_type=F32)
        acc_refs[blk][...] = r
        ssq_ref[...] += jnp.sum(r * r, axis=-1, keepdims=True)

    @pl.when(i < n_tiles)
    def _():
        col = 0
        for y_ref in (ya_ref, yb_ref, yc_ref):
            width = y_ref.shape[1]
            yg_ref[:, col:col + width] = y_ref[...] * sz_ref[:, col:col + width]
            col += width
        for blk in range(len(acc_refs)):
            finalise(blk)
            matmul(blk)

    @pl.when(i == n_tiles)
    def _():
        for blk in range(len(acc_refs)):
            finalise(blk)


def _outproj(layer, y_a, y_b, y_c, silu_z, x2, gate, g_post, w_out_bf):
    tiles_per_batch = SEQ // OUT_TM
    n_tiles = ROWS // OUT_TM
    mixed = lambda i: jnp.minimum(i, n_tiles - 1)
    done = lambda i: jnp.maximum(i - 1, 0)
    return pl.pallas_call(
        _outproj_kernel,
        grid=(n_tiles + 1,),
        in_specs=[
            pl.BlockSpec((OUT_TM, SWA_WIDTH), lambda i: (mixed(i), 0)),
            pl.BlockSpec((OUT_TM, SG_WIDTH), lambda i: (mixed(i), 0)),
            pl.BlockSpec((OUT_TM, GLA_WIDTH), lambda i: (mixed(i), 0)),
            pl.BlockSpec((OUT_TM, Z_WIDTH), lambda i: (mixed(i), 0)),
            pl.BlockSpec((OUT_TM, D_MODEL), lambda i: (done(i), 0)),
            pl.BlockSpec((1, 1, D_MODEL), lambda i: (done(i) // tiles_per_batch, 0, 0)),
            pl.BlockSpec((1, D_MODEL), lambda i: (0, 0)),
            pl.BlockSpec((1, D_MODEL, D_MODEL), lambda i: (layer, 0, 0), pipeline_mode=pl.Buffered(1)),
        ],
        out_specs=pl.BlockSpec((OUT_TM, D_MODEL), lambda i: (done(i), 0)),
        out_shape=jax.ShapeDtypeStruct((ROWS, D_MODEL), F32),
        scratch_shapes=(
            [pltpu.VMEM((OUT_TM, D_MODEL), BF16),
             pltpu.VMEM((OUT_TM, LANES), F32),
             pltpu.VMEM((OUT_TM, LANES), F32)]
            + [pltpu.VMEM((OUT_TM, OUT_TN), F32)] * (D_MODEL // OUT_TN)),
        compiler_params=pltpu.CompilerParams(
            dimension_semantics=("arbitrary",), vmem_limit_bytes=VMEM_LIMIT),
        name="gate_outproj_residual",
    )(y_a, y_b, y_c, silu_z, x2, gate, g_post, w_out_bf)


def kernel(x, c, positions, w_mod, b_mod, g_pre, g_post, w_in, w_out, swa_sinks,
           sg_w, sg_b, sg_ln_g, sg_ln_b, gla_w_gate_up, gla_b_gate, gla_norm_g):
    assert x.shape == (BATCH, SEQ, D_MODEL) and w_in.shape[0] == DEPTH

    c_pad = jnp.pad(c, ((0, 8 - BATCH), (0, 0)))
    mod = _modulation(c_pad, w_mod, b_mod)[:, :BATCH, :]
    cos_t, sin_t = _rope_tables(positions)
    summat3 = jnp.asarray(np.tile(_gla_sum_matrix(), (1, 3)), dtype=BF16)

    w_in_t = jnp.swapaxes(w_in, 1, 2)
    w_main, w_gate = _prep_w_in(w_in_t)
    w_out_bf = _prep_w_out(w_out)

    x2 = x.reshape(ROWS, D_MODEL)
    for l in range(DEPTH):
        shift = mod[l, :, 0:D_MODEL].reshape(BATCH, 1, D_MODEL)
        scale = mod[l, :, D_MODEL:2 * D_MODEL].reshape(BATCH, 1, D_MODEL)
        gate = mod[l, :, 2 * D_MODEL:].reshape(BATCH, 1, D_MODEL)
        bias_tile = jnp.repeat(sg_b[l].T, SG_GROUP_DIM, axis=1)
        y_a, y_b, proj_c, silu_z, proj_g = _front(
            0, x2, scale, shift, g_pre[l].reshape(1, D_MODEL), w_main, w_gate, sg_w[l], bias_tile,
            sg_ln_g[l].reshape(1, SG_WIDTH), sg_ln_b[l].reshape(1, SG_WIDTH), swa_sinks[l], cos_t, sin_t)
        wup_pad = jnp.pad(gla_w_gate_up[l], ((0, G_WIDTH - GLA_GATE_RANK), (0, 0))).astype(BF16)
        gla_args = (proj_c.reshape(BATCH, SEQ, C_WIDTH), proj_g.reshape(BATCH, SEQ, G_WIDTH), wup_pad,
                    gla_b_gate[l].reshape(1, GLA_KEY_WIDTH), summat3, gla_norm_g[l].reshape(1, GLA_DV))
        this_w_out = w_out_bf
        if l + 1 < DEPTH:
            y_c, w_main, w_gate, w_out_bf = _gla(*gla_args, next_layer=l + 1, w_in_t=w_in_t, w_out=w_out)
        else:
            y_c, = _gla(*gla_args)

        x2 = _outproj(0, y_a, y_b, y_c.reshape(ROWS, GLA_WIDTH), silu_z, x2, gate,
                      g_post[l].reshape(1, D_MODEL), this_w_out)
    return x2.reshape(BATCH, SEQ, D_MODEL)
```

```python
import functools

import numpy as np
import jax
import jax.numpy as jnp
from jax import lax
from jax.experimental import pallas as pl
from jax.experimental.pallas import tpu as pltpu

F32 = jnp.float32
BF16 = jnp.bfloat16

D_MODEL = 2048
BATCH = 4
SEQ = 2048
DEPTH = 2
EPS = 1e-6
ROWS = BATCH * SEQ

SWA_HEAD_DIM = 64
SWA_HEADS = 16
SWA_KV_HEADS = 4
SWA_GROUP = SWA_HEADS // SWA_KV_HEADS
SWA_WIDTH = SWA_HEADS * SWA_HEAD_DIM
SWA_KV_WIDTH = SWA_KV_HEADS * SWA_HEAD_DIM
WINDOW = 128
ROT_DIM = 16
ROT_HALF = ROT_DIM // 2
ROPE_THETA = 500000.0

SG_WIDTH = 512
SG_GROUPS = 8
SG_GROUP_DIM = 64
SG_CHUNK = 128

GLA_HEADS = 4
GLA_WIDTH = 512
GLA_DV = 128
GLA_DK = 64
GLA_KEY_WIDTH = 256
GLA_GATE_RANK = 16
GLA_GATE_TAU = 16.0
GLA_CHUNK = 128
GLA_LEVELS = 7
GLA_SAFE_DECAY = 40.0

LANES = 128
A_WIDTH = SWA_WIDTH + 2 * SWA_KV_WIDTH
B_WIDTH = 2 * SG_WIDTH
C_WIDTH = 2 * GLA_KEY_WIDTH + GLA_WIDTH
Z_WIDTH = D_MODEL
G_WIDTH = LANES
W_IN_PAD = A_WIDTH + B_WIDTH + C_WIDTH + Z_WIDTH + G_WIDTH
MOD_WIDTH = 3 * D_MODEL

VMEM_LIMIT = 56 * 1024 * 1024
FRONT_VMEM_LIMIT = 60 * 1024 * 1024


def _sigmoid(x):
    return 1.0 / (1.0 + jnp.exp(-x))


def _gelu_tanh(x):
    return 0.5 * x * (1.0 + jnp.tanh(0.7978845608028654 * (x + 0.044715 * (x * x * x))))


def _dot_nt(a, b):
    return lax.dot_general(a, b, (((1,), (1,)), ((), ())), preferred_element_type=F32)


def _dot_tn(a, b):
    return lax.dot_general(a, b, (((0,), (0,)), ((), ())), preferred_element_type=F32)


MOD_TN = 768


def _mod_kernel(c_ref, w_ref, b_ref, o_ref):
    c = c_ref[...]
    s = (c * _sigmoid(c)).astype(BF16)
    o_ref[0] = jnp.dot(s, w_ref[0].astype(BF16), preferred_element_type=F32) + b_ref[0]


def _modulation(c_pad, w_mod, b_mod):
    return pl.pallas_call(
        _mod_kernel,
        grid=(DEPTH, MOD_WIDTH // MOD_TN),
        in_specs=[
            pl.BlockSpec((8, D_MODEL), lambda l, j: (0, 0)),
            pl.BlockSpec((1, D_MODEL, MOD_TN), lambda l, j: (l, 0, j)),
            pl.BlockSpec((1, 1, MOD_TN), lambda l, j: (l, 0, j)),
        ],
        out_specs=pl.BlockSpec((1, 8, MOD_TN), lambda l, j: (l, 0, j)),
        out_shape=jax.ShapeDtypeStruct((DEPTH, 8, MOD_WIDTH), F32),
        compiler_params=pltpu.CompilerParams(
            dimension_semantics=("arbitrary", "arbitrary"), vmem_limit_bytes=VMEM_LIMIT),
        name="adaln_mod",
    )(c_pad, w_mod, b_mod.reshape(DEPTH, 1, MOD_WIDTH))


POS_PER_ROW = LANES // ROT_DIM


def _rope_table_kernel(pos_ref, invf_ref, cos_ref, sin_ref):
    ang = pos_ref[...].astype(F32) * invf_ref[...]
    lane = lax.broadcasted_iota(jnp.int32, (1, LANES), 1) % ROT_DIM
    s = jnp.sin(ang)
    cos_ref[...] = jnp.cos(ang)
    sin_ref[...] = jnp.where(lane < ROT_HALF, -s, s)


def _rope_tables(positions):
    half = np.arange(ROT_HALF, dtype=np.float32)
    inv_freq = (np.float32(ROPE_THETA) ** (-(half * np.float32(2.0 / ROT_DIM)))).astype(np.float32)
    invf = inv_freq[(np.arange(LANES) % ROT_DIM) % ROT_HALF][None, :]
    packed_rows = ROWS // POS_PER_ROW
    pos_packed = jnp.repeat(positions.reshape(packed_rows, POS_PER_ROW), ROT_DIM, axis=1)
    cos_p, sin_p = pl.pallas_call(
        _rope_table_kernel,
        grid=(1,),
        in_specs=[
            pl.BlockSpec((packed_rows, LANES), lambda i: (0, 0)),
            pl.BlockSpec((1, LANES), lambda i: (0, 0)),
        ],
        out_specs=[pl.BlockSpec((packed_rows, LANES), lambda i: (0, 0))] * 2,
        out_shape=[jax.ShapeDtypeStruct((packed_rows, LANES), F32)] * 2,
        compiler_params=pltpu.CompilerParams(dimension_semantics=("arbitrary",)),
        name="rope_tables",
    )(pos_packed, jnp.asarray(invf))

    def spread(table, fill):
        rot = table.reshape(ROWS, ROT_DIM)
        rest = jnp.full((ROWS, SWA_HEAD_DIM - ROT_DIM), fill, F32)
        return jnp.concatenate([rot, rest] * (LANES // SWA_HEAD_DIM), axis=1)

    return spread(cos_p, 1.0), spread(sin_p, 0.0)


PREP_TN = 512
MAIN_WIDTH = A_WIDTH + B_WIDTH + C_WIDTH + Z_WIDTH
GATE_COL = A_WIDTH + B_WIDTH + C_WIDTH
IN_PROJ_WIDTH = GATE_COL + GLA_GATE_RANK + Z_WIDTH


def _prep_w_in_kernel(a_ref, b_ref, main_ref, gate_ref):
    j = pl.program_id(1)
    first_z = GATE_COL // PREP_TN

    @pl.when(j < first_z)
    def _():
        main_ref[0] = a_ref[0].astype(BF16)

    @pl.when(j >= first_z)
    def _():
        main_ref[0] = jnp.concatenate(
            [a_ref[0, GLA_GATE_RANK:, :], b_ref[0, :GLA_GATE_RANK, :]], axis=0).astype(BF16)

    @pl.when(j == first_z)
    def _():
        gate_ref[0] = jnp.concatenate(
            [a_ref[0, :GLA_GATE_RANK, :], jnp.zeros((G_WIDTH - GLA_GATE_RANK, D_MODEL), F32)],
            axis=0).astype(BF16)


def _prep_w_in(w_in_t):
    first_z = GATE_COL // PREP_TN
    last = (IN_PROJ_WIDTH - 1) // PREP_TN
    return pl.pallas_call(
        _prep_w_in_kernel,
        grid=(1, MAIN_WIDTH // PREP_TN),
        in_specs=[
            pl.BlockSpec((1, PREP_TN, D_MODEL), lambda l, j: (l, j, 0)),
            pl.BlockSpec((1, PREP_TN, D_MODEL), lambda l, j: (l, jnp.clip(j + 1, first_z + 1, last), 0)),
        ],
        out_specs=[
            pl.BlockSpec((1, PREP_TN, D_MODEL), lambda l, j: (l, j, 0)),
            pl.BlockSpec((1, G_WIDTH, D_MODEL), lambda l, j: (l, 0, 0)),
        ],
        out_shape=[jax.ShapeDtypeStruct((1, MAIN_WIDTH, D_MODEL), BF16),
                   jax.ShapeDtypeStruct((1, G_WIDTH, D_MODEL), BF16)],
        compiler_params=pltpu.CompilerParams(
            dimension_semantics=("arbitrary", "arbitrary"), vmem_limit_bytes=VMEM_LIMIT),
        name="prep_w_in",
    )(w_in_t, w_in_t)


def _cast_kernel(x_ref, o_ref):
    o_ref[...] = x_ref[...].astype(o_ref.dtype)


def _prep_w_out(w_out):
    tn = 1024
    return pl.pallas_call(
        _cast_kernel,
        grid=(1, D_MODEL // tn),
        in_specs=[pl.BlockSpec((1, D_MODEL, tn), lambda l, j: (l, 0, j))],
        out_specs=pl.BlockSpec((1, D_MODEL, tn), lambda l, j: (l, 0, j)),
        out_shape=jax.ShapeDtypeStruct((1, D_MODEL, D_MODEL), BF16),
        compiler_params=pltpu.CompilerParams(
            dimension_semantics=("arbitrary", "arbitrary"), vmem_limit_bytes=VMEM_LIMIT),
        name="prep_w_out",
    )(w_out)


INPROJ_TM = 512
INPROJ_CHUNK = 512


def _spatial_gating_chunk(u, v, w_bf, bias, ln_g, ln_b):
    u = _gelu_tanh(u)
    v = _gelu_tanh(v)
    mu = jnp.mean(v, axis=-1, keepdims=True)
    vc = v - mu
    var = jnp.mean(vc * vc, axis=-1, keepdims=True)
    vn = (vc * lax.rsqrt(var + EPS) * ln_g + ln_b).astype(BF16)
    parts = [jnp.dot(w_bf[g], vn[:, g * SG_GROUP_DIM:(g + 1) * SG_GROUP_DIM], preferred_element_type=F32)
             for g in range(SG_GROUPS)]
    return u * (jnp.concatenate(parts, axis=1) + bias)


def _swa_pieces(tile_has_prev, a_ref, cos_ref, sin_ref, sinks_ref, ya_ref,
                kband_ref, vband_ref, qr_ref, ksel_ref, vdup_ref, s_ref, p_ref):
    heads_per_tile = LANES // SWA_HEAD_DIM
    n_blocks = INPROJ_TM // WINDOW
    lane = lax.broadcasted_iota(jnp.int32, (1, LANES), 1)
    first_half = (lane % SWA_HEAD_DIM) < ROT_HALF
    low_half = lane < SWA_HEAD_DIM
    qi = lax.broadcasted_iota(jnp.int32, (WINDOW, WINDOW), 0)
    kj = lax.broadcasted_iota(jnp.int32, (WINDOW, WINDOW), 1)
    from_prev = kj > qi

    def rope(t):
        partner = jnp.where(first_half, pltpu.roll(t, LANES - ROT_HALF, 1), pltpu.roll(t, ROT_HALF, 1))
        return t * cos_ref[...] + partner * sin_ref[...]

    steps = []

    def carry():
        kband_ref[0:WINDOW, :] = kband_ref[INPROJ_TM:INPROJ_TM + WINDOW, :]
        vband_ref[0:WINDOW, :] = vband_ref[INPROJ_TM:INPROJ_TM + WINDOW, :]
    steps.append(carry)

    def stage_k(t):
        cols = slice(t * LANES, (t + 1) * LANES)
        kband_ref[WINDOW:, cols] = rope(a_ref[:, SWA_WIDTH + t * LANES:SWA_WIDTH + (t + 1) * LANES])
        vband_ref[WINDOW:, cols] = a_ref[:, SWA_WIDTH + SWA_KV_WIDTH + t * LANES:
                                         SWA_WIDTH + SWA_KV_WIDTH + (t + 1) * LANES]
    for t in range(SWA_KV_WIDTH // LANES):
        steps.append(functools.partial(stage_k, t))

    def stage_q(t):
        cols = slice(t * LANES, (t + 1) * LANES)
        qr_ref[:, cols] = rope(a_ref[:, cols] * (SWA_HEAD_DIM ** -0.5)).astype(BF16)
    for t in range(SWA_WIDTH // LANES):
        steps.append(functools.partial(stage_q, t))

    def stage_kv(blk, t):
        rows = slice(blk * WINDOW, (blk + 2) * WINDOW)
        cols = slice(t * LANES, (t + 1) * LANES)
        kt = kband_ref[rows, cols]
        vt = vband_ref[rows, cols]
        kt_sw = pltpu.roll(kt, SWA_HEAD_DIM, 1)
        vt_sw = pltpu.roll(vt, SWA_HEAD_DIM, 1)
        for u, (k_lo, k_hi, v_lo, v_hi) in enumerate(((kt, kt_sw, vt, vt_sw), (kt_sw, kt, vt_sw, vt))):
            g = t * heads_per_tile + u
            ksel_ref[2 * g] = jnp.where(low_half, k_lo, 0.0).astype(BF16)
            ksel_ref[2 * g + 1] = jnp.where(low_half, 0.0, k_hi).astype(BF16)
            vdup_ref[g] = jnp.where(low_half, v_lo, v_hi).astype(BF16)

    def scores(blk, h):
        t, u = divmod(h, heads_per_tile)
        q = qr_ref[blk * WINDOW:(blk + 1) * WINDOW, t * LANES:(t + 1) * LANES]
        s2 = _dot_nt(q, ksel_ref[2 * (h // SWA_GROUP) + u])
        s_ref[h] = jnp.where(from_prev, s2[:, :WINDOW], s2[:, WINDOW:])

    def softmax(blk, h):
        s = s_ref[h]
        if blk == 0:
            s = jnp.where(kj <= qi + tile_has_prev * WINDOW, s, -jnp.inf)
        sink = sinks_ref[h]
        m = jnp.maximum(jnp.max(s, axis=-1, keepdims=True), sink)
        p = jnp.exp(s - m)
        denom = jnp.sum(p, axis=-1, keepdims=True) + jnp.exp(sink - m)
        p = (p * (1.0 / denom)).astype(BF16)
        zero = jnp.zeros_like(p)
        p_ref[h, :, :WINDOW] = jnp.where(from_prev, p, zero)
        p_ref[h, :, WINDOW:] = jnp.where(from_prev, zero, p)

    def values(blk, t):
        vg = vdup_ref[(t * heads_per_tile) // SWA_GROUP]
        outs = [jnp.dot(p_ref[t * heads_per_tile + u], vg, preferred_element_type=F32)
                for u in range(heads_per_tile)]
        ya_ref[blk * WINDOW:(blk + 1) * WINDOW, t * LANES:(t + 1) * LANES] = (
            jnp.where(low_half, outs[0], outs[1]).astype(ya_ref.dtype))

    for blk in range(n_blocks):
        for t in range(SWA_KV_WIDTH // LANES):
            steps.append(functools.partial(stage_kv, blk, t))
        for h in range(SWA_HEADS):
            steps.append(functools.partial(scores, blk, h))
        for h in range(SWA_HEADS):
            steps.append(functools.partial(softmax, blk, h))
        for t in range(SWA_WIDTH // LANES):
            steps.append(functools.partial(values, blk, t))
    return steps


def _interleave(main_steps, side_steps):
    done = 0
    for idx, step in enumerate(main_steps):
        step()
        upto = (idx + 1) * len(side_steps) // len(main_steps)
        for side in side_steps[done:upto]:
            side()
        done = upto


def _front_kernel(sinks_ref, x_ref, scale_ref, shift_ref, g_ref, w_ref, wg_ref,
                  sgw_ref, sgb_ref, lng_ref, lnb_ref, cos_ref, sin_ref,
                  ya_ref, yb_ref, oc_ref, oz_ref, og_ref,
                  h_ref, uv_ref, a_ref, kband_ref, vband_ref, qr_ref, ksel_ref, vdup_ref, s_ref, p_ref):
    i = pl.program_id(0)

    @pl.when(i == 0)
    def _():
        kband_ref[...] = jnp.zeros_like(kband_ref)
        vband_ref[...] = jnp.zeros_like(vband_ref)

    x = x_ref[...]
    ms = jnp.mean(x * x, axis=-1, keepdims=True)
    y = x * lax.rsqrt(ms + EPS) * g_ref[...]
    h_ref[...] = (y * (1.0 + scale_ref[0]) + shift_ref[0]).astype(BF16)

    def project(o_ref, col, c0, width, post=None):
        r = _dot_nt(h_ref[...], w_ref[0, col + c0:col + c0 + width, :])
        if post is not None:
            r = post(r)
        o_ref[:, c0:c0 + width] = r.astype(o_ref.dtype)

    def projection_steps(o_ref, col, post=None):
        return [functools.partial(project, o_ref, col, c0, INPROJ_CHUNK, post)
                for c0 in range(0, o_ref.shape[1], INPROJ_CHUNK)]

    for step in projection_steps(uv_ref, A_WIDTH):
        step()

    t = lax.broadcasted_iota(jnp.int32, (SG_CHUNK, SG_CHUNK), 0)
    s = lax.broadcasted_iota(jnp.int32, (SG_CHUNK, SG_CHUNK), 1)
    w_bf = [jnp.where(t >= s, sgw_ref[g], 0.0).astype(BF16) for g in range(SG_GROUPS)]

    def mixer_b(r0):
        rows = slice(r0, r0 + SG_CHUNK)
        yb_ref[rows, :] = _spatial_gating_chunk(
            uv_ref[rows, :SG_WIDTH], uv_ref[rows, SG_WIDTH:], w_bf, sgb_ref[...], lng_ref[...],
            lnb_ref[...]).astype(yb_ref.dtype)

    _interleave(projection_steps(a_ref, 0),
                [functools.partial(mixer_b, r0) for r0 in range(0, INPROJ_TM, SG_CHUNK)])

    tile_has_prev = ((i % (SEQ // INPROJ_TM)) != 0).astype(jnp.int32)
    swa_steps = _swa_pieces(tile_has_prev, a_ref, cos_ref, sin_ref, sinks_ref, ya_ref,
                            kband_ref, vband_ref, qr_ref, ksel_ref, vdup_ref, s_ref, p_ref)

    def gate_rank():
        og_ref[...] = _dot_nt(h_ref[...], wg_ref[0])

    silu = lambda z: z * (0.5 * jnp.tanh(0.5 * z) + 0.5)
    _interleave(projection_steps(oc_ref, A_WIDTH + B_WIDTH)
                + projection_steps(oz_ref, A_WIDTH + B_WIDTH + C_WIDTH, silu) + [gate_rank],
                swa_steps)


def _front(layer, x2, scale, shift, g_pre, w_main, w_gate, sg_w, sg_bias_tile, sg_ln_g, sg_ln_b,
           sinks, cos_t, sin_t):
    tiles_per_batch = SEQ // INPROJ_TM
    widths = (SWA_WIDTH, SG_WIDTH, C_WIDTH, Z_WIDTH, G_WIDTH)
    dtypes = (BF16, BF16, BF16, BF16, F32)
    const = pl.Buffered(1)
    return pl.pallas_call(
        _front_kernel,
        grid=(ROWS // INPROJ_TM,),
        in_specs=[
            pl.BlockSpec(memory_space=pltpu.SMEM),
            pl.BlockSpec((INPROJ_TM, D_MODEL), lambda i: (i, 0)),
            pl.BlockSpec((1, 1, D_MODEL), lambda i: (i // tiles_per_batch, 0, 0)),
            pl.BlockSpec((1, 1, D_MODEL), lambda i: (i // tiles_per_batch, 0, 0)),
            pl.BlockSpec((1, D_MODEL), lambda i: (0, 0)),
            pl.BlockSpec((1, MAIN_WIDTH, D_MODEL), lambda i: (layer, 0, 0), pipeline_mode=const),
            pl.BlockSpec((1, G_WIDTH, D_MODEL), lambda i: (layer, 0, 0), pipeline_mode=const),
            pl.BlockSpec((SG_GROUPS, SG_CHUNK, SG_CHUNK), lambda i: (0, 0, 0), pipeline_mode=const),
            pl.BlockSpec((SG_CHUNK, SG_WIDTH), lambda i: (0, 0), pipeline_mode=const),
            pl.BlockSpec((1, SG_WIDTH), lambda i: (0, 0)),
            pl.BlockSpec((1, SG_WIDTH), lambda i: (0, 0)),
            pl.BlockSpec((INPROJ_TM, LANES), lambda i: (i, 0)),
            pl.BlockSpec((INPROJ_TM, LANES), lambda i: (i, 0)),
        ],
        out_specs=[pl.BlockSpec((INPROJ_TM, w), lambda i: (i, 0)) for w in widths],
        out_shape=[jax.ShapeDtypeStruct((ROWS, w), dt) for w, dt in zip(widths, dtypes)],
        scratch_shapes=[
            pltpu.VMEM((INPROJ_TM, D_MODEL), BF16),
            pltpu.VMEM((INPROJ_TM, B_WIDTH), F32),
            pltpu.VMEM((INPROJ_TM, A_WIDTH), F32),
            pltpu.VMEM((INPROJ_TM + WINDOW, SWA_KV_WIDTH), F32),
            pltpu.VMEM((INPROJ_TM + WINDOW, SWA_KV_WIDTH), F32),
            pltpu.VMEM((INPROJ_TM, SWA_WIDTH), BF16),
            pltpu.VMEM((2 * SWA_KV_HEADS, 2 * WINDOW, LANES), BF16),
            pltpu.VMEM((SWA_KV_HEADS, 2 * WINDOW, LANES), BF16),
            pltpu.VMEM((SWA_HEADS, WINDOW, WINDOW), F32),
            pltpu.VMEM((SWA_HEADS, WINDOW, 2 * WINDOW), BF16),
        ],
        compiler_params=pltpu.CompilerParams(
            dimension_semantics=("arbitrary",), vmem_limit_bytes=FRONT_VMEM_LIMIT),
        name="front",
    )(sinks, x2, scale, shift, g_pre, w_main, w_gate, sg_w, sg_bias_tile, sg_ln_g, sg_ln_b, cos_t, sin_t)


def _gla_sum_matrix():
    c = GLA_CHUNK
    mat = np.zeros(((2 + GLA_LEVELS) * c, c), np.float32)
    for t in range(c):
        mat[t, :t + 1] = 1.0
        mat[c + t, t + 1:] = 1.0
        for k in range(GLA_LEVELS):
            m = 1 << k
            r = (t >> (k + 1) << (k + 1)) + m
            row = (2 + k) * c + t
            if (t >> k) & 1:
                mat[row, r + 1:t + 1] = 1.0
            else:
                mat[row, t + 1:r + 1] = 1.0
    return mat


def _split_heads_on_rows(x, low_half):
    zero = jnp.zeros_like(x)
    return jnp.concatenate([jnp.where(low_half, x, zero), jnp.where(low_half, zero, x)], axis=0)


def _gla_kernel(q_ref, k_ref, v_ref, cg_ref, wup_ref, bg_ref, summat_ref, ng_ref, o_ref, st_ref, sc_ref):
    c = GLA_CHUNK
    heads_per_tile = LANES // GLA_DK
    n_tiles = BATCH * GLA_KEY_WIDTH // LANES

    @pl.when(pl.program_id(0) == 0)
    def _():
        st_ref[...] = jnp.zeros_like(st_ref)

    cg = cg_ref[...].reshape(BATCH * c, G_WIDTH).astype(BF16)
    logits = jnp.dot(cg, wup_ref[...], preferred_element_type=F32) + bg_ref[...]
    log_alpha = (jnp.minimum(logits, 0.0) - jnp.log1p(jnp.exp(-jnp.abs(logits)))) * (1.0 / GLA_GATE_TAU)

    hi = log_alpha.astype(BF16)
    r1 = log_alpha - hi.astype(F32)
    mid = r1.astype(BF16)
    lo = (r1 - mid.astype(F32)).astype(BF16)
    pieces = jnp.concatenate(
        [jnp.concatenate([p[b * c:(b + 1) * c] for p in (hi, mid, lo)], axis=0) for b in range(BATCH)],
        axis=1)
    expo = jnp.dot(summat_ref[0:2 * c, :], pieces, preferred_element_type=F32)

    q = jnp.concatenate([q_ref[b] for b in range(BATCH)], axis=1).astype(F32) * (GLA_DK ** -0.5)
    k = jnp.concatenate([k_ref[b] for b in range(BATCH)], axis=1).astype(F32)
    b_cum = expo[0:c]
    q_inter = (q * jnp.exp(b_cum)).astype(BF16)
    k_state = (k * jnp.exp(expo[c:2 * c])).astype(BF16)
    b_last = b_cum[c - 1:c, :]
    decay = jnp.exp(b_last)

    t_i = lax.broadcasted_iota(jnp.int32, (heads_per_tile * c, c), 0) % c
    s_i = lax.broadcasted_iota(jnp.int32, (heads_per_tile * c, c), 1)
    low_half = lax.broadcasted_iota(jnp.int32, (1, LANES), 1) < GLA_DK

    safe = jnp.min(b_last) >= -GLA_SAFE_DECAY

    @pl.when(safe)
    def _():
        k_grown = (k * jnp.exp(-b_cum)).astype(BF16)
        for j in range(n_tiles):
            sl = slice(j * LANES, (j + 1) * LANES)
            a2 = _dot_nt(_split_heads_on_rows(q_inter[:, sl], low_half), k_grown[:, sl])
            sc_ref[j] = jnp.where(s_i <= t_i, a2, 0.0).astype(BF16)

    @pl.when(jnp.logical_not(safe))
    def _():
        lv_expo = jnp.dot(summat_ref[2 * c:, :], pieces, preferred_element_type=F32)
        q_lv, k_lv, masks = [q.astype(BF16)], [k.astype(BF16)], [t_i == s_i]
        for lv in range(GLA_LEVELS):
            f = jnp.exp(lv_expo[lv * c:(lv + 1) * c])
            q_lv.append((q * f).astype(BF16))
            k_lv.append((k * f).astype(BF16))
            masks.append(((t_i >> (lv + 1)) == (s_i >> (lv + 1)))
                         & (((t_i >> lv) & 1) == 1) & (((s_i >> lv) & 1) == 0))
        for j in range(n_tiles):
            sl = slice(j * LANES, (j + 1) * LANES)
            a2 = jnp.zeros((heads_per_tile * c, c), F32)
            for ql, kl, mask in zip(q_lv, k_lv, masks):
                a2 = a2 + jnp.where(mask, _dot_nt(_split_heads_on_rows(ql[:, sl], low_half), kl[:, sl]), 0.0)
            sc_ref[j] = a2.astype(BF16)

    scores = [sc_ref[j] for j in range(n_tiles)]
    tiles_per_seq = GLA_KEY_WIDTH // LANES
    for j in range(n_tiles):
        b, pair = divmod(j, tiles_per_seq)
        sl = slice(j * LANES, (j + 1) * LANES)
        st_cols = slice(pair * LANES, (pair + 1) * LANES)
        st = st_ref[b, :, st_cols]
        inter = _dot_nt(_split_heads_on_rows(q_inter[:, sl], low_half), st.astype(BF16))
        updates = []
        for u in range(heads_per_tile):
            h = pair * heads_per_tile + u
            vs = slice(h * GLA_DV, (h + 1) * GLA_DV)
            vh = v_ref[b, :, vs]
            o = jnp.dot(scores[j][u * c:(u + 1) * c], vh, preferred_element_type=F32) + inter[u * c:(u + 1) * c]
            y = o * lax.rsqrt(jnp.mean(o * o, axis=-1, keepdims=True) + EPS) * ng_ref[...]
            o_ref[b, :, vs] = y.astype(o_ref.dtype)
            updates.append(_dot_tn(vh, k_state[:, sl]))
        st_ref[b, :, st_cols] = st * decay[:, sl] + jnp.where(low_half, updates[0], updates[1])


GLA_STEPS = SEQ // GLA_CHUNK
HOST_IN_ROWS = MAIN_WIDTH // GLA_STEPS
HOST_OUT_ROWS = D_MODEL // GLA_STEPS


def _gla_hosting_kernel(q_ref, k_ref, v_ref, cg_ref, wup_ref, bg_ref, summat_ref, ng_ref,
                        wa_ref, wb_ref, wo_ref, o_ref, main_ref, gate_ref, wout_ref, st_ref, sc_ref):
    _gla_kernel(q_ref, k_ref, v_ref, cg_ref, wup_ref, bg_ref, summat_ref, ng_ref, o_ref, st_ref, sc_ref)

    c = pl.program_id(0)
    mixed_blk, split = divmod(GATE_COL, HOST_IN_ROWS)
    tail = lambda lo: jnp.concatenate([wa_ref[0, lo + GLA_GATE_RANK:, :], wb_ref[0, :GLA_GATE_RANK, :]], axis=0)

    @pl.when(c < mixed_blk)
    def _():
        main_ref[0] = wa_ref[0].astype(BF16)

    @pl.when(c == mixed_blk)
    def _():
        main_ref[0] = jnp.concatenate([wa_ref[0, :split, :], tail(split)], axis=0).astype(BF16)
        gate_ref[0] = jnp.concatenate(
            [wa_ref[0, split:split + GLA_GATE_RANK, :], jnp.zeros((G_WIDTH - GLA_GATE_RANK, D_MODEL), F32)],
            axis=0).astype(BF16)

    @pl.when(c > mixed_blk)
    def _():
        main_ref[0] = tail(0).astype(BF16)

    wout_ref[0] = wo_ref[0].astype(BF16)


def _gla(proj_c, proj_g, wup_pad, b_gate, summat3, norm_g, next_layer=None, w_in_t=None, w_out=None):
    in_specs = [
        pl.BlockSpec((BATCH, GLA_CHUNK, GLA_KEY_WIDTH), lambda c: (0, c, 0)),
        pl.BlockSpec((BATCH, GLA_CHUNK, GLA_KEY_WIDTH), lambda c: (0, c, 1)),
        pl.BlockSpec((BATCH, GLA_CHUNK, GLA_WIDTH), lambda c: (0, c, 1)),
        pl.BlockSpec((BATCH, GLA_CHUNK, G_WIDTH), lambda c: (0, c, 0)),
        pl.BlockSpec((G_WIDTH, GLA_KEY_WIDTH), lambda c: (0, 0)),
        pl.BlockSpec((1, GLA_KEY_WIDTH), lambda c: (0, 0)),
        pl.BlockSpec(((2 + GLA_LEVELS) * GLA_CHUNK, 3 * GLA_CHUNK), lambda c: (0, 0)),
        pl.BlockSpec((1, GLA_DV), lambda c: (0, 0)),
    ]
    out_specs = [pl.BlockSpec((BATCH, GLA_CHUNK, GLA_WIDTH), lambda c: (0, c, 0))]
    out_shape = [jax.ShapeDtypeStruct((BATCH, SEQ, GLA_WIDTH), BF16)]
    operands = [proj_c, proj_c, proj_c, proj_g, wup_pad, b_gate, summat3, norm_g]
    body = _gla_kernel
    if next_layer is not None:
        mixed_blk = GATE_COL // HOST_IN_ROWS
        last = (IN_PROJ_WIDTH - 1) // HOST_IN_ROWS
        in_specs += [
            pl.BlockSpec((1, HOST_IN_ROWS, D_MODEL), lambda c: (next_layer, c, 0)),
            pl.BlockSpec((1, HOST_IN_ROWS, D_MODEL),
                         lambda c: (next_layer, jnp.clip(c + 1, mixed_blk + 1, last), 0)),
            pl.BlockSpec((1, HOST_OUT_ROWS, D_MODEL), lambda c: (next_layer, c, 0)),
        ]
        out_specs += [
            pl.BlockSpec((1, HOST_IN_ROWS, D_MODEL), lambda c: (0, c, 0)),
            pl.BlockSpec((1, G_WIDTH, D_MODEL), lambda c: (0, 0, 0)),
            pl.BlockSpec((1, HOST_OUT_ROWS, D_MODEL), lambda c: (0, c, 0)),
        ]
        out_shape += [jax.ShapeDtypeStruct((1, MAIN_WIDTH, D_MODEL), BF16),
                      jax.ShapeDtypeStruct((1, G_WIDTH, D_MODEL), BF16),
                      jax.ShapeDtypeStruct((1, D_MODEL, D_MODEL), BF16)]
        operands += [w_in_t, w_in_t, w_out]
        body = _gla_hosting_kernel
    return pl.pallas_call(
        body,
        grid=(GLA_STEPS,),
        in_specs=in_specs,
        out_specs=out_specs,
        out_shape=out_shape,
        scratch_shapes=[pltpu.VMEM((BATCH, GLA_DV, GLA_KEY_WIDTH), F32),
                        pltpu.VMEM((BATCH * GLA_KEY_WIDTH // LANES, 2 * GLA_CHUNK, GLA_CHUNK), BF16)],
        compiler_params=pltpu.CompilerParams(
            dimension_semantics=("arbitrary",), vmem_limit_bytes=VMEM_LIMIT),
        name="gla",
    )(*operands)


OUT_TM = 512
OUT_TN = 256


def _outproj_kernel(ya_ref, yb_ref, yc_ref, sz_ref, x_ref, gate_ref, g_ref, w_ref, o_ref,
                    yg_ref, ssq_ref, inv_ref, *acc_refs):
    i = pl.program_id(0)
    n_tiles = ROWS // OUT_TM

    @pl.when(i == 0)
    def _():
        for acc_ref in acc_refs:
            acc_ref[...] = jnp.zeros_like(acc_ref)
        ssq_ref[...] = jnp.zeros_like(ssq_ref)

    inv_ref[...] = lax.rsqrt(ssq_ref[...] * (1.0 / D_MODEL) + EPS)
    ssq_ref[...] = jnp.zeros_like(ssq_ref)
    out_gain = g_ref[...] * gate_ref[0]

    def finalise(blk):
        cols = slice(blk * OUT_TN, (blk + 1) * OUT_TN)
        o_ref[:, cols] = x_ref[:, cols] + acc_refs[blk][...] * inv_ref[:, 0:1] * out_gain[:, cols]

    def matmul(blk):
        r = jnp.dot(yg_ref[...], w_ref[0, :, blk * OUT_TN:(blk + 1) * OUT_TN], preferred_element_type=F32)
        acc_refs[blk][...] = r
        ssq_ref[...] += jnp.sum(r * r, axis=-1, keepdims=True)

    @pl.when(i < n_tiles)
    def _():
        col = 0
        for y_ref in (ya_ref, yb_ref, yc_ref):
            width = y_ref.shape[1]
            yg_ref[:, col:col + width] = y_ref[...] * sz_ref[:, col:col + width]
            col += width
        for blk in range(len(acc_refs)):
            finalise(blk)
            matmul(blk)

    @pl.when(i == n_tiles)
    def _():
        for blk in range(len(acc_refs)):
            finalise(blk)


def _outproj(layer, y_a, y_b, y_c, silu_z, x2, gate, g_post, w_out_bf):
    tiles_per_batch = SEQ // OUT_TM
    n_tiles = ROWS // OUT_TM
    mixed = lambda i: jnp.minimum(i, n_tiles - 1)
    done = lambda i: jnp.maximum(i - 1, 0)
    return pl.pallas_call(
        _outproj_kernel,
        grid=(n_tiles + 1,),
        in_specs=[
            pl.BlockSpec((OUT_TM, SWA_WIDTH), lambda i: (mixed(i), 0)),
            pl.BlockSpec((OUT_TM, SG_WIDTH), lambda i: (mixed(i), 0)),
            pl.BlockSpec((OUT_TM, GLA_WIDTH), lambda i: (mixed(i), 0)),
            pl.BlockSpec((OUT_TM, Z_WIDTH), lambda i: (mixed(i), 0)),
            pl.BlockSpec((OUT_TM, D_MODEL), lambda i: (done(i), 0)),
            pl.BlockSpec((1, 1, D_MODEL), lambda i: (done(i) // tiles_per_batch, 0, 0)),
            pl.BlockSpec((1, D_MODEL), lambda i: (0, 0)),
            pl.BlockSpec((1, D_MODEL, D_MODEL), lambda i: (layer, 0, 0), pipeline_mode=pl.Buffered(1)),
        ],
        out_specs=pl.BlockSpec((OUT_TM, D_MODEL), lambda i: (done(i), 0)),
        out_shape=jax.ShapeDtypeStruct((ROWS, D_MODEL), F32),
        scratch_shapes=(
            [pltpu.VMEM((OUT_TM, D_MODEL), BF16),
             pltpu.VMEM((OUT_TM, LANES), F32),
             pltpu.VMEM((OUT_TM, LANES), F32)]
            + [pltpu.VMEM((OUT_TM, OUT_TN), F32)] * (D_MODEL // OUT_TN)),
        compiler_params=pltpu.CompilerParams(
            dimension_semantics=("arbitrary",), vmem_limit_bytes=VMEM_LIMIT),
        name="gate_outproj_residual",
    )(y_a, y_b, y_c, silu_z, x2, gate, g_post, w_out_bf)


def kernel(x, c, positions, w_mod, b_mod, g_pre, g_post, w_in, w_out, swa_sinks,
           sg_w, sg_b, sg_ln_g, sg_ln_b, gla_w_gate_up, gla_b_gate, gla_norm_g):
    assert x.shape == (BATCH, SEQ, D_MODEL) and w_in.shape[0] == DEPTH

    c_pad = jnp.pad(c, ((0, 8 - BATCH), (0, 0)))
    mod = _modulation(c_pad, w_mod, b_mod)[:, :BATCH, :]
    cos_t, sin_t = _rope_tables(positions)
    summat3 = jnp.asarray(np.tile(_gla_sum_matrix(), (1, 3)), dtype=BF16)

    w_in_t = jnp.swapaxes(w_in, 1, 2)
    w_main, w_gate = _prep_w_in(w_in_t)
    w_out_bf = _prep_w_out(w_out)

    x2 = x.reshape(ROWS, D_MODEL)
    for l in range(DEPTH):
        shift = mod[l, :, 0:D_MODEL].reshape(BATCH, 1, D_MODEL)
        scale = mod[l, :, D_MODEL:2 * D_MODEL].reshape(BATCH, 1, D_MODEL)
        gate = mod[l, :, 2 * D_MODEL:].reshape(BATCH, 1, D_MODEL)
        bias_tile = jnp.repeat(sg_b[l].T, SG_GROUP_DIM, axis=1)
        y_a, y_b, proj_c, silu_z, proj_g = _front(
            0, x2, scale, shift, g_pre[l].reshape(1, D_MODEL), w_main, w_gate, sg_w[l], bias_tile,
            sg_ln_g[l].reshape(1, SG_WIDTH), sg_ln_b[l].reshape(1, SG_WIDTH), swa_sinks[l], cos_t, sin_t)
        wup_pad = jnp.pad(gla_w_gate_up[l], ((0, G_WIDTH - GLA_GATE_RANK), (0, 0))).astype(BF16)
        gla_args = (proj_c.reshape(BATCH, SEQ, C_WIDTH), proj_g.reshape(BATCH, SEQ, G_WIDTH), wup_pad,
                    gla_b_gate[l].reshape(1, GLA_KEY_WIDTH), summat3, gla_norm_g[l].reshape(1, GLA_DV))
        this_w_out = w_out_bf
        if l + 1 < DEPTH:
            y_c, w_main, w_gate, w_out_bf = _gla(*gla_args, next_layer=l + 1, w_in_t=w_in_t, w_out=w_out)
        else:
            y_c, = _gla(*gla_args)

        x2 = _outproj(0, y_a, y_b, y_c.reshape(ROWS, GLA_WIDTH), silu_z, x2, gate,
                      g_post[l].reshape(1, D_MODEL), this_w_out)
    return x2.reshape(BATCH, SEQ, D_MODEL)
```

```python
import functools

import numpy as np
import jax
import jax.numpy as jnp
from jax import lax
from jax.experimental import pallas as pl
from jax.experimental.pallas import tpu as pltpu

F32 = jnp.float32
BF16 = jnp.bfloat16

D_MODEL = 2048
BATCH = 4
SEQ = 2048
DEPTH = 2
EPS = 1e-6
ROWS = BATCH * SEQ

SWA_HEAD_DIM = 64
SWA_HEADS = 16
SWA_KV_HEADS = 4
SWA_GROUP = SWA_HEADS // SWA_KV_HEADS
SWA_WIDTH = SWA_HEADS * SWA_HEAD_DIM
SWA_KV_WIDTH = SWA_KV_HEADS * SWA_HEAD_DIM
WINDOW = 128
ROT_DIM = 16
ROT_HALF = ROT_DIM // 2
ROPE_THETA = 500000.0

SG_WIDTH = 512
SG_GROUPS = 8
SG_GROUP_DIM = 64
SG_CHUNK = 128

GLA_HEADS = 4
GLA_WIDTH = 512
GLA_DV = 128
GLA_DK = 64
GLA_KEY_WIDTH = 256
GLA_GATE_RANK = 16
GLA_GATE_TAU = 16.0
GLA_CHUNK = 128
GLA_LEVELS = 7
GLA_SAFE_DECAY = 40.0

LANES = 128
A_WIDTH = SWA_WIDTH + 2 * SWA_KV_WIDTH
B_WIDTH = 2 * SG_WIDTH
C_WIDTH = 2 * GLA_KEY_WIDTH + GLA_WIDTH
Z_WIDTH = D_MODEL
G_WIDTH = LANES
W_IN_PAD = A_WIDTH + B_WIDTH + C_WIDTH + Z_WIDTH + G_WIDTH
MOD_WIDTH = 3 * D_MODEL

VMEM_LIMIT = 56 * 1024 * 1024
FRONT_VMEM_LIMIT = 60 * 1024 * 1024


def _sigmoid(x):
    return 1.0 / (1.0 + jnp.exp(-x))


def _gelu_tanh(x):
    return 0.5 * x * (1.0 + jnp.tanh(0.7978845608028654 * (x + 0.044715 * (x * x * x))))


def _dot_nt(a, b):
    return lax.dot_general(a, b, (((1,), (1,)), ((), ())), preferred_element_type=F32)


def _dot_tn(a, b):
    return lax.dot_general(a, b, (((0,), (0,)), ((), ())), preferred_element_type=F32)


MOD_TN = 768


def _mod_kernel(c_ref, w_ref, b_ref, o_ref):
    c = c_ref[...]
    s = (c * _sigmoid(c)).astype(BF16)
    o_ref[0] = jnp.dot(s, w_ref[0].astype(BF16), preferred_element_type=F32) + b_ref[0]


def _modulation(c_pad, w_mod, b_mod):
    return pl.pallas_call(
        _mod_kernel,
        grid=(DEPTH, MOD_WIDTH // MOD_TN),
        in_specs=[
            pl.BlockSpec((8, D_MODEL), lambda l, j: (0, 0)),
            pl.BlockSpec((1, D_MODEL, MOD_TN), lambda l, j: (l, 0, j)),
            pl.BlockSpec((1, 1, MOD_TN), lambda l, j: (l, 0, j)),
        ],
        out_specs=pl.BlockSpec((1, 8, MOD_TN), lambda l, j: (l, 0, j)),
        out_shape=jax.ShapeDtypeStruct((DEPTH, 8, MOD_WIDTH), F32),
        compiler_params=pltpu.CompilerParams(
            dimension_semantics=("arbitrary", "arbitrary"), vmem_limit_bytes=VMEM_LIMIT),
        name="adaln_mod",
    )(c_pad, w_mod, b_mod.reshape(DEPTH, 1, MOD_WIDTH))


ROPE_TM = 1024


def _rope_table_kernel(pos_ref, invf_ref, cos_ref, sin_ref):
    ang = pos_ref[...].astype(F32) * invf_ref[...]
    lane = lax.broadcasted_iota(jnp.int32, (1, LANES), 1) % SWA_HEAD_DIM
    s = jnp.sin(ang)
    cos_ref[...] = jnp.cos(ang)
    sin_ref[...] = jnp.where(lane < ROT_HALF, -s, s)


def _rope_tables(positions):
    half = np.arange(ROT_HALF, dtype=np.float32)
    inv_freq = (np.float32(ROPE_THETA) ** (-(half * np.float32(2.0 / ROT_DIM)))).astype(np.float32)
    lane = np.arange(LANES) % SWA_HEAD_DIM
    invf = np.where(lane < ROT_DIM, inv_freq[lane % ROT_HALF], 0.0).astype(np.float32)[None, :]
    return pl.pallas_call(
        _rope_table_kernel,
        grid=(ROWS // ROPE_TM,),
        in_specs=[
            pl.BlockSpec((ROPE_TM, 1), lambda i: (i, 0)),
            pl.BlockSpec((1, LANES), lambda i: (0, 0)),
        ],
        out_specs=[pl.BlockSpec((ROPE_TM, LANES), lambda i: (i, 0))] * 2,
        out_shape=[jax.ShapeDtypeStruct((ROWS, LANES), F32)] * 2,
        compiler_params=pltpu.CompilerParams(dimension_semantics=("arbitrary",)),
        name="rope_tables",
    )(positions.reshape(ROWS, 1), jnp.asarray(invf))


PREP_TN = 512
MAIN_WIDTH = A_WIDTH + B_WIDTH + C_WIDTH + Z_WIDTH
GATE_COL = A_WIDTH + B_WIDTH + C_WIDTH
IN_PROJ_WIDTH = GATE_COL + GLA_GATE_RANK + Z_WIDTH


def _prep_w_in_kernel(a_ref, main_ref, gate_ref):
    j = pl.program_id(1)
    first_z = GATE_COL // PREP_TN

    @pl.when(j < first_z)
    def _():
        main_ref[0] = a_ref[0, :PREP_TN, :].astype(BF16)

    @pl.when(j >= first_z)
    def _():
        main_ref[0] = a_ref[0, GLA_GATE_RANK:, :].astype(BF16)

    @pl.when(j == first_z)
    def _():
        gate_ref[0] = jnp.concatenate(
            [a_ref[0, :GLA_GATE_RANK, :], jnp.zeros((G_WIDTH - GLA_GATE_RANK, D_MODEL), F32)],
            axis=0).astype(BF16)


def _prep_w_in(w_in_t):
    assert GATE_COL % PREP_TN == 0 and MAIN_WIDTH + GLA_GATE_RANK == IN_PROJ_WIDTH
    return pl.pallas_call(
        _prep_w_in_kernel,
        grid=(1, MAIN_WIDTH // PREP_TN),
        in_specs=[
            pl.BlockSpec((pl.Element(1), pl.Element(PREP_TN + GLA_GATE_RANK), pl.Element(D_MODEL)),
                         lambda l, j: (l, j * PREP_TN, 0)),
        ],
        out_specs=[
            pl.BlockSpec((1, PREP_TN, D_MODEL), lambda l, j: (l, j, 0)),
            pl.BlockSpec((1, G_WIDTH, D_MODEL), lambda l, j: (l, 0, 0)),
        ],
        out_shape=[jax.ShapeDtypeStruct((1, MAIN_WIDTH, D_MODEL), BF16),
                   jax.ShapeDtypeStruct((1, G_WIDTH, D_MODEL), BF16)],
        compiler_params=pltpu.CompilerParams(
            dimension_semantics=("arbitrary", "arbitrary"), vmem_limit_bytes=VMEM_LIMIT),
        name="prep_w_in",
    )(w_in_t)


def _cast_kernel(x_ref, o_ref):
    o_ref[...] = x_ref[...].astype(o_ref.dtype)


def _prep_w_out(w_out):
    tn = 1024
    return pl.pallas_call(
        _cast_kernel,
        grid=(1, D_MODEL // tn),
        in_specs=[pl.BlockSpec((1, D_MODEL, tn), lambda l, j: (l, 0, j))],
        out_specs=pl.BlockSpec((1, D_MODEL, tn), lambda l, j: (l, 0, j)),
        out_shape=jax.ShapeDtypeStruct((1, D_MODEL, D_MODEL), BF16),
        compiler_params=pltpu.CompilerParams(
            dimension_semantics=("arbitrary", "arbitrary"), vmem_limit_bytes=VMEM_LIMIT),
        name="prep_w_out",
    )(w_out)


INPROJ_TM = 512
INPROJ_CHUNK = 512


def _spatial_gating_chunk(u, v, w_bf, bias, ln_g, ln_b):
    u = _gelu_tanh(u)
    v = _gelu_tanh(v)
    mu = jnp.mean(v, axis=-1, keepdims=True)
    vc = v - mu
    var = jnp.mean(vc * vc, axis=-1, keepdims=True)
    vn = (vc * lax.rsqrt(var + EPS) * ln_g + ln_b).astype(BF16)
    parts = [jnp.dot(w_bf[g], vn[:, g * SG_GROUP_DIM:(g + 1) * SG_GROUP_DIM], preferred_element_type=F32)
             for g in range(SG_GROUPS)]
    return u * (jnp.concatenate(parts, axis=1) + bias)


def _swa_pieces(tile_has_prev, a_ref, cos_ref, sin_ref, sinks_ref, ya_ref,
                kband_ref, vband_ref, qr_ref, ksel_ref, vdup_ref, s_ref, p_ref):
    heads_per_tile = LANES // SWA_HEAD_DIM
    n_blocks = INPROJ_TM // WINDOW
    lane = lax.broadcasted_iota(jnp.int32, (1, LANES), 1)
    first_half = (lane % SWA_HEAD_DIM) < ROT_HALF
    low_half = lane < SWA_HEAD_DIM
    qi = lax.broadcasted_iota(jnp.int32, (WINDOW, WINDOW), 0)
    kj = lax.broadcasted_iota(jnp.int32, (WINDOW, WINDOW), 1)
    from_prev = kj > qi

    def rope(t):
        partner = jnp.where(first_half, pltpu.roll(t, LANES - ROT_HALF, 1), pltpu.roll(t, ROT_HALF, 1))
        return t * cos_ref[...] + partner * sin_ref[...]

    steps = []

    def carry():
        kband_ref[0:WINDOW, :] = kband_ref[INPROJ_TM:INPROJ_TM + WINDOW, :]
        vband_ref[0:WINDOW, :] = vband_ref[INPROJ_TM:INPROJ_TM + WINDOW, :]
    steps.append(carry)

    def stage_k(t):
        cols = slice(t * LANES, (t + 1) * LANES)
        kband_ref[WINDOW:, cols] = rope(a_ref[:, SWA_WIDTH + t * LANES:SWA_WIDTH + (t + 1) * LANES])
        vband_ref[WINDOW:, cols] = a_ref[:, SWA_WIDTH + SWA_KV_WIDTH + t * LANES:
                                         SWA_WIDTH + SWA_KV_WIDTH + (t + 1) * LANES]
    for t in range(SWA_KV_WIDTH // LANES):
        steps.append(functools.partial(stage_k, t))

    def stage_q(t):
        cols = slice(t * LANES, (t + 1) * LANES)
        qr_ref[:, cols] = rope(a_ref[:, cols] * (SWA_HEAD_DIM ** -0.5)).astype(BF16)
    for t in range(SWA_WIDTH // LANES):
        steps.append(functools.partial(stage_q, t))

    def stage_kv(blk, t):
        rows = slice(blk * WINDOW, (blk + 2) * WINDOW)
        cols = slice(t * LANES, (t + 1) * LANES)
        kt = kband_ref[rows, cols]
        vt = vband_ref[rows, cols]
        kt_sw = pltpu.roll(kt, SWA_HEAD_DIM, 1)
        vt_sw = pltpu.roll(vt, SWA_HEAD_DIM, 1)
        for u, (k_lo, k_hi, v_lo, v_hi) in enumerate(((kt, kt_sw, vt, vt_sw), (kt_sw, kt, vt_sw, vt))):
            g = t * heads_per_tile + u
            ksel_ref[2 * g] = jnp.where(low_half, k_lo, 0.0).astype(BF16)
            ksel_ref[2 * g + 1] = jnp.where(low_half, 0.0, k_hi).astype(BF16)
            vdup_ref[g] = jnp.where(low_half, v_lo, v_hi).astype(BF16)

    def scores(blk, h):
        t, u = divmod(h, heads_per_tile)
        q = qr_ref[blk * WINDOW:(blk + 1) * WINDOW, t * LANES:(t + 1) * LANES]
        s2 = _dot_nt(q, ksel_ref[2 * (h // SWA_GROUP) + u])
        s_ref[h] = jnp.where(from_prev, s2[:, :WINDOW], s2[:, WINDOW:])

    def softmax(blk, h):
        s = s_ref[h]
        if blk == 0:
            s = jnp.where(kj <= qi + tile_has_prev * WINDOW, s, -jnp.inf)
        sink = sinks_ref[h]
        m = jnp.maximum(jnp.max(s, axis=-1, keepdims=True), sink)
        p = jnp.exp(s - m)
        denom = jnp.sum(p, axis=-1, keepdims=True) + jnp.exp(sink - m)
        p = (p * (1.0 / denom)).astype(BF16)
        zero = jnp.zeros_like(p)
        p_ref[h, :, :WINDOW] = jnp.where(from_prev, p, zero)
        p_ref[h, :, WINDOW:] = jnp.where(from_prev, zero, p)

    def values(blk, t):
        vg = vdup_ref[(t * heads_per_tile) // SWA_GROUP]
        outs = [jnp.dot(p_ref[t * heads_per_tile + u], vg, preferred_element_type=F32)
                for u in range(heads_per_tile)]
        ya_ref[blk * WINDOW:(blk + 1) * WINDOW, t * LANES:(t + 1) * LANES] = (
            jnp.where(low_half, outs[0], outs[1]).astype(ya_ref.dtype))

    for blk in range(n_blocks):
        for t in range(SWA_KV_WIDTH // LANES):
            steps.append(functools.partial(stage_kv, blk, t))
        for h in range(SWA_HEADS):
            steps.append(functools.partial(scores, blk, h))
        for h in range(SWA_HEADS):
            steps.append(functools.partial(softmax, blk, h))
        for t in range(SWA_WIDTH // LANES):
            steps.append(functools.partial(values, blk, t))
    return steps


def _interleave(main_steps, side_steps):
    done = 0
    for idx, step in enumerate(main_steps):
        step()
        upto = (idx + 1) * len(side_steps) // len(main_steps)
        for side in side_steps[done:upto]:
            side()
        done = upto


def _front_kernel(sinks_ref, x_ref, scale_ref, shift_ref, g_ref, w_ref, wg_ref,
                  sgw_ref, sgb_ref, lng_ref, lnb_ref, cos_ref, sin_ref,
                  ya_ref, yb_ref, oc_ref, oz_ref, og_ref,
                  h_ref, uv_ref, a_ref, kband_ref, vband_ref, qr_ref, ksel_ref, vdup_ref, s_ref, p_ref):
    i = pl.program_id(0)

    @pl.when(i == 0)
    def _():
        kband_ref[...] = jnp.zeros_like(kband_ref)
        vband_ref[...] = jnp.zeros_like(vband_ref)

    x = x_ref[...]
    ms = jnp.mean(x * x, axis=-1, keepdims=True)
    y = x * lax.rsqrt(ms + EPS) * g_ref[...]
    h_ref[...] = (y * (1.0 + scale_ref[0]) + shift_ref[0]).astype(BF16)

    def project(o_ref, col, c0, width, post=None):
        r = _dot_nt(h_ref[...], w_ref[0, col + c0:col + c0 + width, :])
        if post is not None:
            r = post(r)
        o_ref[:, c0:c0 + width] = r.astype(o_ref.dtype)

    def projection_steps(o_ref, col, post=None):
        return [functools.partial(project, o_ref, col, c0, INPROJ_CHUNK, post)
                for c0 in range(0, o_ref.shape[1], INPROJ_CHUNK)]

    for step in projection_steps(uv_ref, A_WIDTH):
        step()

    t = lax.broadcasted_iota(jnp.int32, (SG_CHUNK, SG_CHUNK), 0)
    s = lax.broadcasted_iota(jnp.int32, (SG_CHUNK, SG_CHUNK), 1)
    w_bf = [jnp.where(t >= s, sgw_ref[g], 0.0).astype(BF16) for g in range(SG_GROUPS)]

    def mixer_b(r0):
        rows = slice(r0, r0 + SG_CHUNK)
        yb_ref[rows, :] = _spatial_gating_chunk(
            uv_ref[rows, :SG_WIDTH], uv_ref[rows, SG_WIDTH:], w_bf, sgb_ref[...], lng_ref[...],
            lnb_ref[...]).astype(yb_ref.dtype)

    _interleave(projection_steps(a_ref, 0),
                [functools.partial(mixer_b, r0) for r0 in range(0, INPROJ_TM, SG_CHUNK)])

    tile_has_prev = ((i % (SEQ // INPROJ_TM)) != 0).astype(jnp.int32)
    swa_steps = _swa_pieces(tile_has_prev, a_ref, cos_ref, sin_ref, sinks_ref, ya_ref,
                            kband_ref, vband_ref, qr_ref, ksel_ref, vdup_ref, s_ref, p_ref)

    def gate_rank():
        og_ref[...] = _dot_nt(h_ref[...], wg_ref[0])

    silu = lambda z: z * (0.5 * jnp.tanh(0.5 * z) + 0.5)
    _interleave(projection_steps(oc_ref, A_WIDTH + B_WIDTH)
                + projection_steps(oz_ref, A_WIDTH + B_WIDTH + C_WIDTH, silu) + [gate_rank],
                swa_steps)


def _front(layer, x2, scale, shift, g_pre, w_main, w_gate, sg_w, sg_bias_tile, sg_ln_g, sg_ln_b,
           sinks, cos_t, sin_t):
    tiles_per_batch = SEQ // INPROJ_TM
    widths = (SWA_WIDTH, SG_WIDTH, C_WIDTH, Z_WIDTH, G_WIDTH)
    dtypes = (BF16, BF16, BF16, BF16, F32)
    const = pl.Buffered(1)
    return pl.pallas_call(
        _front_kernel,
        grid=(ROWS // INPROJ_TM,),
        in_specs=[
            pl.BlockSpec(memory_space=pltpu.SMEM),
            pl.BlockSpec((INPROJ_TM, D_MODEL), lambda i: (i, 0)),
            pl.BlockSpec((1, 1, D_MODEL), lambda i: (i // tiles_per_batch, 0, 0)),
            pl.BlockSpec((1, 1, D_MODEL), lambda i: (i // tiles_per_batch, 0, 0)),
            pl.BlockSpec((1, D_MODEL), lambda i: (0, 0)),
            pl.BlockSpec((1, MAIN_WIDTH, D_MODEL), lambda i: (layer, 0, 0), pipeline_mode=const),
            pl.BlockSpec((1, G_WIDTH, D_MODEL), lambda i: (layer, 0, 0), pipeline_mode=const),
            pl.BlockSpec((SG_GROUPS, SG_CHUNK, SG_CHUNK), lambda i: (0, 0, 0), pipeline_mode=const),
            pl.BlockSpec((SG_CHUNK, SG_WIDTH), lambda i: (0, 0), pipeline_mode=const),
            pl.BlockSpec((1, SG_WIDTH), lambda i: (0, 0)),
            pl.BlockSpec((1, SG_WIDTH), lambda i: (0, 0)),
            pl.BlockSpec((INPROJ_TM, LANES), lambda i: (i, 0)),
            pl.BlockSpec((INPROJ_TM, LANES), lambda i: (i, 0)),
        ],
        out_specs=[pl.BlockSpec((INPROJ_TM, w), lambda i: (i, 0)) for w in widths],
        out_shape=[jax.ShapeDtypeStruct((ROWS, w), dt) for w, dt in zip(widths, dtypes)],
        scratch_shapes=[
            pltpu.VMEM((INPROJ_TM, D_MODEL), BF16),
            pltpu.VMEM((INPROJ_TM, B_WIDTH), F32),
            pltpu.VMEM((INPROJ_TM, A_WIDTH), F32),
            pltpu.VMEM((INPROJ_TM + WINDOW, SWA_KV_WIDTH), F32),
            pltpu.VMEM((INPROJ_TM + WINDOW, SWA_KV_WIDTH), F32),
            pltpu.VMEM((INPROJ_TM, SWA_WIDTH), BF16),
            pltpu.VMEM((2 * SWA_KV_HEADS, 2 * WINDOW, LANES), BF16),
            pltpu.VMEM((SWA_KV_HEADS, 2 * WINDOW, LANES), BF16),
            pltpu.VMEM((SWA_HEADS, WINDOW, WINDOW), F32),
            pltpu.VMEM((SWA_HEADS, WINDOW, 2 * WINDOW), BF16),
        ],
        compiler_params=pltpu.CompilerParams(
            dimension_semantics=("arbitrary",), vmem_limit_bytes=FRONT_VMEM_LIMIT),
        name="front",
    )(sinks, x2, scale, shift, g_pre, w_main, w_gate, sg_w, sg_bias_tile, sg_ln_g, sg_ln_b, cos_t, sin_t)


def _gla_sum_matrix():
    c = GLA_CHUNK
    mat = np.zeros(((2 + GLA_LEVELS) * c, c), np.float32)
    for t in range(c):
        mat[t, :t + 1] = 1.0
        mat[c + t, t + 1:] = 1.0
        for k in range(GLA_LEVELS):
            m = 1 << k
            r = (t >> (k + 1) << (k + 1)) + m
            row = (2 + k) * c + t
            if (t >> k) & 1:
                mat[row, r + 1:t + 1] = 1.0
            else:
                mat[row, t + 1:r + 1] = 1.0
    return mat


def _split_heads_on_rows(x, low_half):
    zero = jnp.zeros_like(x)
    return jnp.concatenate([jnp.where(low_half, x, zero), jnp.where(low_half, zero, x)], axis=0)


def _gla_kernel(q_ref, k_ref, v_ref, cg_ref, wup_ref, bg_ref, summat_ref, ng_ref, o_ref, st_ref, sc_ref):
    c = GLA_CHUNK
    heads_per_tile = LANES // GLA_DK
    n_tiles = BATCH * GLA_KEY_WIDTH // LANES

    @pl.when(pl.program_id(0) == 0)
    def _():
        st_ref[...] = jnp.zeros_like(st_ref)

    cg = cg_ref[...].reshape(BATCH * c, G_WIDTH).astype(BF16)
    logits = jnp.dot(cg, wup_ref[...], preferred_element_type=F32) + bg_ref[...]
    log_alpha = (jnp.minimum(logits, 0.0) - jnp.log1p(jnp.exp(-jnp.abs(logits)))) * (1.0 / GLA_GATE_TAU)

    hi = log_alpha.astype(BF16)
    r1 = log_alpha - hi.astype(F32)
    mid = r1.astype(BF16)
    lo = (r1 - mid.astype(F32)).astype(BF16)
    pieces = jnp.concatenate(
        [jnp.concatenate([p[b * c:(b + 1) * c] for p in (hi, mid, lo)], axis=0) for b in range(BATCH)],
        axis=1)
    expo = jnp.dot(summat_ref[0:2 * c, :], pieces, preferred_element_type=F32)

    q = jnp.concatenate([q_ref[b] for b in range(BATCH)], axis=1).astype(F32) * (GLA_DK ** -0.5)
    k = jnp.concatenate([k_ref[b] for b in range(BATCH)], axis=1).astype(F32)
    b_cum = expo[0:c]
    q_inter = (q * jnp.exp(b_cum)).astype(BF16)
    k_state = (k * jnp.exp(expo[c:2 * c])).astype(BF16)
    b_last = b_cum[c - 1:c, :]
    decay = jnp.exp(b_last)

    t_i = lax.broadcasted_iota(jnp.int32, (heads_per_tile * c, c), 0) % c
    s_i = lax.broadcasted_iota(jnp.int32, (heads_per_tile * c, c), 1)
    low_half = lax.broadcasted_iota(jnp.int32, (1, LANES), 1) < GLA_DK

    safe = jnp.min(b_last) >= -GLA_SAFE_DECAY

    @pl.when(safe)
    def _():
        k_grown = (k * jnp.exp(-b_cum)).astype(BF16)
        for j in range(n_tiles):
            sl = slice(j * LANES, (j + 1) * LANES)
            a2 = _dot_nt(_split_heads_on_rows(q_inter[:, sl], low_half), k_grown[:, sl])
            sc_ref[j] = jnp.where(s_i <= t_i, a2, 0.0).astype(BF16)

    @pl.when(jnp.logical_not(safe))
    def _():
        lv_expo = jnp.dot(summat_ref[2 * c:, :], pieces, preferred_element_type=F32)
        q_lv, k_lv, masks = [q.astype(BF16)], [k.astype(BF16)], [t_i == s_i]
        for lv in range(GLA_LEVELS):
            f = jnp.exp(lv_expo[lv * c:(lv + 1) * c])
            q_lv.append((q * f).astype(BF16))
            k_lv.append((k * f).astype(BF16))
            masks.append(((t_i >> (lv + 1)) == (s_i >> (lv + 1)))
                         & (((t_i >> lv) & 1) == 1) & (((s_i >> lv) & 1) == 0))
        for j in range(n_tiles):
            sl = slice(j * LANES, (j + 1) * LANES)
            a2 = jnp.zeros((heads_per_tile * c, c), F32)
            for ql, kl, mask in zip(q_lv, k_lv, masks):
                a2 = a2 + jnp.where(mask, _dot_nt(_split_heads_on_rows(ql[:, sl], low_half), kl[:, sl]), 0.0)
            sc_ref[j] = a2.astype(BF16)

    scores = [sc_ref[j] for j in range(n_tiles)]
    tiles_per_seq = GLA_KEY_WIDTH // LANES
    for j in range(n_tiles):
        b, pair = divmod(j, tiles_per_seq)
        sl = slice(j * LANES, (j + 1) * LANES)
        st_cols = slice(pair * LANES, (pair + 1) * LANES)
        st = st_ref[b, :, st_cols]
        inter = _dot_nt(_split_heads_on_rows(q_inter[:, sl], low_half), st.astype(BF16))
        updates = []
        for u in range(heads_per_tile):
            h = pair * heads_per_tile + u
            vs = slice(h * GLA_DV, (h + 1) * GLA_DV)
            vh = v_ref[b, :, vs]
            o = jnp.dot(scores[j][u * c:(u + 1) * c], vh, preferred_element_type=F32) + inter[u * c:(u + 1) * c]
            y = o * lax.rsqrt(jnp.mean(o * o, axis=-1, keepdims=True) + EPS) * ng_ref[...]
            o_ref[b, :, vs] = y.astype(o_ref.dtype)
            updates.append(_dot_tn(vh, k_state[:, sl]))
        st_ref[b, :, st_cols] = st * decay[:, sl] + jnp.where(low_half, updates[0], updates[1])


GLA_STEPS = SEQ // GLA_CHUNK
HOST_IN_ROWS = MAIN_WIDTH // GLA_STEPS
HOST_OUT_ROWS = D_MODEL // GLA_STEPS


def _gla_hosting_kernel(q_ref, k_ref, v_ref, cg_ref, wup_ref, bg_ref, summat_ref, ng_ref,
                        wa_ref, wo_ref, o_ref, main_ref, gate_ref, wout_ref, st_ref, sc_ref):
    _gla_kernel(q_ref, k_ref, v_ref, cg_ref, wup_ref, bg_ref, summat_ref, ng_ref, o_ref, st_ref, sc_ref)

    c = pl.program_id(0)
    mixed_blk, split = divmod(GATE_COL, HOST_IN_ROWS)

    @pl.when(c < mixed_blk)
    def _():
        main_ref[0] = wa_ref[0, :HOST_IN_ROWS, :].astype(BF16)

    @pl.when(c == mixed_blk)
    def _():
        main_ref[0] = jnp.concatenate(
            [wa_ref[0, :split, :], wa_ref[0, split + GLA_GATE_RANK:, :]], axis=0).astype(BF16)
        gate_ref[0] = jnp.concatenate(
            [wa_ref[0, split:split + GLA_GATE_RANK, :], jnp.zeros((G_WIDTH - GLA_GATE_RANK, D_MODEL), F32)],
            axis=0).astype(BF16)

    @pl.when(c > mixed_blk)
    def _():
        main_ref[0] = wa_ref[0, GLA_GATE_RANK:, :].astype(BF16)

    wout_ref[0] = wo_ref[0].astype(BF16)


def _gla(proj_c, proj_g, wup_pad, b_gate, summat3, norm_g, next_layer=None, w_in_t=None, w_out=None):
    in_specs = [
        pl.BlockSpec((BATCH, GLA_CHUNK, GLA_KEY_WIDTH), lambda c: (0, c, 0)),
        pl.BlockSpec((BATCH, GLA_CHUNK, GLA_KEY_WIDTH), lambda c: (0, c, 1)),
        pl.BlockSpec((BATCH, GLA_CHUNK, GLA_WIDTH), lambda c: (0, c, 1)),
        pl.BlockSpec((BATCH, GLA_CHUNK, G_WIDTH), lambda c: (0, c, 0)),
        pl.BlockSpec((G_WIDTH, GLA_KEY_WIDTH), lambda c: (0, 0)),
        pl.BlockSpec((1, GLA_KEY_WIDTH), lambda c: (0, 0)),
        pl.BlockSpec(((2 + GLA_LEVELS) * GLA_CHUNK, 3 * GLA_CHUNK), lambda c: (0, 0)),
        pl.BlockSpec((1, GLA_DV), lambda c: (0, 0)),
    ]
    out_specs = [pl.BlockSpec((BATCH, GLA_CHUNK, GLA_WIDTH), lambda c: (0, c, 0))]
    out_shape = [jax.ShapeDtypeStruct((BATCH, SEQ, GLA_WIDTH), BF16)]
    operands = [proj_c, proj_c, proj_c, proj_g, wup_pad, b_gate, summat3, norm_g]
    body = _gla_kernel
    if next_layer is not None:
        assert GLA_STEPS * HOST_IN_ROWS + GLA_GATE_RANK == IN_PROJ_WIDTH
        in_specs += [
            pl.BlockSpec((pl.Element(1), pl.Element(HOST_IN_ROWS + GLA_GATE_RANK), pl.Element(D_MODEL)),
                         lambda c: (next_layer, c * HOST_IN_ROWS, 0)),
            pl.BlockSpec((1, HOST_OUT_ROWS, D_MODEL), lambda c: (next_layer, c, 0)),
        ]
        out_specs += [
            pl.BlockSpec((1, HOST_IN_ROWS, D_MODEL), lambda c: (0, c, 0)),
            pl.BlockSpec((1, G_WIDTH, D_MODEL), lambda c: (0, 0, 0)),
            pl.BlockSpec((1, HOST_OUT_ROWS, D_MODEL), lambda c: (0, c, 0)),
        ]
        out_shape += [jax.ShapeDtypeStruct((1, MAIN_WIDTH, D_MODEL), BF16),
                      jax.ShapeDtypeStruct((1, G_WIDTH, D_MODEL), BF16),
                      jax.ShapeDtypeStruct((1, D_MODEL, D_MODEL), BF16)]
        operands += [w_in_t, w_out]
        body = _gla_hosting_kernel
    return pl.pallas_call(
        body,
        grid=(GLA_STEPS,),
        in_specs=in_specs,
        out_specs=out_specs,
        out_shape=out_shape,
        scratch_shapes=[pltpu.VMEM((BATCH, GLA_DV, GLA_KEY_WIDTH), F32),
                        pltpu.VMEM((BATCH * GLA_KEY_WIDTH // LANES, 2 * GLA_CHUNK, GLA_CHUNK), BF16)],
        compiler_params=pltpu.CompilerParams(
            dimension_semantics=("arbitrary",), vmem_limit_bytes=VMEM_LIMIT),
        name="gla",
    )(*operands)


OUT_TM = 512
OUT_TN = 256


def _outproj_kernel(ya_ref, yb_ref, yc_ref, sz_ref, x_ref, gate_ref, g_ref, w_ref, o_ref,
                    yg_ref, ssq_ref, inv_ref, *acc_refs):
    i = pl.program_id(0)
    n_tiles = ROWS // OUT_TM

    @pl.when(i == 0)
    def _():
        for acc_ref in acc_refs:
            acc_ref[...] = jnp.zeros_like(acc_ref)
        ssq_ref[...] = jnp.zeros_like(ssq_ref)

    inv_ref[...] = lax.rsqrt(ssq_ref[...] * (1.0 / D_MODEL) + EPS)
    ssq_ref[...] = jnp.zeros_like(ssq_ref)
    out_gain = g_ref[...] * gate_ref[0]

    def finalise(blk):
        cols = slice(blk * OUT_TN, (blk + 1) * OUT_TN)
        o_ref[:, cols] = x_ref[:, cols] + acc_refs[blk][...] * inv_ref[:, 0:1] * out_gain[:, cols]

    def matmul(blk):
        r = jnp.dot(yg_ref[...], w_ref[0, :, blk * OUT_TN:(blk + 1) * OUT_TN], preferred_element_type=F32)
        acc_refs[blk][...] = r
        ssq_ref[...] += jnp.sum(r * r, axis=-1, keepdims=True)

    @pl.when(i < n_tiles)
    def _():
        col = 0
        for y_ref in (ya_ref, yb_ref, yc_ref):
            width = y_ref.shape[1]
            yg_ref[:, col:col + width] = y_ref[...] * sz_ref[:, col:col + width]
            col += width
        for blk in range(len(acc_refs)):
            finalise(blk)
            matmul(blk)

    @pl.when(i == n_tiles)
    def _():
        for blk in range(len(acc_refs)):
            finalise(blk)


def _outproj(layer, y_a, y_b, y_c, silu_z, x2, gate, g_post, w_out_bf):
    tiles_per_batch = SEQ // OUT_TM
    n_tiles = ROWS // OUT_TM
    mixed = lambda i: jnp.minimum(i, n_tiles - 1)
    done = lambda i: jnp.maximum(i - 1, 0)
    return pl.pallas_call(
        _outproj_kernel,
        grid=(n_tiles + 1,),
        in_specs=[
            pl.BlockSpec((OUT_TM, SWA_WIDTH), lambda i: (mixed(i), 0)),
            pl.BlockSpec((OUT_TM, SG_WIDTH), lambda i: (mixed(i), 0)),
            pl.BlockSpec((OUT_TM, GLA_WIDTH), lambda i: (mixed(i), 0)),
            pl.BlockSpec((OUT_TM, Z_WIDTH), lambda i: (mixed(i), 0)),
            pl.BlockSpec((OUT_TM, D_MODEL), lambda i: (done(i), 0)),
            pl.BlockSpec((1, 1, D_MODEL), lambda i: (done(i) // tiles_per_batch, 0, 0)),
            pl.BlockSpec((1, D_MODEL), lambda i: (0, 0)),
            pl.BlockSpec((1, D_MODEL, D_MODEL), lambda i: (layer, 0, 0), pipeline_mode=pl.Buffered(1)),
        ],
        out_specs=pl.BlockSpec((OUT_TM, D_MODEL), lambda i: (done(i), 0)),
        out_shape=jax.ShapeDtypeStruct((ROWS, D_MODEL), F32),
        scratch_shapes=(
            [pltpu.VMEM((OUT_TM, D_MODEL), BF16),
             pltpu.VMEM((OUT_TM, LANES), F32),
             pltpu.VMEM((OUT_TM, LANES), F32)]
            + [pltpu.VMEM((OUT_TM, OUT_TN), F32)] * (D_MODEL // OUT_TN)),
        compiler_params=pltpu.CompilerParams(
            dimension_semantics=("arbitrary",), vmem_limit_bytes=VMEM_LIMIT),
        name="gate_outproj_residual",
    )(y_a, y_b, y_c, silu_z, x2, gate, g_post, w_out_bf)


def kernel(x, c, positions, w_mod, b_mod, g_pre, g_post, w_in, w_out, swa_sinks,
           sg_w, sg_b, sg_ln_g, sg_ln_b, gla_w_gate_up, gla_b_gate, gla_norm_g):
    assert x.shape == (BATCH, SEQ, D_MODEL) and w_in.shape[0] == DEPTH

    c_pad = jnp.pad(c, ((0, 8 - BATCH), (0, 0)))
    mod = _modulation(c_pad, w_mod, b_mod)[:, :BATCH, :]
    cos_t, sin_t = _rope_tables(positions)
    summat3 = jnp.asarray(np.tile(_gla_sum_matrix(), (1, 3)), dtype=BF16)

    w_in_t = jnp.swapaxes(w_in, 1, 2)
    w_main, w_gate = _prep_w_in(w_in_t)
    w_out_bf = _prep_w_out(w_out)

    x2 = x.reshape(ROWS, D_MODEL)
    for l in range(DEPTH):
        shift = mod[l, :, 0:D_MODEL].reshape(BATCH, 1, D_MODEL)
        scale = mod[l, :, D_MODEL:2 * D_MODEL].reshape(BATCH, 1, D_MODEL)
        gate = mod[l, :, 2 * D_MODEL:].reshape(BATCH, 1, D_MODEL)
        bias_tile = jnp.repeat(sg_b[l].T, SG_GROUP_DIM, axis=1)
        y_a, y_b, proj_c, silu_z, proj_g = _front(
            0, x2, scale, shift, g_pre[l].reshape(1, D_MODEL), w_main, w_gate, sg_w[l], bias_tile,
            sg_ln_g[l].reshape(1, SG_WIDTH), sg_ln_b[l].reshape(1, SG_WIDTH), swa_sinks[l], cos_t, sin_t)
        wup_pad = jnp.pad(gla_w_gate_up[l], ((0, G_WIDTH - GLA_GATE_RANK), (0, 0))).astype(BF16)
        gla_args = (proj_c.reshape(BATCH, SEQ, C_WIDTH), proj_g.reshape(BATCH, SEQ, G_WIDTH), wup_pad,
                    gla_b_gate[l].reshape(1, GLA_KEY_WIDTH), summat3, gla_norm_g[l].reshape(1, GLA_DV))
        this_w_out = w_out_bf
        if l + 1 < DEPTH:
            y_c, w_main, w_gate, w_out_bf = _gla(*gla_args, next_layer=l + 1, w_in_t=w_in_t, w_out=w_out)
        else:
            y_c, = _gla(*gla_args)

        x2 = _outproj(0, y_a, y_b, y_c.reshape(ROWS, GLA_WIDTH), silu_z, x2, gate,
                      g_post[l].reshape(1, D_MODEL), this_w_out)
    return x2.reshape(BATCH, SEQ, D_MODEL)
```

```python
import functools

import numpy as np
import jax
import jax.numpy as jnp
from jax import lax
from jax.experimental import pallas as pl
from jax.experimental.pallas import tpu as pltpu

F32 = jnp.float32
BF16 = jnp.bfloat16

D_MODEL = 2048
BATCH = 4
SEQ = 2048
DEPTH = 2
EPS = 1e-6
ROWS = BATCH * SEQ

SWA_HEAD_DIM = 64
SWA_HEADS = 16
SWA_KV_HEADS = 4
SWA_GROUP = SWA_HEADS // SWA_KV_HEADS
SWA_WIDTH = SWA_HEADS * SWA_HEAD_DIM
SWA_KV_WIDTH = SWA_KV_HEADS * SWA_HEAD_DIM
WINDOW = 128
ROT_DIM = 16
ROT_HALF = ROT_DIM // 2
ROPE_THETA = 500000.0

SG_WIDTH = 512
SG_GROUPS = 8
SG_GROUP_DIM = 64
SG_CHUNK = 128

GLA_HEADS = 4
GLA_WIDTH = 512
GLA_DV = 128
GLA_DK = 64
GLA_KEY_WIDTH = 256
GLA_GATE_RANK = 16
GLA_GATE_TAU = 16.0
GLA_CHUNK = 128
GLA_LEVELS = 7
GLA_SAFE_DECAY = 40.0

LANES = 128
SUBLANES = 8
A_WIDTH = SWA_WIDTH + 2 * SWA_KV_WIDTH
B_WIDTH = 2 * SG_WIDTH
C_WIDTH = 2 * GLA_KEY_WIDTH + GLA_WIDTH
Z_WIDTH = D_MODEL
G_WIDTH = LANES
MOD_WIDTH = 3 * D_MODEL

V7X_VMEM_BYTES = 64 * 1024 * 1024
VMEM_LIMIT = V7X_VMEM_BYTES - 8 * 1024 * 1024
FRONT_VMEM_LIMIT = V7X_VMEM_BYTES - 4 * 1024 * 1024


def _sigmoid(x):
    return 1.0 / (1.0 + jnp.exp(-x))


def _gelu_tanh(x):
    return 0.5 * x * (1.0 + jnp.tanh(float(np.sqrt(2.0 / np.pi)) * (x + 0.044715 * (x * x * x))))


def _dot_nt(a, b):
    return lax.dot_general(a, b, (((1,), (1,)), ((), ())), preferred_element_type=F32)


def _dot_tn(a, b):
    return lax.dot_general(a, b, (((0,), (0,)), ((), ())), preferred_element_type=F32)


MOD_TN = 1536


def _mod_kernel(c_ref, w_ref, b_ref, o_ref):
    c = c_ref[...]
    s = (c * _sigmoid(c)).astype(BF16)
    o_ref[0] = jnp.dot(s, w_ref[0].astype(BF16), preferred_element_type=F32) + b_ref[0]


def _modulation(c_pad, w_mod, b_mod):
    return pl.pallas_call(
        _mod_kernel,
        grid=(DEPTH, MOD_WIDTH // MOD_TN),
        in_specs=[
            pl.BlockSpec((SUBLANES, D_MODEL), lambda l, j: (0, 0)),
            pl.BlockSpec((1, D_MODEL, MOD_TN), lambda l, j: (l, 0, j)),
            pl.BlockSpec((1, 1, MOD_TN), lambda l, j: (l, 0, j)),
        ],
        out_specs=pl.BlockSpec((1, SUBLANES, MOD_TN), lambda l, j: (l, 0, j)),
        out_shape=jax.ShapeDtypeStruct((DEPTH, SUBLANES, MOD_WIDTH), F32),
        compiler_params=pltpu.CompilerParams(
            dimension_semantics=("arbitrary", "arbitrary"), vmem_limit_bytes=VMEM_LIMIT),
        name="adaln_mod",
    )(c_pad, w_mod, b_mod.reshape(DEPTH, 1, MOD_WIDTH))


ROPE_TM = 2048


def _rope_table_kernel(pos_ref, invf_ref, cos_ref, sin_ref):
    ang = pos_ref[...].astype(F32) * invf_ref[...]
    lane = lax.broadcasted_iota(jnp.int32, (1, LANES), 1) % SWA_HEAD_DIM
    s = jnp.sin(ang)
    cos_ref[...] = jnp.cos(ang)
    sin_ref[...] = jnp.where(lane < ROT_HALF, -s, s)


def _rope_tables(positions):
    half = np.arange(ROT_HALF, dtype=np.float32)
    inv_freq = (np.float32(ROPE_THETA) ** (-(half * np.float32(2.0 / ROT_DIM)))).astype(np.float32)
    lane = np.arange(LANES) % SWA_HEAD_DIM
    invf = np.where(lane < ROT_DIM, inv_freq[lane % ROT_HALF], 0.0).astype(np.float32)[None, :]
    return pl.pallas_call(
        _rope_table_kernel,
        grid=(ROWS // ROPE_TM,),
        in_specs=[
            pl.BlockSpec((ROPE_TM, 1), lambda i: (i, 0)),
            pl.BlockSpec((1, LANES), lambda i: (0, 0)),
        ],
        out_specs=[pl.BlockSpec((ROPE_TM, LANES), lambda i: (i, 0))] * 2,
        out_shape=[jax.ShapeDtypeStruct((ROWS, LANES), F32)] * 2,
        compiler_params=pltpu.CompilerParams(dimension_semantics=("arbitrary",)),
        name="rope_tables",
    )(positions.reshape(ROWS, 1), jnp.asarray(invf))


PREP_TN = 512
MAIN_WIDTH = A_WIDTH + B_WIDTH + C_WIDTH + Z_WIDTH
GATE_COL = A_WIDTH + B_WIDTH + C_WIDTH
IN_PROJ_WIDTH = GATE_COL + GLA_GATE_RANK + Z_WIDTH


def _prep_w_in_kernel(a_ref, main_ref, gate_ref):
    j = pl.program_id(1)
    first_z = GATE_COL // PREP_TN

    @pl.when(j < first_z)
    def _():
        main_ref[0] = a_ref[0, :PREP_TN, :].astype(BF16)

    @pl.when(j >= first_z)
    def _():
        main_ref[0] = a_ref[0, GLA_GATE_RANK:, :].astype(BF16)

    @pl.when(j == first_z)
    def _():
        gate_ref[0] = jnp.concatenate(
            [a_ref[0, :GLA_GATE_RANK, :], jnp.zeros((G_WIDTH - GLA_GATE_RANK, D_MODEL), F32)],
            axis=0).astype(BF16)


def _prep_w_in(w_in_t):
    assert GATE_COL % PREP_TN == 0 and MAIN_WIDTH + GLA_GATE_RANK == IN_PROJ_WIDTH
    return pl.pallas_call(
        _prep_w_in_kernel,
        grid=(1, MAIN_WIDTH // PREP_TN),
        in_specs=[
            pl.BlockSpec((pl.Element(1), pl.Element(PREP_TN + GLA_GATE_RANK), pl.Element(D_MODEL)),
                         lambda l, j: (l, j * PREP_TN, 0)),
        ],
        out_specs=[
            pl.BlockSpec((1, PREP_TN, D_MODEL), lambda l, j: (l, j, 0)),
            pl.BlockSpec((1, G_WIDTH, D_MODEL), lambda l, j: (l, 0, 0)),
        ],
        out_shape=[jax.ShapeDtypeStruct((1, MAIN_WIDTH, D_MODEL), BF16),
                   jax.ShapeDtypeStruct((1, G_WIDTH, D_MODEL), BF16)],
        compiler_params=pltpu.CompilerParams(
            dimension_semantics=("arbitrary", "arbitrary"), vmem_limit_bytes=VMEM_LIMIT),
        name="prep_w_in",
    )(w_in_t)


def _cast_kernel(x_ref, o_ref):
    o_ref[...] = x_ref[...].astype(o_ref.dtype)


def _prep_w_out(w_out):
    tn = 1024
    return pl.pallas_call(
        _cast_kernel,
        grid=(1, D_MODEL // tn),
        in_specs=[pl.BlockSpec((1, D_MODEL, tn), lambda l, j: (l, 0, j))],
        out_specs=pl.BlockSpec((1, D_MODEL, tn), lambda l, j: (l, 0, j)),
        out_shape=jax.ShapeDtypeStruct((1, D_MODEL, D_MODEL), BF16),
        compiler_params=pltpu.CompilerParams(
            dimension_semantics=("arbitrary", "arbitrary"), vmem_limit_bytes=VMEM_LIMIT),
        name="prep_w_out",
    )(w_out)


INPROJ_TM = 512
INPROJ_CHUNK = 256


def _spatial_gating_chunk(u, v, w_bf, bias, ln_g, ln_b):
    u = _gelu_tanh(u)
    v = _gelu_tanh(v)
    mu = jnp.mean(v, axis=-1, keepdims=True)
    vc = v - mu
    var = jnp.mean(vc * vc, axis=-1, keepdims=True)
    vn = (vc * lax.rsqrt(var + EPS) * ln_g + ln_b).astype(BF16)
    parts = [jnp.dot(w_bf[g], vn[:, g * SG_GROUP_DIM:(g + 1) * SG_GROUP_DIM], preferred_element_type=F32)
             for g in range(SG_GROUPS)]
    return u * (jnp.concatenate(parts, axis=1) + bias)


def _swa_pieces(tile_has_prev, a_ref, cos_ref, sin_ref, sinks_ref, ya_ref,
                kband_ref, vband_ref, qr_ref, ksel_ref, vdup_ref, s_ref, p_ref):
    heads_per_tile = LANES // SWA_HEAD_DIM
    n_blocks = INPROJ_TM // WINDOW
    lane = lax.broadcasted_iota(jnp.int32, (1, LANES), 1)
    first_half = (lane % SWA_HEAD_DIM) < ROT_HALF
    low_half = lane < SWA_HEAD_DIM
    qi = lax.broadcasted_iota(jnp.int32, (WINDOW, WINDOW), 0)
    kj = lax.broadcasted_iota(jnp.int32, (WINDOW, WINDOW), 1)
    from_prev = kj > qi

    def rope(t):
        partner = jnp.where(first_half, pltpu.roll(t, LANES - ROT_HALF, 1), pltpu.roll(t, ROT_HALF, 1))
        return t * cos_ref[...] + partner * sin_ref[...]

    steps = []

    def carry():
        kband_ref[0:WINDOW, :] = kband_ref[INPROJ_TM:INPROJ_TM + WINDOW, :]
        vband_ref[0:WINDOW, :] = vband_ref[INPROJ_TM:INPROJ_TM + WINDOW, :]
    steps.append(carry)

    def stage_k(t):
        cols = slice(t * LANES, (t + 1) * LANES)
        kband_ref[WINDOW:, cols] = rope(a_ref[:, SWA_WIDTH + t * LANES:SWA_WIDTH + (t + 1) * LANES])
        vband_ref[WINDOW:, cols] = a_ref[:, SWA_WIDTH + SWA_KV_WIDTH + t * LANES:
                                         SWA_WIDTH + SWA_KV_WIDTH + (t + 1) * LANES]
    for t in range(SWA_KV_WIDTH // LANES):
        steps.append(functools.partial(stage_k, t))

    def stage_q(t):
        cols = slice(t * LANES, (t + 1) * LANES)
        qr_ref[:, cols] = rope(a_ref[:, cols] * (SWA_HEAD_DIM ** -0.5)).astype(BF16)
    for t in range(SWA_WIDTH // LANES):
        steps.append(functools.partial(stage_q, t))

    def stage_kv(blk, t):
        rows = slice(blk * WINDOW, (blk + 2) * WINDOW)
        cols = slice(t * LANES, (t + 1) * LANES)
        kt = kband_ref[rows, cols]
        vt = vband_ref[rows, cols]
        kt_sw = pltpu.roll(kt, SWA_HEAD_DIM, 1)
        vt_sw = pltpu.roll(vt, SWA_HEAD_DIM, 1)
        for u, (k_lo, k_hi, v_lo, v_hi) in enumerate(((kt, kt_sw, vt, vt_sw), (kt_sw, kt, vt_sw, vt))):
            g = t * heads_per_tile + u
            ksel_ref[2 * g] = jnp.where(low_half, k_lo, 0.0).astype(BF16)
            ksel_ref[2 * g + 1] = jnp.where(low_half, 0.0, k_hi).astype(BF16)
            vdup_ref[g] = jnp.where(low_half, v_lo, v_hi).astype(BF16)

    def scores(blk, h):
        t, u = divmod(h, heads_per_tile)
        q = qr_ref[blk * WINDOW:(blk + 1) * WINDOW, t * LANES:(t + 1) * LANES]
        s2 = _dot_nt(q, ksel_ref[2 * (h // SWA_GROUP) + u])
        s_ref[h] = jnp.where(from_prev, s2[:, :WINDOW], s2[:, WINDOW:])

    def softmax(blk, h):
        s = s_ref[h]
        if blk == 0:
            s = jnp.where(kj <= qi + tile_has_prev * WINDOW, s, -jnp.inf)
        sink = sinks_ref[h]
        m = jnp.maximum(jnp.max(s, axis=-1, keepdims=True), sink)
        p = jnp.exp(s - m)
        denom = jnp.sum(p, axis=-1, keepdims=True) + jnp.exp(sink - m)
        p = (p * (1.0 / denom)).astype(BF16)
        zero = jnp.zeros_like(p)
        p_ref[h, :, :WINDOW] = jnp.where(from_prev, p, zero)
        p_ref[h, :, WINDOW:] = jnp.where(from_prev, zero, p)

    def values(blk, t):
        vg = vdup_ref[(t * heads_per_tile) // SWA_GROUP]
        outs = [jnp.dot(p_ref[t * heads_per_tile + u], vg, preferred_element_type=F32)
                for u in range(heads_per_tile)]
        ya_ref[blk * WINDOW:(blk + 1) * WINDOW, t * LANES:(t + 1) * LANES] = (
            jnp.where(low_half, outs[0], outs[1]).astype(ya_ref.dtype))

    for blk in range(n_blocks):
        for t in range(SWA_KV_WIDTH // LANES):
            steps.append(functools.partial(stage_kv, blk, t))
        for h in range(SWA_HEADS):
            steps.append(functools.partial(scores, blk, h))
        for h in range(SWA_HEADS):
            steps.append(functools.partial(softmax, blk, h))
        for t in range(SWA_WIDTH // LANES):
            steps.append(functools.partial(values, blk, t))
    return steps


def _interleave(main_steps, side_steps):
    done = 0
    for idx, step in enumerate(main_steps):
        step()
        upto = (idx + 1) * len(side_steps) // len(main_steps)
        for side in side_steps[done:upto]:
            side()
        done = upto


def _front_kernel(sinks_ref, x_ref, scale_ref, shift_ref, g_ref, w_ref, wg_ref,
                  sgw_ref, sgb_ref, lng_ref, lnb_ref, cos_ref, sin_ref,
                  ya_ref, yb_ref, oc_ref, oz_ref, og_ref,
                  h_ref, uv_ref, a_ref, kband_ref, vband_ref, qr_ref, ksel_ref, vdup_ref, s_ref, p_ref):
    i = pl.program_id(0)

    @pl.when(i == 0)
    def _():
        kband_ref[...] = jnp.zeros_like(kband_ref)
        vband_ref[...] = jnp.zeros_like(vband_ref)

    x = x_ref[...]
    ms = jnp.mean(x * x, axis=-1, keepdims=True)
    y = x * lax.rsqrt(ms + EPS) * g_ref[...]
    h_ref[...] = (y * (1.0 + scale_ref[0]) + shift_ref[0]).astype(BF16)

    def project(o_ref, col, c0, width, post=None):
        r = _dot_nt(h_ref[...], w_ref[0, col + c0:col + c0 + width, :])
        if post is not None:
            r = post(r)
        o_ref[:, c0:c0 + width] = r.astype(o_ref.dtype)

    def projection_steps(o_ref, col, post=None):
        return [functools.partial(project, o_ref, col, c0, INPROJ_CHUNK, post)
                for c0 in range(0, o_ref.shape[1], INPROJ_CHUNK)]

    for step in projection_steps(uv_ref, A_WIDTH):
        step()

    t = lax.broadcasted_iota(jnp.int32, (SG_CHUNK, SG_CHUNK), 0)
    s = lax.broadcasted_iota(jnp.int32, (SG_CHUNK, SG_CHUNK), 1)
    w_bf = [jnp.where(t >= s, sgw_ref[g], 0.0).astype(BF16) for g in range(SG_GROUPS)]

    def mixer_b(r0):
        rows = slice(r0, r0 + SG_CHUNK)
        yb_ref[rows, :] = _spatial_gating_chunk(
            uv_ref[rows, :SG_WIDTH], uv_ref[rows, SG_WIDTH:], w_bf, sgb_ref[...], lng_ref[...],
            lnb_ref[...]).astype(yb_ref.dtype)

    _interleave(projection_steps(a_ref, 0),
                [functools.partial(mixer_b, r0) for r0 in range(0, INPROJ_TM, SG_CHUNK)])

    tile_has_prev = ((i % (SEQ // INPROJ_TM)) != 0).astype(jnp.int32)
    swa_steps = _swa_pieces(tile_has_prev, a_ref, cos_ref, sin_ref, sinks_ref, ya_ref,
                            kband_ref, vband_ref, qr_ref, ksel_ref, vdup_ref, s_ref, p_ref)

    def gate_rank():
        og_ref[...] = _dot_nt(h_ref[...], wg_ref[0])

    silu = lambda z: z * (0.5 * jnp.tanh(0.5 * z) + 0.5)
    _interleave(projection_steps(oc_ref, A_WIDTH + B_WIDTH)
                + projection_steps(oz_ref, A_WIDTH + B_WIDTH + C_WIDTH, silu) + [gate_rank],
                swa_steps)


def _front(layer, x2, scale, shift, g_pre, w_main, w_gate, sg_w, sg_bias_tile, sg_ln_g, sg_ln_b,
           sinks, cos_t, sin_t):
    tiles_per_batch = SEQ // INPROJ_TM
    widths = (SWA_WIDTH, SG_WIDTH, C_WIDTH, Z_WIDTH, G_WIDTH)
    dtypes = (BF16, BF16, BF16, BF16, F32)
    const = pl.Buffered(1)
    return pl.pallas_call(
        _front_kernel,
        grid=(ROWS // INPROJ_TM,),
        in_specs=[
            pl.BlockSpec(memory_space=pltpu.SMEM),
            pl.BlockSpec((INPROJ_TM, D_MODEL), lambda i: (i, 0)),
            pl.BlockSpec((1, 1, D_MODEL), lambda i: (i // tiles_per_batch, 0, 0)),
            pl.BlockSpec((1, 1, D_MODEL), lambda i: (i // tiles_per_batch, 0, 0)),
            pl.BlockSpec((1, D_MODEL), lambda i: (0, 0)),
            pl.BlockSpec((1, MAIN_WIDTH, D_MODEL), lambda i: (layer, 0, 0), pipeline_mode=const),
            pl.BlockSpec((1, G_WIDTH, D_MODEL), lambda i: (layer, 0, 0), pipeline_mode=const),
            pl.BlockSpec((SG_GROUPS, SG_CHUNK, SG_CHUNK), lambda i: (0, 0, 0), pipeline_mode=const),
            pl.BlockSpec((SG_CHUNK, SG_WIDTH), lambda i: (0, 0), pipeline_mode=const),
            pl.BlockSpec((1, SG_WIDTH), lambda i: (0, 0)),
            pl.BlockSpec((1, SG_WIDTH), lambda i: (0, 0)),
            pl.BlockSpec((INPROJ_TM, LANES), lambda i: (i, 0)),
            pl.BlockSpec((INPROJ_TM, LANES), lambda i: (i, 0)),
        ],
        out_specs=[pl.BlockSpec((INPROJ_TM, w), lambda i: (i, 0)) for w in widths],
        out_shape=[jax.ShapeDtypeStruct((ROWS, w), dt) for w, dt in zip(widths, dtypes)],
        scratch_shapes=[
            pltpu.VMEM((INPROJ_TM, D_MODEL), BF16),
            pltpu.VMEM((INPROJ_TM, B_WIDTH), F32),
            pltpu.VMEM((INPROJ_TM, A_WIDTH), F32),
            pltpu.VMEM((INPROJ_TM + WINDOW, SWA_KV_WIDTH), F32),
            pltpu.VMEM((INPROJ_TM + WINDOW, SWA_KV_WIDTH), F32),
            pltpu.VMEM((INPROJ_TM, SWA_WIDTH), BF16),
            pltpu.VMEM((2 * SWA_KV_HEADS, 2 * WINDOW, LANES), BF16),
            pltpu.VMEM((SWA_KV_HEADS, 2 * WINDOW, LANES), BF16),
            pltpu.VMEM((SWA_HEADS, WINDOW, WINDOW), F32),
            pltpu.VMEM((SWA_HEADS, WINDOW, 2 * WINDOW), BF16),
        ],
        compiler_params=pltpu.CompilerParams(
            dimension_semantics=("arbitrary",), vmem_limit_bytes=FRONT_VMEM_LIMIT),
        name="front",
    )(sinks, x2, scale, shift, g_pre, w_main, w_gate, sg_w, sg_bias_tile, sg_ln_g, sg_ln_b, cos_t, sin_t)


def _gla_sum_matrix():
    c = GLA_CHUNK
    mat = np.zeros(((2 + GLA_LEVELS) * c, c), np.float32)
    for t in range(c):
        mat[t, :t + 1] = 1.0
        mat[c + t, t + 1:] = 1.0
        for k in range(GLA_LEVELS):
            m = 1 << k
            r = (t >> (k + 1) << (k + 1)) + m
            row = (2 + k) * c + t
            if (t >> k) & 1:
                mat[row, r + 1:t + 1] = 1.0
            else:
                mat[row, t + 1:r + 1] = 1.0
    return mat


def _split_heads_on_rows(x, low_half):
    zero = jnp.zeros_like(x)
    return jnp.concatenate([jnp.where(low_half, x, zero), jnp.where(low_half, zero, x)], axis=0)


def _gla_kernel(q_ref, k_ref, v_ref, cg_ref, wup_ref, bg_ref, summat_ref, ng_ref, o_ref, st_ref, sc_ref):
    c = GLA_CHUNK
    heads_per_tile = LANES // GLA_DK
    n_tiles = BATCH * GLA_KEY_WIDTH // LANES

    @pl.when(pl.program_id(0) == 0)
    def _():
        st_ref[...] = jnp.zeros_like(st_ref)

    cg = cg_ref[...].reshape(BATCH * c, G_WIDTH).astype(BF16)
    logits = jnp.dot(cg, wup_ref[...], preferred_element_type=F32) + bg_ref[...]
    log_alpha = (jnp.minimum(logits, 0.0) - jnp.log1p(jnp.exp(-jnp.abs(logits)))) * (1.0 / GLA_GATE_TAU)

    hi = log_alpha.astype(BF16)
    r1 = log_alpha - hi.astype(F32)
    mid = r1.astype(BF16)
    lo = (r1 - mid.astype(F32)).astype(BF16)
    pieces = jnp.concatenate(
        [jnp.concatenate([p[b * c:(b + 1) * c] for p in (hi, mid, lo)], axis=0) for b in range(BATCH)],
        axis=1)
    expo = jnp.dot(summat_ref[0:2 * c, :], pieces, preferred_element_type=F32)

    q = jnp.concatenate([q_ref[b] for b in range(BATCH)], axis=1).astype(F32) * (GLA_DK ** -0.5)
    k = jnp.concatenate([k_ref[b] for b in range(BATCH)], axis=1).astype(F32)
    b_cum = expo[0:c]
    q_inter = (q * jnp.exp(b_cum)).astype(BF16)
    k_state = (k * jnp.exp(expo[c:2 * c])).astype(BF16)
    b_last = b_cum[c - 1:c, :]
    decay = jnp.exp(b_last)

    t_i = lax.broadcasted_iota(jnp.int32, (heads_per_tile * c, c), 0) % c
    s_i = lax.broadcasted_iota(jnp.int32, (heads_per_tile * c, c), 1)
    low_half = lax.broadcasted_iota(jnp.int32, (1, LANES), 1) < GLA_DK

    safe = jnp.min(b_last) >= -GLA_SAFE_DECAY

    @pl.when(safe)
    def _():
        k_grown = (k * jnp.exp(-b_cum)).astype(BF16)
        for j in range(n_tiles):
            sl = slice(j * LANES, (j + 1) * LANES)
            a2 = _dot_nt(_split_heads_on_rows(q_inter[:, sl], low_half), k_grown[:, sl])
            sc_ref[j] = jnp.where(s_i <= t_i, a2, 0.0).astype(BF16)

    @pl.when(jnp.logical_not(safe))
    def _():
        lv_expo = jnp.dot(summat_ref[2 * c:, :], pieces, preferred_element_type=F32)
        q_lv, k_lv, masks = [q.astype(BF16)], [k.astype(BF16)], [t_i == s_i]
        for lv in range(GLA_LEVELS):
            f = jnp.exp(lv_expo[lv * c:(lv + 1) * c])
            q_lv.append((q * f).astype(BF16))
            k_lv.append((k * f).astype(BF16))
            masks.append(((t_i >> (lv + 1)) == (s_i >> (lv + 1)))
                         & (((t_i >> lv) & 1) == 1) & (((s_i >> lv) & 1) == 0))
        for j in range(n_tiles):
            sl = slice(j * LANES, (j + 1) * LANES)
            a2 = jnp.zeros((heads_per_tile * c, c), F32)
            for ql, kl, mask in zip(q_lv, k_lv, masks):
                a2 = a2 + jnp.where(mask, _dot_nt(_split_heads_on_rows(ql[:, sl], low_half), kl[:, sl]), 0.0)
            sc_ref[j] = a2.astype(BF16)

    scores = [sc_ref[j] for j in range(n_tiles)]
    tiles_per_seq = GLA_KEY_WIDTH // LANES
    for j in range(n_tiles):
        b, pair = divmod(j, tiles_per_seq)
        sl = slice(j * LANES, (j + 1) * LANES)
        st_cols = slice(pair * LANES, (pair + 1) * LANES)
        st = st_ref[b, :, st_cols]
        inter = _dot_nt(_split_heads_on_rows(q_inter[:, sl], low_half), st.astype(BF16))
        updates = []
        for u in range(heads_per_tile):
            h = pair * heads_per_tile + u
            vs = slice(h * GLA_DV, (h + 1) * GLA_DV)
            vh = v_ref[b, :, vs]
            o = jnp.dot(scores[j][u * c:(u + 1) * c], vh, preferred_element_type=F32) + inter[u * c:(u + 1) * c]
            y = o * lax.rsqrt(jnp.mean(o * o, axis=-1, keepdims=True) + EPS) * ng_ref[...]
            o_ref[b, :, vs] = y.astype(o_ref.dtype)
            updates.append(_dot_tn(vh, k_state[:, sl]))
        st_ref[b, :, st_cols] = st * decay[:, sl] + jnp.where(low_half, updates[0], updates[1])


GLA_STEPS = SEQ // GLA_CHUNK
HOST_IN_ROWS = MAIN_WIDTH // GLA_STEPS
HOST_OUT_ROWS = D_MODEL // GLA_STEPS


def _gla_hosting_kernel(q_ref, k_ref, v_ref, cg_ref, wup_ref, bg_ref, summat_ref, ng_ref,
                        wa_ref, wo_ref, o_ref, main_ref, gate_ref, wout_ref, st_ref, sc_ref):
    _gla_kernel(q_ref, k_ref, v_ref, cg_ref, wup_ref, bg_ref, summat_ref, ng_ref, o_ref, st_ref, sc_ref)

    c = pl.program_id(0)
    mixed_blk, split = divmod(GATE_COL, HOST_IN_ROWS)

    @pl.when(c < mixed_blk)
    def _():
        main_ref[0] = wa_ref[0, :HOST_IN_ROWS, :].astype(BF16)

    @pl.when(c == mixed_blk)
    def _():
        main_ref[0] = jnp.concatenate(
            [wa_ref[0, :split, :], wa_ref[0, split + GLA_GATE_RANK:, :]], axis=0).astype(BF16)
        gate_ref[0] = jnp.concatenate(
            [wa_ref[0, split:split + GLA_GATE_RANK, :], jnp.zeros((G_WIDTH - GLA_GATE_RANK, D_MODEL), F32)],
            axis=0).astype(BF16)

    @pl.when(c > mixed_blk)
    def _():
        main_ref[0] = wa_ref[0, GLA_GATE_RANK:, :].astype(BF16)

    wout_ref[0] = wo_ref[0].astype(BF16)


def _gla(proj_c, proj_g, wup_pad, b_gate, summat3, norm_g, next_layer=None, w_in_t=None, w_out=None):
    in_specs = [
        pl.BlockSpec((BATCH, GLA_CHUNK, GLA_KEY_WIDTH), lambda c: (0, c, 0)),
        pl.BlockSpec((BATCH, GLA_CHUNK, GLA_KEY_WIDTH), lambda c: (0, c, 1)),
        pl.BlockSpec((BATCH, GLA_CHUNK, GLA_WIDTH), lambda c: (0, c, 1)),
        pl.BlockSpec((BATCH, GLA_CHUNK, G_WIDTH), lambda c: (0, c, 0)),
        pl.BlockSpec((G_WIDTH, GLA_KEY_WIDTH), lambda c: (0, 0)),
        pl.BlockSpec((1, GLA_KEY_WIDTH), lambda c: (0, 0)),
        pl.BlockSpec(((2 + GLA_LEVELS) * GLA_CHUNK, 3 * GLA_CHUNK), lambda c: (0, 0)),
        pl.BlockSpec((1, GLA_DV), lambda c: (0, 0)),
    ]
    out_specs = [pl.BlockSpec((BATCH, GLA_CHUNK, GLA_WIDTH), lambda c: (0, c, 0))]
    out_shape = [jax.ShapeDtypeStruct((BATCH, SEQ, GLA_WIDTH), BF16)]
    operands = [proj_c, proj_c, proj_c, proj_g, wup_pad, b_gate, summat3, norm_g]
    body = _gla_kernel
    if next_layer is not None:
        assert GLA_STEPS * HOST_IN_ROWS + GLA_GATE_RANK == IN_PROJ_WIDTH
        in_specs += [
            pl.BlockSpec((pl.Element(1), pl.Element(HOST_IN_ROWS + GLA_GATE_RANK), pl.Element(D_MODEL)),
                         lambda c: (next_layer, c * HOST_IN_ROWS, 0)),
            pl.BlockSpec((1, HOST_OUT_ROWS, D_MODEL), lambda c: (next_layer, c, 0)),
        ]
        out_specs += [
            pl.BlockSpec((1, HOST_IN_ROWS, D_MODEL), lambda c: (0, c, 0)),
            pl.BlockSpec((1, G_WIDTH, D_MODEL), lambda c: (0, 0, 0)),
            pl.BlockSpec((1, HOST_OUT_ROWS, D_MODEL), lambda c: (0, c, 0)),
        ]
        out_shape += [jax.ShapeDtypeStruct((1, MAIN_WIDTH, D_MODEL), BF16),
                      jax.ShapeDtypeStruct((1, G_WIDTH, D_MODEL), BF16),
                      jax.ShapeDtypeStruct((1, D_MODEL, D_MODEL), BF16)]
        operands += [w_in_t, w_out]
        body = _gla_hosting_kernel
    return pl.pallas_call(
        body,
        grid=(GLA_STEPS,),
        in_specs=in_specs,
        out_specs=out_specs,
        out_shape=out_shape,
        scratch_shapes=[pltpu.VMEM((BATCH, GLA_DV, GLA_KEY_WIDTH), F32),
                        pltpu.VMEM((BATCH * GLA_KEY_WIDTH // LANES, 2 * GLA_CHUNK, GLA_CHUNK), BF16)],
        compiler_params=pltpu.CompilerParams(
            dimension_semantics=("arbitrary",), vmem_limit_bytes=VMEM_LIMIT),
        name="gla",
    )(*operands)


OUT_TM = 512
OUT_TN = 256


def _outproj_kernel(ya_ref, yb_ref, yc_ref, sz_ref, x_ref, gate_ref, g_ref, w_ref, o_ref,
                    yg_ref, ssq_ref, inv_ref, *acc_refs):
    i = pl.program_id(0)
    n_tiles = ROWS // OUT_TM

    @pl.when(i == 0)
    def _():
        for acc_ref in acc_refs:
            acc_ref[...] = jnp.zeros_like(acc_ref)
        ssq_ref[...] = jnp.zeros_like(ssq_ref)

    inv_ref[...] = lax.rsqrt(ssq_ref[...] * (1.0 / D_MODEL) + EPS)
    ssq_ref[...] = jnp.zeros_like(ssq_ref)
    out_gain = g_ref[...] * gate_ref[0]

    def finalise(blk):
        cols = slice(blk * OUT_TN, (blk + 1) * OUT_TN)
        o_ref[:, cols] = x_ref[:, cols] + acc_refs[blk][...] * inv_ref[:, 0:1] * out_gain[:, cols]

    def matmul(blk):
        r = jnp.dot(yg_ref[...], w_ref[0, :, blk * OUT_TN:(blk + 1) * OUT_TN], preferred_element_type=F32)
        acc_refs[blk][...] = r
        ssq_ref[...] += jnp.sum(r * r, axis=-1, keepdims=True)

    @pl.when(i < n_tiles)
    def _():
        col = 0
        for y_ref in (ya_ref, yb_ref, yc_ref):
            width = y_ref.shape[1]
            yg_ref[:, col:col + width] = y_ref[...] * sz_ref[:, col:col + width]
            col += width
        for blk in range(len(acc_refs)):
            finalise(blk)
            matmul(blk)

    @pl.when(i == n_tiles)
    def _():
        for blk in range(len(acc_refs)):
            finalise(blk)


def _outproj(layer, y_a, y_b, y_c, silu_z, x2, gate, g_post, w_out_bf):
    tiles_per_batch = SEQ // OUT_TM
    n_tiles = ROWS // OUT_TM
    mixed = lambda i: jnp.minimum(i, n_tiles - 1)
    done = lambda i: jnp.maximum(i - 1, 0)
    return pl.pallas_call(
        _outproj_kernel,
        grid=(n_tiles + 1,),
        in_specs=[
            pl.BlockSpec((OUT_TM, SWA_WIDTH), lambda i: (mixed(i), 0)),
            pl.BlockSpec((OUT_TM, SG_WIDTH), lambda i: (mixed(i), 0)),
            pl.BlockSpec((OUT_TM, GLA_WIDTH), lambda i: (mixed(i), 0)),
            pl.BlockSpec((OUT_TM, Z_WIDTH), lambda i: (mixed(i), 0)),
            pl.BlockSpec((OUT_TM, D_MODEL), lambda i: (done(i), 0)),
            pl.BlockSpec((1, 1, D_MODEL), lambda i: (done(i) // tiles_per_batch, 0, 0)),
            pl.BlockSpec((1, D_MODEL), lambda i: (0, 0)),
            pl.BlockSpec((1, D_MODEL, D_MODEL), lambda i: (layer, 0, 0), pipeline_mode=pl.Buffered(1)),
        ],
        out_specs=pl.BlockSpec((OUT_TM, D_MODEL), lambda i: (done(i), 0)),
        out_shape=jax.ShapeDtypeStruct((ROWS, D_MODEL), F32),
        scratch_shapes=(
            [pltpu.VMEM((OUT_TM, D_MODEL), BF16),
             pltpu.VMEM((OUT_TM, LANES), F32),
             pltpu.VMEM((OUT_TM, LANES), F32)]
            + [pltpu.VMEM((OUT_TM, OUT_TN), F32)] * (D_MODEL // OUT_TN)),
        compiler_params=pltpu.CompilerParams(
            dimension_semantics=("arbitrary",), vmem_limit_bytes=VMEM_LIMIT),
        name="gate_outproj_residual",
    )(y_a, y_b, y_c, silu_z, x2, gate, g_post, w_out_bf)


def kernel(x, c, positions, w_mod, b_mod, g_pre, g_post, w_in, w_out, swa_sinks,
           sg_w, sg_b, sg_ln_g, sg_ln_b, gla_w_gate_up, gla_b_gate, gla_norm_g):
    assert x.shape == (BATCH, SEQ, D_MODEL) and w_in.shape[0] == DEPTH

    c_pad = jnp.pad(c, ((0, SUBLANES - BATCH), (0, 0)))
    mod = _modulation(c_pad, w_mod, b_mod)[:, :BATCH, :]
    cos_t, sin_t = _rope_tables(positions)
    summat3 = jnp.asarray(np.tile(_gla_sum_matrix(), (1, 3)), dtype=BF16)

    w_in_t = jnp.swapaxes(w_in, 1, 2)
    w_main, w_gate = _prep_w_in(w_in_t)
    w_out_bf = _prep_w_out(w_out)

    x2 = x.reshape(ROWS, D_MODEL)
    for l in range(DEPTH):
        shift = mod[l, :, 0:D_MODEL].reshape(BATCH, 1, D_MODEL)
        scale = mod[l, :, D_MODEL:2 * D_MODEL].reshape(BATCH, 1, D_MODEL)
        gate = mod[l, :, 2 * D_MODEL:].reshape(BATCH, 1, D_MODEL)
        bias_tile = jnp.repeat(sg_b[l].T, SG_GROUP_DIM, axis=1)
        y_a, y_b, proj_c, silu_z, proj_g = _front(
            0, x2, scale, shift, g_pre[l].reshape(1, D_MODEL), w_main, w_gate, sg_w[l], bias_tile,
            sg_ln_g[l].reshape(1, SG_WIDTH), sg_ln_b[l].reshape(1, SG_WIDTH), swa_sinks[l], cos_t, sin_t)
        wup_pad = jnp.pad(gla_w_gate_up[l], ((0, G_WIDTH - GLA_GATE_RANK), (0, 0))).astype(BF16)
        gla_args = (proj_c.reshape(BATCH, SEQ, C_WIDTH), proj_g.reshape(BATCH, SEQ, G_WIDTH), wup_pad,
                    gla_b_gate[l].reshape(1, GLA_KEY_WIDTH), summat3, gla_norm_g[l].reshape(1, GLA_DV))
        this_w_out = w_out_bf
        if l + 1 < DEPTH:
            y_c, w_main, w_gate, w_out_bf = _gla(*gla_args, next_layer=l + 1, w_in_t=w_in_t, w_out=w_out)
        else:
            y_c, = _gla(*gla_args)

        x2 = _outproj(0, y_a, y_b, y_c.reshape(ROWS, GLA_WIDTH), silu_z, x2, gate,
                      g_post[l].reshape(1, D_MODEL), this_w_out)
    return x2.reshape(BATCH, SEQ, D_MODEL)
```

```python
import functools

import numpy as np
import jax
import jax.numpy as jnp
from jax import lax
from jax.experimental import pallas as pl
from jax.experimental.pallas import tpu as pltpu

F32 = jnp.float32
BF16 = jnp.bfloat16

D_MODEL = 2048
BATCH = 4
SEQ = 2048
DEPTH = 2
EPS = 1e-6
ROWS = BATCH * SEQ

SWA_HEAD_DIM = 64
SWA_HEADS = 16
SWA_KV_HEADS = 4
SWA_GROUP = SWA_HEADS // SWA_KV_HEADS
SWA_WIDTH = SWA_HEADS * SWA_HEAD_DIM
SWA_KV_WIDTH = SWA_KV_HEADS * SWA_HEAD_DIM
WINDOW = 128
ROT_DIM = 16
ROT_HALF = ROT_DIM // 2
ROPE_THETA = 500000.0

SG_WIDTH = 512
SG_GROUPS = 8
SG_GROUP_DIM = 64
SG_CHUNK = 128

GLA_HEADS = 4
GLA_WIDTH = 512
GLA_DV = 128
GLA_DK = 64
GLA_KEY_WIDTH = 256
GLA_GATE_RANK = 16
GLA_GATE_TAU = 16.0
GLA_CHUNK = 128
GLA_LEVELS = 7
GLA_SAFE_DECAY = 40.0

LANES = 128
SUBLANES = 8
A_WIDTH = SWA_WIDTH + 2 * SWA_KV_WIDTH
B_WIDTH = 2 * SG_WIDTH
C_WIDTH = 2 * GLA_KEY_WIDTH + GLA_WIDTH
Z_WIDTH = D_MODEL
G_WIDTH = LANES
MOD_WIDTH = 3 * D_MODEL

V7X_VMEM_BYTES = 64 * 1024 * 1024
VMEM_LIMIT = V7X_VMEM_BYTES - 8 * 1024 * 1024
FRONT_VMEM_LIMIT = V7X_VMEM_BYTES - 4 * 1024 * 1024


def _sigmoid(x):
    return 1.0 / (1.0 + jnp.exp(-x))


def _gelu_tanh(x):
    return 0.5 * x * (1.0 + jnp.tanh(float(np.sqrt(2.0 / np.pi)) * (x + 0.044715 * (x * x * x))))


def _dot_nt(a, b):
    return lax.dot_general(a, b, (((1,), (1,)), ((), ())), preferred_element_type=F32)


def _dot_tn(a, b):
    return lax.dot_general(a, b, (((0,), (0,)), ((), ())), preferred_element_type=F32)


MOD_TN = 1536
MOD_STEPS = DEPTH * (MOD_WIDTH // MOD_TN)
ROPE_TM = ROWS // MOD_STEPS


def _mod_rope_kernel(c_ref, w_ref, b_ref, pos_ref, invf_ref, o_ref, cos_ref, sin_ref):
    c = c_ref[...]
    s = (c * _sigmoid(c)).astype(BF16)
    o_ref[0] = jnp.dot(s, w_ref[0].astype(BF16), preferred_element_type=F32) + b_ref[0]

    ang = pos_ref[...].astype(F32) * invf_ref[...]
    lane = lax.broadcasted_iota(jnp.int32, (1, LANES), 1) % SWA_HEAD_DIM
    sn = jnp.sin(ang)
    cos_ref[...] = jnp.cos(ang)
    sin_ref[...] = jnp.where(lane < ROT_HALF, -sn, sn)


def _modulation_and_rope(c_pad, w_mod, b_mod, positions):
    half = np.arange(ROT_HALF, dtype=np.float32)
    inv_freq = (np.float32(ROPE_THETA) ** (-(half * np.float32(2.0 / ROT_DIM)))).astype(np.float32)
    lane = np.arange(LANES) % SWA_HEAD_DIM
    invf = np.where(lane < ROT_DIM, inv_freq[lane % ROT_HALF], 0.0).astype(np.float32)[None, :]
    per_layer = MOD_WIDTH // MOD_TN
    rows = lambda l, j: (l * per_layer + j, 0)
    return pl.pallas_call(
        _mod_rope_kernel,
        grid=(DEPTH, per_layer),
        in_specs=[
            pl.BlockSpec((SUBLANES, D_MODEL), lambda l, j: (0, 0)),
            pl.BlockSpec((1, D_MODEL, MOD_TN), lambda l, j: (l, 0, j)),
            pl.BlockSpec((1, 1, MOD_TN), lambda l, j: (l, 0, j)),
            pl.BlockSpec((ROPE_TM, 1), rows),
            pl.BlockSpec((1, LANES), lambda l, j: (0, 0)),
        ],
        out_specs=[pl.BlockSpec((1, SUBLANES, MOD_TN), lambda l, j: (l, 0, j)),
                   pl.BlockSpec((ROPE_TM, LANES), rows),
                   pl.BlockSpec((ROPE_TM, LANES), rows)],
        out_shape=[jax.ShapeDtypeStruct((DEPTH, SUBLANES, MOD_WIDTH), F32),
                   jax.ShapeDtypeStruct((ROWS, LANES), F32),
                   jax.ShapeDtypeStruct((ROWS, LANES), F32)],
        compiler_params=pltpu.CompilerParams(
            dimension_semantics=("arbitrary", "arbitrary"), vmem_limit_bytes=VMEM_LIMIT),
        name="adaln_mod_rope",
    )(c_pad, w_mod, b_mod.reshape(DEPTH, 1, MOD_WIDTH), positions.reshape(ROWS, 1), jnp.asarray(invf))


PREP_TN = 512
MAIN_WIDTH = A_WIDTH + B_WIDTH + C_WIDTH + Z_WIDTH
GATE_COL = A_WIDTH + B_WIDTH + C_WIDTH
IN_PROJ_WIDTH = GATE_COL + GLA_GATE_RANK + Z_WIDTH


def _prep_w_in_kernel(a_ref, main_ref, gate_ref):
    j = pl.program_id(1)
    first_z = GATE_COL // PREP_TN

    @pl.when(j < first_z)
    def _():
        main_ref[0] = a_ref[0, :PREP_TN, :].astype(BF16)

    @pl.when(j >= first_z)
    def _():
        main_ref[0] = a_ref[0, GLA_GATE_RANK:, :].astype(BF16)

    @pl.when(j == first_z)
    def _():
        gate_ref[0] = jnp.concatenate(
            [a_ref[0, :GLA_GATE_RANK, :], jnp.zeros((G_WIDTH - GLA_GATE_RANK, D_MODEL), F32)],
            axis=0).astype(BF16)


def _prep_w_in(w_in_t):
    assert GATE_COL % PREP_TN == 0 and MAIN_WIDTH + GLA_GATE_RANK == IN_PROJ_WIDTH
    return pl.pallas_call(
        _prep_w_in_kernel,
        grid=(1, MAIN_WIDTH // PREP_TN),
        in_specs=[
            pl.BlockSpec((pl.Element(1), pl.Element(PREP_TN + GLA_GATE_RANK), pl.Element(D_MODEL)),
                         lambda l, j: (l, j * PREP_TN, 0)),
        ],
        out_specs=[
            pl.BlockSpec((1, PREP_TN, D_MODEL), lambda l, j: (l, j, 0)),
            pl.BlockSpec((1, G_WIDTH, D_MODEL), lambda l, j: (l, 0, 0)),
        ],
        out_shape=[jax.ShapeDtypeStruct((1, MAIN_WIDTH, D_MODEL), BF16),
                   jax.ShapeDtypeStruct((1, G_WIDTH, D_MODEL), BF16)],
        compiler_params=pltpu.CompilerParams(
            dimension_semantics=("arbitrary", "arbitrary"), vmem_limit_bytes=VMEM_LIMIT),
        name="prep_w_in",
    )(w_in_t)


def _cast_kernel(x_ref, o_ref):
    o_ref[...] = x_ref[...].astype(o_ref.dtype)


def _prep_w_out(w_out):
    tn = 1024
    return pl.pallas_call(
        _cast_kernel,
        grid=(1, D_MODEL // tn),
        in_specs=[pl.BlockSpec((1, D_MODEL, tn), lambda l, j: (l, 0, j))],
        out_specs=pl.BlockSpec((1, D_MODEL, tn), lambda l, j: (l, 0, j)),
        out_shape=jax.ShapeDtypeStruct((1, D_MODEL, D_MODEL), BF16),
        compiler_params=pltpu.CompilerParams(
            dimension_semantics=("arbitrary", "arbitrary"), vmem_limit_bytes=VMEM_LIMIT),
        name="prep_w_out",
    )(w_out)


INPROJ_TM = 512
INPROJ_CHUNK = 256


def _spatial_gating_chunk(u, v, w_bf, bias, ln_g, ln_b):
    u = _gelu_tanh(u)
    v = _gelu_tanh(v)
    mu = jnp.mean(v, axis=-1, keepdims=True)
    vc = v - mu
    var = jnp.mean(vc * vc, axis=-1, keepdims=True)
    vn = (vc * lax.rsqrt(var + EPS) * ln_g + ln_b).astype(BF16)
    parts = [jnp.dot(w_bf[g], vn[:, g * SG_GROUP_DIM:(g + 1) * SG_GROUP_DIM], preferred_element_type=F32)
             for g in range(SG_GROUPS)]
    return u * (jnp.concatenate(parts, axis=1) + bias)


def _swa_pieces(tile_has_prev, a_ref, cos_ref, sin_ref, sinks_ref, ya_ref,
                kband_ref, vband_ref, qr_ref, ksel_ref, vdup_ref, s_ref, p_ref):
    heads_per_tile = LANES // SWA_HEAD_DIM
    n_blocks = INPROJ_TM // WINDOW
    lane = lax.broadcasted_iota(jnp.int32, (1, LANES), 1)
    first_half = (lane % SWA_HEAD_DIM) < ROT_HALF
    low_half = lane < SWA_HEAD_DIM
    qi = lax.broadcasted_iota(jnp.int32, (WINDOW, WINDOW), 0)
    kj = lax.broadcasted_iota(jnp.int32, (WINDOW, WINDOW), 1)
    from_prev = kj > qi

    def rope(t):
        partner = jnp.where(first_half, pltpu.roll(t, LANES - ROT_HALF, 1), pltpu.roll(t, ROT_HALF, 1))
        return t * cos_ref[...] + partner * sin_ref[...]

    steps = []

    def carry():
        kband_ref[0:WINDOW, :] = kband_ref[INPROJ_TM:INPROJ_TM + WINDOW, :]
        vband_ref[0:WINDOW, :] = vband_ref[INPROJ_TM:INPROJ_TM + WINDOW, :]
    steps.append(carry)

    def stage_k(t):
        cols = slice(t * LANES, (t + 1) * LANES)
        kband_ref[WINDOW:, cols] = rope(a_ref[:, SWA_WIDTH + t * LANES:SWA_WIDTH + (t + 1) * LANES])
        vband_ref[WINDOW:, cols] = a_ref[:, SWA_WIDTH + SWA_KV_WIDTH + t * LANES:
                                         SWA_WIDTH + SWA_KV_WIDTH + (t + 1) * LANES]
    for t in range(SWA_KV_WIDTH // LANES):
        steps.append(functools.partial(stage_k, t))

    def stage_q(t):
        cols = slice(t * LANES, (t + 1) * LANES)
        qr_ref[:, cols] = rope(a_ref[:, cols] * (SWA_HEAD_DIM ** -0.5)).astype(BF16)
    for t in range(SWA_WIDTH // LANES):
        steps.append(functools.partial(stage_q, t))

    def stage_kv(blk, t):
        rows = slice(blk * WINDOW, (blk + 2) * WINDOW)
        cols = slice(t * LANES, (t + 1) * LANES)
        kt = kband_ref[rows, cols]
        vt = vband_ref[rows, cols]
        kt_sw = pltpu.roll(kt, SWA_HEAD_DIM, 1)
        vt_sw = pltpu.roll(vt, SWA_HEAD_DIM, 1)
        for u, (k_lo, k_hi, v_lo, v_hi) in enumerate(((kt, kt_sw, vt, vt_sw), (kt_sw, kt, vt_sw, vt))):
            g = t * heads_per_tile + u
            ksel_ref[2 * g] = jnp.where(low_half, k_lo, 0.0).astype(BF16)
            ksel_ref[2 * g + 1] = jnp.where(low_half, 0.0, k_hi).astype(BF16)
            vdup_ref[g] = jnp.where(low_half, v_lo, v_hi).astype(BF16)

    def scores(blk, h):
        t, u = divmod(h, heads_per_tile)
        q = qr_ref[blk * WINDOW:(blk + 1) * WINDOW, t * LANES:(t + 1) * LANES]
        s2 = _dot_nt(q, ksel_ref[2 * (h // SWA_GROUP) + u])
        s_ref[h] = jnp.where(from_prev, s2[:, :WINDOW], s2[:, WINDOW:])

    def softmax(blk, h):
        s = s_ref[h]
        if blk == 0:
            s = jnp.where(kj <= qi + tile_has_prev * WINDOW, s, -jnp.inf)
        sink = sinks_ref[h]
        m = jnp.maximum(jnp.max(s, axis=-1, keepdims=True), sink)
        p = jnp.exp(s - m)
        denom = jnp.sum(p, axis=-1, keepdims=True) + jnp.exp(sink - m)
        p = (p * (1.0 / denom)).astype(BF16)
        zero = jnp.zeros_like(p)
        p_ref[h, :, :WINDOW] = jnp.where(from_prev, p, zero)
        p_ref[h, :, WINDOW:] = jnp.where(from_prev, zero, p)

    def values(blk, t):
        vg = vdup_ref[(t * heads_per_tile) // SWA_GROUP]
        outs = [jnp.dot(p_ref[t * heads_per_tile + u], vg, preferred_element_type=F32)
                for u in range(heads_per_tile)]
        ya_ref[blk * WINDOW:(blk + 1) * WINDOW, t * LANES:(t + 1) * LANES] = (
            jnp.where(low_half, outs[0], outs[1]).astype(ya_ref.dtype))

    for blk in range(n_blocks):
        for t in range(SWA_KV_WIDTH // LANES):
            steps.append(functools.partial(stage_kv, blk, t))
        for h in range(SWA_HEADS):
            steps.append(functools.partial(scores, blk, h))
        for h in range(SWA_HEADS):
            steps.append(functools.partial(softmax, blk, h))
        for t in range(SWA_WIDTH // LANES):
            steps.append(functools.partial(values, blk, t))
    return steps


def _interleave(main_steps, side_steps):
    done = 0
    for idx, step in enumerate(main_steps):
        step()
        upto = (idx + 1) * len(side_steps) // len(main_steps)
        for side in side_steps[done:upto]:
            side()
        done = upto


def _front_kernel(sinks_ref, x_ref, scale_ref, shift_ref, g_ref, w_ref, wg_ref,
                  sgw_ref, sgb_ref, lng_ref, lnb_ref, cos_ref, sin_ref,
                  ya_ref, yb_ref, oc_ref, oz_ref, og_ref,
                  h_ref, uv_ref, a_ref, kband_ref, vband_ref, qr_ref, ksel_ref, vdup_ref, s_ref, p_ref):
    i = pl.program_id(0)

    @pl.when(i == 0)
    def _():
        kband_ref[...] = jnp.zeros_like(kband_ref)
        vband_ref[...] = jnp.zeros_like(vband_ref)

    x = x_ref[...]
    ms = jnp.mean(x * x, axis=-1, keepdims=True)
    y = x * lax.rsqrt(ms + EPS) * g_ref[...]
    h_ref[...] = (y * (1.0 + scale_ref[0]) + shift_ref[0]).astype(BF16)

    def project(o_ref, col, c0, width, post=None):
        r = _dot_nt(h_ref[...], w_ref[0, col + c0:col + c0 + width, :])
        if post is not None:
            r = post(r)
        o_ref[:, c0:c0 + width] = r.astype(o_ref.dtype)

    def projection_steps(o_ref, col, post=None):
        return [functools.partial(project, o_ref, col, c0, INPROJ_CHUNK, post)
                for c0 in range(0, o_ref.shape[1], INPROJ_CHUNK)]

    for step in projection_steps(uv_ref, A_WIDTH):
        step()

    t = lax.broadcasted_iota(jnp.int32, (SG_CHUNK, SG_CHUNK), 0)
    s = lax.broadcasted_iota(jnp.int32, (SG_CHUNK, SG_CHUNK), 1)
    w_bf = [jnp.where(t >= s, sgw_ref[g], 0.0).astype(BF16) for g in range(SG_GROUPS)]

    def mixer_b(r0):
        rows = slice(r0, r0 + SG_CHUNK)
        yb_ref[rows, :] = _spatial_gating_chunk(
            uv_ref[rows, :SG_WIDTH], uv_ref[rows, SG_WIDTH:], w_bf, sgb_ref[...], lng_ref[...],
            lnb_ref[...]).astype(yb_ref.dtype)

    _interleave(projection_steps(a_ref, 0),
                [functools.partial(mixer_b, r0) for r0 in range(0, INPROJ_TM, SG_CHUNK)])

    tile_has_prev = ((i % (SEQ // INPROJ_TM)) != 0).astype(jnp.int32)
    swa_steps = _swa_pieces(tile_has_prev, a_ref, cos_ref, sin_ref, sinks_ref, ya_ref,
                            kband_ref, vband_ref, qr_ref, ksel_ref, vdup_ref, s_ref, p_ref)

    def gate_rank():
        og_ref[...] = _dot_nt(h_ref[...], wg_ref[0])

    silu = lambda z: z * (0.5 * jnp.tanh(0.5 * z) + 0.5)
    _interleave(projection_steps(oc_ref, A_WIDTH + B_WIDTH)
                + projection_steps(oz_ref, A_WIDTH + B_WIDTH + C_WIDTH, silu) + [gate_rank],
                swa_steps)


def _front(layer, x2, scale, shift, g_pre, w_main, w_gate, sg_w, sg_bias_tile, sg_ln_g, sg_ln_b,
           sinks, cos_t, sin_t):
    tiles_per_batch = SEQ // INPROJ_TM
    widths = (SWA_WIDTH, SG_WIDTH, C_WIDTH, Z_WIDTH, G_WIDTH)
    dtypes = (BF16, BF16, BF16, BF16, F32)
    const = pl.Buffered(1)
    return pl.pallas_call(
        _front_kernel,
        grid=(ROWS // INPROJ_TM,),
        in_specs=[
            pl.BlockSpec(memory_space=pltpu.SMEM),
            pl.BlockSpec((INPROJ_TM, D_MODEL), lambda i: (i, 0)),
            pl.BlockSpec((1, 1, D_MODEL), lambda i: (i // tiles_per_batch, 0, 0)),
            pl.BlockSpec((1, 1, D_MODEL), lambda i: (i // tiles_per_batch, 0, 0)),
            pl.BlockSpec((1, D_MODEL), lambda i: (0, 0)),
            pl.BlockSpec((1, MAIN_WIDTH, D_MODEL), lambda i: (layer, 0, 0), pipeline_mode=const),
            pl.BlockSpec((1, G_WIDTH, D_MODEL), lambda i: (layer, 0, 0), pipeline_mode=const),
            pl.BlockSpec((SG_GROUPS, SG_CHUNK, SG_CHUNK), lambda i: (0, 0, 0), pipeline_mode=const),
            pl.BlockSpec((SG_CHUNK, SG_WIDTH), lambda i: (0, 0), pipeline_mode=const),
            pl.BlockSpec((1, SG_WIDTH), lambda i: (0, 0)),
            pl.BlockSpec((1, SG_WIDTH), lambda i: (0, 0)),
            pl.BlockSpec((INPROJ_TM, LANES), lambda i: (i, 0)),
            pl.BlockSpec((INPROJ_TM, LANES), lambda i: (i, 0)),
        ],
        out_specs=[pl.BlockSpec((INPROJ_TM, w), lambda i: (i, 0)) for w in widths],
        out_shape=[jax.ShapeDtypeStruct((ROWS, w), dt) for w, dt in zip(widths, dtypes)],
        scratch_shapes=[
            pltpu.VMEM((INPROJ_TM, D_MODEL), BF16),
            pltpu.VMEM((INPROJ_TM, B_WIDTH), F32),
            pltpu.VMEM((INPROJ_TM, A_WIDTH), F32),
            pltpu.VMEM((INPROJ_TM + WINDOW, SWA_KV_WIDTH), F32),
            pltpu.VMEM((INPROJ_TM + WINDOW, SWA_KV_WIDTH), F32),
            pltpu.VMEM((INPROJ_TM, SWA_WIDTH), BF16),
            pltpu.VMEM((2 * SWA_KV_HEADS, 2 * WINDOW, LANES), BF16),
            pltpu.VMEM((SWA_KV_HEADS, 2 * WINDOW, LANES), BF16),
            pltpu.VMEM((SWA_HEADS, WINDOW, WINDOW), F32),
            pltpu.VMEM((SWA_HEADS, WINDOW, 2 * WINDOW), BF16),
        ],
        compiler_params=pltpu.CompilerParams(
            dimension_semantics=("arbitrary",), vmem_limit_bytes=FRONT_VMEM_LIMIT),
        name="front",
    )(sinks, x2, scale, shift, g_pre, w_main, w_gate, sg_w, sg_bias_tile, sg_ln_g, sg_ln_b, cos_t, sin_t)


def _gla_sum_matrix():
    c = GLA_CHUNK
    mat = np.zeros(((2 + GLA_LEVELS) * c, c), np.float32)
    for t in range(c):
        mat[t, :t + 1] = 1.0
        mat[c + t, t + 1:] = 1.0
        for k in range(GLA_LEVELS):
            m = 1 << k
            r = (t >> (k + 1) << (k + 1)) + m
            row = (2 + k) * c + t
            if (t >> k) & 1:
                mat[row, r + 1:t + 1] = 1.0
            else:
                mat[row, t + 1:r + 1] = 1.0
    return mat


def _split_heads_on_rows(x, low_half):
    zero = jnp.zeros_like(x)
    return jnp.concatenate([jnp.where(low_half, x, zero), jnp.where(low_half, zero, x)], axis=0)


def _gla_kernel(qkv_ref, cg_ref, wup_ref, bg_ref, summat_ref, ng_ref, o_ref, st_ref, sc_ref):
    c = GLA_CHUNK
    heads_per_tile = LANES // GLA_DK
    n_tiles = BATCH * GLA_KEY_WIDTH // LANES

    @pl.when(pl.program_id(0) == 0)
    def _():
        st_ref[...] = jnp.zeros_like(st_ref)

    cg = cg_ref[...].reshape(BATCH * c, G_WIDTH).astype(BF16)
    logits = jnp.dot(cg, wup_ref[...], preferred_element_type=F32) + bg_ref[...]
    log_alpha = (jnp.minimum(logits, 0.0) - jnp.log1p(jnp.exp(-jnp.abs(logits)))) * (1.0 / GLA_GATE_TAU)

    hi = log_alpha.astype(BF16)
    r1 = log_alpha - hi.astype(F32)
    mid = r1.astype(BF16)
    lo = (r1 - mid.astype(F32)).astype(BF16)
    pieces = jnp.concatenate(
        [jnp.concatenate([p[b * c:(b + 1) * c] for p in (hi, mid, lo)], axis=0) for b in range(BATCH)],
        axis=1)
    expo = jnp.dot(summat_ref[0:2 * c, :], pieces, preferred_element_type=F32)

    q = jnp.concatenate([qkv_ref[b, :, :GLA_KEY_WIDTH] for b in range(BATCH)],
                        axis=1).astype(F32) * (GLA_DK ** -0.5)
    k = jnp.concatenate([qkv_ref[b, :, GLA_KEY_WIDTH:2 * GLA_KEY_WIDTH] for b in range(BATCH)],
                        axis=1).astype(F32)
    b_cum = expo[0:c]
    q_inter = (q * jnp.exp(b_cum)).astype(BF16)
    k_state = (k * jnp.exp(expo[c:2 * c])).astype(BF16)
    b_last = b_cum[c - 1:c, :]
    decay = jnp.exp(b_last)

    t_i = lax.broadcasted_iota(jnp.int32, (heads_per_tile * c, c), 0) % c
    s_i = lax.broadcasted_iota(jnp.int32, (heads_per_tile * c, c), 1)
    low_half = lax.broadcasted_iota(jnp.int32, (1, LANES), 1) < GLA_DK

    safe = jnp.min(b_last) >= -GLA_SAFE_DECAY

    @pl.when(safe)
    def _():
        k_grown = (k * jnp.exp(-b_cum)).astype(BF16)
        for j in range(n_tiles):
            sl = slice(j * LANES, (j + 1) * LANES)
            a2 = _dot_nt(_split_heads_on_rows(q_inter[:, sl], low_half), k_grown[:, sl])
            sc_ref[j] = jnp.where(s_i <= t_i, a2, 0.0).astype(BF16)

    @pl.when(jnp.logical_not(safe))
    def _():
        lv_expo = jnp.dot(summat_ref[2 * c:, :], pieces, preferred_element_type=F32)
        q_lv, k_lv, masks = [q.astype(BF16)], [k.astype(BF16)], [t_i == s_i]
        for lv in range(GLA_LEVELS):
            f = jnp.exp(lv_expo[lv * c:(lv + 1) * c])
            q_lv.append((q * f).astype(BF16))
            k_lv.append((k * f).astype(BF16))
            masks.append(((t_i >> (lv + 1)) == (s_i >> (lv + 1)))
                         & (((t_i >> lv) & 1) == 1) & (((s_i >> lv) & 1) == 0))
        for j in range(n_tiles):
            sl = slice(j * LANES, (j + 1) * LANES)
            a2 = jnp.zeros((heads_per_tile * c, c), F32)
            for ql, kl, mask in zip(q_lv, k_lv, masks):
                a2 = a2 + jnp.where(mask, _dot_nt(_split_heads_on_rows(ql[:, sl], low_half), kl[:, sl]), 0.0)
            sc_ref[j] = a2.astype(BF16)

    scores = [sc_ref[j] for j in range(n_tiles)]
    tiles_per_seq = GLA_KEY_WIDTH // LANES
    for j in range(n_tiles):
        b, pair = divmod(j, tiles_per_seq)
        sl = slice(j * LANES, (j + 1) * LANES)
        st_cols = slice(pair * LANES, (pair + 1) * LANES)
        st = st_ref[b, :, st_cols]
        inter = _dot_nt(_split_heads_on_rows(q_inter[:, sl], low_half), st.astype(BF16))
        updates = []
        for u in range(heads_per_tile):
            h = pair * heads_per_tile + u
            vs = slice(h * GLA_DV, (h + 1) * GLA_DV)
            vh = qkv_ref[b, :, 2 * GLA_KEY_WIDTH + h * GLA_DV:2 * GLA_KEY_WIDTH + (h + 1) * GLA_DV]
            o = jnp.dot(scores[j][u * c:(u + 1) * c], vh, preferred_element_type=F32) + inter[u * c:(u + 1) * c]
            y = o * lax.rsqrt(jnp.mean(o * o, axis=-1, keepdims=True) + EPS) * ng_ref[...]
            o_ref[b, :, vs] = y.astype(o_ref.dtype)
            updates.append(_dot_tn(vh, k_state[:, sl]))
        st_ref[b, :, st_cols] = st * decay[:, sl] + jnp.where(low_half, updates[0], updates[1])


GLA_STEPS = SEQ // GLA_CHUNK
HOST_IN_ROWS = MAIN_WIDTH // GLA_STEPS
HOST_OUT_ROWS = D_MODEL // GLA_STEPS


def _gla_hosting_kernel(qkv_ref, cg_ref, wup_ref, bg_ref, summat_ref, ng_ref,
                        wa_ref, wo_ref, o_ref, main_ref, gate_ref, wout_ref, st_ref, sc_ref):
    _gla_kernel(qkv_ref, cg_ref, wup_ref, bg_ref, summat_ref, ng_ref, o_ref, st_ref, sc_ref)

    c = pl.program_id(0)
    mixed_blk, split = divmod(GATE_COL, HOST_IN_ROWS)

    @pl.when(c < mixed_blk)
    def _():
        main_ref[0] = wa_ref[0, :HOST_IN_ROWS, :].astype(BF16)

    @pl.when(c == mixed_blk)
    def _():
        main_ref[0] = jnp.concatenate(
            [wa_ref[0, :split, :], wa_ref[0, split + GLA_GATE_RANK:, :]], axis=0).astype(BF16)
        gate_ref[0] = jnp.concatenate(
            [wa_ref[0, split:split + GLA_GATE_RANK, :], jnp.zeros((G_WIDTH - GLA_GATE_RANK, D_MODEL), F32)],
            axis=0).astype(BF16)

    @pl.when(c > mixed_blk)
    def _():
        main_ref[0] = wa_ref[0, GLA_GATE_RANK:, :].astype(BF16)

    wout_ref[0] = wo_ref[0].astype(BF16)


def _gla(proj_c, proj_g, wup_pad, b_gate, summat3, norm_g, next_layer=None, w_in_t=None, w_out=None):
    in_specs = [
        pl.BlockSpec((BATCH, GLA_CHUNK, C_WIDTH), lambda c: (0, c, 0)),
        pl.BlockSpec((BATCH, GLA_CHUNK, G_WIDTH), lambda c: (0, c, 0)),
        pl.BlockSpec((G_WIDTH, GLA_KEY_WIDTH), lambda c: (0, 0)),
        pl.BlockSpec((1, GLA_KEY_WIDTH), lambda c: (0, 0)),
        pl.BlockSpec(((2 + GLA_LEVELS) * GLA_CHUNK, 3 * GLA_CHUNK), lambda c: (0, 0)),
        pl.BlockSpec((1, GLA_DV), lambda c: (0, 0)),
    ]
    out_specs = [pl.BlockSpec((BATCH, GLA_CHUNK, GLA_WIDTH), lambda c: (0, c, 0))]
    out_shape = [jax.ShapeDtypeStruct((BATCH, SEQ, GLA_WIDTH), BF16)]
    operands = [proj_c, proj_g, wup_pad, b_gate, summat3, norm_g]
    body = _gla_kernel
    if next_layer is not None:
        assert GLA_STEPS * HOST_IN_ROWS + GLA_GATE_RANK == IN_PROJ_WIDTH
        in_specs += [
            pl.BlockSpec((pl.Element(1), pl.Element(HOST_IN_ROWS + GLA_GATE_RANK), pl.Element(D_MODEL)),
                         lambda c: (next_layer, c * HOST_IN_ROWS, 0)),
            pl.BlockSpec((1, HOST_OUT_ROWS, D_MODEL), lambda c: (next_layer, c, 0)),
        ]
        out_specs += [
            pl.BlockSpec((1, HOST_IN_ROWS, D_MODEL), lambda c: (0, c, 0)),
            pl.BlockSpec((1, G_WIDTH, D_MODEL), lambda c: (0, 0, 0)),
            pl.BlockSpec((1, HOST_OUT_ROWS, D_MODEL), lambda c: (0, c, 0)),
        ]
        out_shape += [jax.ShapeDtypeStruct((1, MAIN_WIDTH, D_MODEL), BF16),
                      jax.ShapeDtypeStruct((1, G_WIDTH, D_MODEL), BF16),
                      jax.ShapeDtypeStruct((1, D_MODEL, D_MODEL), BF16)]
        operands += [w_in_t, w_out]
        body = _gla_hosting_kernel
    return pl.pallas_call(
        body,
        grid=(GLA_STEPS,),
        in_specs=in_specs,
        out_specs=out_specs,
        out_shape=out_shape,
        scratch_shapes=[pltpu.VMEM((BATCH, GLA_DV, GLA_KEY_WIDTH), F32),
                        pltpu.VMEM((BATCH * GLA_KEY_WIDTH // LANES, 2 * GLA_CHUNK, GLA_CHUNK), BF16)],
        compiler_params=pltpu.CompilerParams(
            dimension_semantics=("arbitrary",), vmem_limit_bytes=VMEM_LIMIT),
        name="gla",
    )(*operands)


OUT_TM = 512
OUT_TN = 256


def _outproj_kernel(ya_ref, yb_ref, yc_ref, sz_ref, x_ref, gate_ref, g_ref, w_ref, o_ref,
                    yg_ref, ssq_ref, inv_ref, *acc_refs):
    i = pl.program_id(0)
    n_tiles = ROWS // OUT_TM

    @pl.when(i == 0)
    def _():
        for acc_ref in acc_refs:
            acc_ref[...] = jnp.zeros_like(acc_ref)
        ssq_ref[...] = jnp.zeros_like(ssq_ref)

    inv_ref[...] = lax.rsqrt(ssq_ref[...] * (1.0 / D_MODEL) + EPS)
    ssq_ref[...] = jnp.zeros_like(ssq_ref)
    out_gain = g_ref[...] * gate_ref[0]

    def finalise(blk):
        cols = slice(blk * OUT_TN, (blk + 1) * OUT_TN)
        o_ref[:, cols] = x_ref[:, cols] + acc_refs[blk][...] * inv_ref[:, 0:1] * out_gain[:, cols]

    def matmul(blk):
        r = jnp.dot(yg_ref[...], w_ref[0, :, blk * OUT_TN:(blk + 1) * OUT_TN], preferred_element_type=F32)
        acc_refs[blk][...] = r
        ssq_ref[...] += jnp.sum(r * r, axis=-1, keepdims=True)

    @pl.when(i < n_tiles)
    def _():
        col = 0
        for y_ref in (ya_ref, yb_ref, yc_ref):
            width = y_ref.shape[1]
            yg_ref[:, col:col + width] = y_ref[...] * sz_ref[:, col:col + width]
            col += width
        for blk in range(len(acc_refs)):
            finalise(blk)
            matmul(blk)

    @pl.when(i == n_tiles)
    def _():
        for blk in range(len(acc_refs)):
            finalise(blk)


def _outproj(layer, y_a, y_b, y_c, silu_z, x2, gate, g_post, w_out_bf):
    tiles_per_batch = SEQ // OUT_TM
    n_tiles = ROWS // OUT_TM
    mixed = lambda i: jnp.minimum(i, n_tiles - 1)
    done = lambda i: jnp.maximum(i - 1, 0)
    return pl.pallas_call(
        _outproj_kernel,
        grid=(n_tiles + 1,),
        in_specs=[
            pl.BlockSpec((OUT_TM, SWA_WIDTH), lambda i: (mixed(i), 0)),
            pl.BlockSpec((OUT_TM, SG_WIDTH), lambda i: (mixed(i), 0)),
            pl.BlockSpec((OUT_TM, GLA_WIDTH), lambda i: (mixed(i), 0)),
            pl.BlockSpec((OUT_TM, Z_WIDTH), lambda i: (mixed(i), 0)),
            pl.BlockSpec((OUT_TM, D_MODEL), lambda i: (done(i), 0)),
            pl.BlockSpec((1, 1, D_MODEL), lambda i: (done(i) // tiles_per_batch, 0, 0)),
            pl.BlockSpec((1, D_MODEL), lambda i: (0, 0)),
            pl.BlockSpec((1, D_MODEL, D_MODEL), lambda i: (layer, 0, 0), pipeline_mode=pl.Buffered(1)),
        ],
        out_specs=pl.BlockSpec((OUT_TM, D_MODEL), lambda i: (done(i), 0)),
        out_shape=jax.ShapeDtypeStruct((ROWS, D_MODEL), F32),
        scratch_shapes=(
            [pltpu.VMEM((OUT_TM, D_MODEL), BF16),
             pltpu.VMEM((OUT_TM, LANES), F32),
             pltpu.VMEM((OUT_TM, LANES), F32)]
            + [pltpu.VMEM((OUT_TM, OUT_TN), F32)] * (D_MODEL // OUT_TN)),
        compiler_params=pltpu.CompilerParams(
            dimension_semantics=("arbitrary",), vmem_limit_bytes=VMEM_LIMIT),
        name="gate_outproj_residual",
    )(y_a, y_b, y_c, silu_z, x2, gate, g_post, w_out_bf)


def kernel(x, c, positions, w_mod, b_mod, g_pre, g_post, w_in, w_out, swa_sinks,
           sg_w, sg_b, sg_ln_g, sg_ln_b, gla_w_gate_up, gla_b_gate, gla_norm_g):
    assert x.shape == (BATCH, SEQ, D_MODEL) and w_in.shape[0] == DEPTH

    c_pad = jnp.pad(c, ((0, SUBLANES - BATCH), (0, 0)))
    mod, cos_t, sin_t = _modulation_and_rope(c_pad, w_mod, b_mod, positions)
    mod = mod[:, :BATCH, :]
    summat3 = jnp.asarray(np.tile(_gla_sum_matrix(), (1, 3)), dtype=BF16)

    w_in_t = jnp.swapaxes(w_in, 1, 2)
    w_main, w_gate = _prep_w_in(w_in_t)
    w_out_bf = _prep_w_out(w_out)

    x2 = x.reshape(ROWS, D_MODEL)
    for l in range(DEPTH):
        shift = mod[l, :, 0:D_MODEL].reshape(BATCH, 1, D_MODEL)
        scale = mod[l, :, D_MODEL:2 * D_MODEL].reshape(BATCH, 1, D_MODEL)
        gate = mod[l, :, 2 * D_MODEL:].reshape(BATCH, 1, D_MODEL)
        bias_tile = jnp.repeat(sg_b[l].T, SG_GROUP_DIM, axis=1)
        y_a, y_b, proj_c, silu_z, proj_g = _front(
            0, x2, scale, shift, g_pre[l].reshape(1, D_MODEL), w_main, w_gate, sg_w[l], bias_tile,
            sg_ln_g[l].reshape(1, SG_WIDTH), sg_ln_b[l].reshape(1, SG_WIDTH), swa_sinks[l], cos_t, sin_t)
        wup_pad = jnp.pad(gla_w_gate_up[l], ((0, G_WIDTH - GLA_GATE_RANK), (0, 0))).astype(BF16)
        gla_args = (proj_c.reshape(BATCH, SEQ, C_WIDTH), proj_g.reshape(BATCH, SEQ, G_WIDTH), wup_pad,
                    gla_b_gate[l].reshape(1, GLA_KEY_WIDTH), summat3, gla_norm_g[l].reshape(1, GLA_DV))
        this_w_out = w_out_bf
        if l + 1 < DEPTH:
            y_c, w_main, w_gate, w_out_bf = _gla(*gla_args, next_layer=l + 1, w_in_t=w_in_t, w_out=w_out)
        else:
            y_c, = _gla(*gla_args)

        x2 = _outproj(0, y_a, y_b, y_c.reshape(ROWS, GLA_WIDTH), silu_z, x2, gate,
                      g_post[l].reshape(1, D_MODEL), this_w_out)
    return x2.reshape(BATCH, SEQ, D_MODEL)
```

```python
import functools

import numpy as np
import jax
import jax.numpy as jnp
from jax import lax
from jax.experimental import pallas as pl
from jax.experimental.pallas import tpu as pltpu

F32 = jnp.float32
BF16 = jnp.bfloat16

D_MODEL = 2048
BATCH = 4
SEQ = 2048
DEPTH = 2
EPS = 1e-6
ROWS = BATCH * SEQ

SWA_HEAD_DIM = 64
SWA_HEADS = 16
SWA_KV_HEADS = 4
SWA_GROUP = SWA_HEADS // SWA_KV_HEADS
SWA_WIDTH = SWA_HEADS * SWA_HEAD_DIM
SWA_KV_WIDTH = SWA_KV_HEADS * SWA_HEAD_DIM
WINDOW = 128
ROT_DIM = 16
ROT_HALF = ROT_DIM // 2
ROPE_THETA = 500000.0

SG_WIDTH = 512
SG_GROUPS = 8
SG_GROUP_DIM = 64
SG_CHUNK = 128

GLA_HEADS = 4
GLA_WIDTH = 512
GLA_DV = 128
GLA_DK = 64
GLA_KEY_WIDTH = 256
GLA_GATE_RANK = 16
GLA_GATE_TAU = 16.0
GLA_CHUNK = 128
GLA_LEVELS = 7
GLA_SAFE_DECAY = 40.0

LANES = 128
SUBLANES = 8
A_WIDTH = SWA_WIDTH + 2 * SWA_KV_WIDTH
B_WIDTH = 2 * SG_WIDTH
C_WIDTH = 2 * GLA_KEY_WIDTH + GLA_WIDTH
Z_WIDTH = D_MODEL
G_WIDTH = LANES
MOD_WIDTH = 3 * D_MODEL

V7X_VMEM_BYTES = 64 * 1024 * 1024
VMEM_LIMIT = V7X_VMEM_BYTES - 8 * 1024 * 1024
FRONT_VMEM_LIMIT = V7X_VMEM_BYTES - 4 * 1024 * 1024


def _sigmoid(x):
    return 1.0 / (1.0 + jnp.exp(-x))


def _gelu_tanh(x):
    return 0.5 * x * (1.0 + jnp.tanh(float(np.sqrt(2.0 / np.pi)) * (x + 0.044715 * (x * x * x))))


def _dot_nt(a, b):
    return lax.dot_general(a, b, (((1,), (1,)), ((), ())), preferred_element_type=F32)


def _dot_tn(a, b):
    return lax.dot_general(a, b, (((0,), (0,)), ((), ())), preferred_element_type=F32)


MOD_TN = 1536
MOD_STEPS = DEPTH * (MOD_WIDTH // MOD_TN)
ROPE_TM = ROWS // MOD_STEPS


def _mod_rope_kernel(c_ref, w_ref, b_ref, pos_ref, invf_ref, o_ref, cos_ref, sin_ref):
    c = c_ref[...]
    s = (c * _sigmoid(c)).astype(BF16)
    o_ref[0] = jnp.dot(s, w_ref[0].astype(BF16), preferred_element_type=F32) + b_ref[0]

    ang = pos_ref[...].astype(F32) * invf_ref[...]
    lane = lax.broadcasted_iota(jnp.int32, (1, LANES), 1) % SWA_HEAD_DIM
    sn = jnp.sin(ang)
    cos_ref[...] = jnp.cos(ang)
    sin_ref[...] = jnp.where(lane < ROT_HALF, -sn, sn)


def _modulation_and_rope(c_pad, w_mod, b_mod, positions):
    half = np.arange(ROT_HALF, dtype=np.float32)
    inv_freq = (np.float32(ROPE_THETA) ** (-(half * np.float32(2.0 / ROT_DIM)))).astype(np.float32)
    lane = np.arange(LANES) % SWA_HEAD_DIM
    invf = np.where(lane < ROT_DIM, inv_freq[lane % ROT_HALF], 0.0).astype(np.float32)[None, :]
    per_layer = MOD_WIDTH // MOD_TN
    rows = lambda l, j: (l * per_layer + j, 0)
    return pl.pallas_call(
        _mod_rope_kernel,
        grid=(DEPTH, per_layer),
        in_specs=[
            pl.BlockSpec((SUBLANES, D_MODEL), lambda l, j: (0, 0)),
            pl.BlockSpec((1, D_MODEL, MOD_TN), lambda l, j: (l, 0, j)),
            pl.BlockSpec((1, 1, MOD_TN), lambda l, j: (l, 0, j)),
            pl.BlockSpec((ROPE_TM, 1), rows),
            pl.BlockSpec((1, LANES), lambda l, j: (0, 0)),
        ],
        out_specs=[pl.BlockSpec((1, SUBLANES, MOD_TN), lambda l, j: (l, 0, j)),
                   pl.BlockSpec((ROPE_TM, LANES), rows),
                   pl.BlockSpec((ROPE_TM, LANES), rows)],
        out_shape=[jax.ShapeDtypeStruct((DEPTH, SUBLANES, MOD_WIDTH), F32),
                   jax.ShapeDtypeStruct((ROWS, LANES), F32),
                   jax.ShapeDtypeStruct((ROWS, LANES), F32)],
        compiler_params=pltpu.CompilerParams(
            dimension_semantics=("arbitrary", "arbitrary"), vmem_limit_bytes=VMEM_LIMIT),
        name="adaln_mod_rope",
    )(c_pad, w_mod, b_mod.reshape(DEPTH, 1, MOD_WIDTH), positions.reshape(ROWS, 1), jnp.asarray(invf))


PREP_TN = 512
MAIN_WIDTH = A_WIDTH + B_WIDTH + C_WIDTH + Z_WIDTH
GATE_COL = A_WIDTH + B_WIDTH + C_WIDTH
IN_PROJ_WIDTH = GATE_COL + GLA_GATE_RANK + Z_WIDTH


def _prep_w_in_kernel(a_ref, main_ref, gate_ref):
    j = pl.program_id(1)
    first_z = GATE_COL // PREP_TN

    @pl.when(j < first_z)
    def _():
        main_ref[0] = a_ref[0, :PREP_TN, :].astype(BF16)

    @pl.when(j >= first_z)
    def _():
        main_ref[0] = a_ref[0, GLA_GATE_RANK:, :].astype(BF16)

    @pl.when(j == first_z)
    def _():
        gate_ref[0] = jnp.concatenate(
            [a_ref[0, :GLA_GATE_RANK, :], jnp.zeros((G_WIDTH - GLA_GATE_RANK, D_MODEL), F32)],
            axis=0).astype(BF16)


def _prep_w_in(w_in_t):
    assert GATE_COL % PREP_TN == 0 and MAIN_WIDTH + GLA_GATE_RANK == IN_PROJ_WIDTH
    return pl.pallas_call(
        _prep_w_in_kernel,
        grid=(1, MAIN_WIDTH // PREP_TN),
        in_specs=[
            pl.BlockSpec((pl.Element(1), pl.Element(PREP_TN + GLA_GATE_RANK), pl.Element(D_MODEL)),
                         lambda l, j: (l, j * PREP_TN, 0)),
        ],
        out_specs=[
            pl.BlockSpec((1, PREP_TN, D_MODEL), lambda l, j: (l, j, 0)),
            pl.BlockSpec((1, G_WIDTH, D_MODEL), lambda l, j: (l, 0, 0)),
        ],
        out_shape=[jax.ShapeDtypeStruct((1, MAIN_WIDTH, D_MODEL), BF16),
                   jax.ShapeDtypeStruct((1, G_WIDTH, D_MODEL), BF16)],
        compiler_params=pltpu.CompilerParams(
            dimension_semantics=("arbitrary", "arbitrary"), vmem_limit_bytes=VMEM_LIMIT),
        name="prep_w_in",
    )(w_in_t)


def _cast_kernel(x_ref, o_ref):
    o_ref[...] = x_ref[...].astype(o_ref.dtype)


def _prep_w_out(w_out):
    tn = 1024
    return pl.pallas_call(
        _cast_kernel,
        grid=(1, D_MODEL // tn),
        in_specs=[pl.BlockSpec((1, D_MODEL, tn), lambda l, j: (l, 0, j))],
        out_specs=pl.BlockSpec((1, D_MODEL, tn), lambda l, j: (l, 0, j)),
        out_shape=jax.ShapeDtypeStruct((1, D_MODEL, D_MODEL), BF16),
        compiler_params=pltpu.CompilerParams(
            dimension_semantics=("arbitrary", "arbitrary"), vmem_limit_bytes=VMEM_LIMIT),
        name="prep_w_out",
    )(w_out)


INPROJ_TM = 512
INPROJ_CHUNK = 256


def _spatial_gating_chunk(u, v, w_bf, bias, ln_g, ln_b):
    u = _gelu_tanh(u)
    v = _gelu_tanh(v)
    mu = jnp.mean(v, axis=-1, keepdims=True)
    vc = v - mu
    var = jnp.mean(vc * vc, axis=-1, keepdims=True)
    vn = (vc * lax.rsqrt(var + EPS) * ln_g + ln_b).astype(BF16)
    parts = [jnp.dot(w_bf[g], vn[:, g * SG_GROUP_DIM:(g + 1) * SG_GROUP_DIM], preferred_element_type=F32)
             for g in range(SG_GROUPS)]
    return u * (jnp.concatenate(parts, axis=1) + bias)


def _swa_pieces(tile_has_prev, a_ref, cos_ref, sin_ref, sinks_ref, ya_ref,
                kband_ref, vband_ref, qr_ref, ksel_ref, vdup_ref, s_ref, p_ref):
    heads_per_tile = LANES // SWA_HEAD_DIM
    n_blocks = INPROJ_TM // WINDOW
    lane = lax.broadcasted_iota(jnp.int32, (1, LANES), 1)
    first_half = (lane % SWA_HEAD_DIM) < ROT_HALF
    low_half = lane < SWA_HEAD_DIM
    qi = lax.broadcasted_iota(jnp.int32, (WINDOW, WINDOW), 0)
    kj = lax.broadcasted_iota(jnp.int32, (WINDOW, WINDOW), 1)
    from_prev = kj > qi

    def rope(t):
        partner = jnp.where(first_half, pltpu.roll(t, LANES - ROT_HALF, 1), pltpu.roll(t, ROT_HALF, 1))
        return t * cos_ref[...] + partner * sin_ref[...]

    steps = []

    def carry():
        kband_ref[0:WINDOW, :] = kband_ref[INPROJ_TM:INPROJ_TM + WINDOW, :]
        vband_ref[0:WINDOW, :] = vband_ref[INPROJ_TM:INPROJ_TM + WINDOW, :]
    steps.append(carry)

    def stage_k(t):
        cols = slice(t * LANES, (t + 1) * LANES)
        kband_ref[WINDOW:, cols] = rope(a_ref[:, SWA_WIDTH + t * LANES:SWA_WIDTH + (t + 1) * LANES])
        vband_ref[WINDOW:, cols] = a_ref[:, SWA_WIDTH + SWA_KV_WIDTH + t * LANES:
                                         SWA_WIDTH + SWA_KV_WIDTH + (t + 1) * LANES]
    for t in range(SWA_KV_WIDTH // LANES):
        steps.append(functools.partial(stage_k, t))

    def stage_q(t):
        cols = slice(t * LANES, (t + 1) * LANES)
        qr_ref[:, cols] = rope(a_ref[:, cols] * (SWA_HEAD_DIM ** -0.5)).astype(BF16)
    for t in range(SWA_WIDTH // LANES):
        steps.append(functools.partial(stage_q, t))

    def stage_kv(blk, t):
        rows = slice(blk * WINDOW, (blk + 2) * WINDOW)
        cols = slice(t * LANES, (t + 1) * LANES)
        kt = kband_ref[rows, cols]
        vt = vband_ref[rows, cols]
        kt_sw = pltpu.roll(kt, SWA_HEAD_DIM, 1)
        vt_sw = pltpu.roll(vt, SWA_HEAD_DIM, 1)
        for u, (k_lo, k_hi, v_lo, v_hi) in enumerate(((kt, kt_sw, vt, vt_sw), (kt_sw, kt, vt_sw, vt))):
            g = t * heads_per_tile + u
            ksel_ref[2 * g] = jnp.where(low_half, k_lo, 0.0).astype(BF16)
            ksel_ref[2 * g + 1] = jnp.where(low_half, 0.0, k_hi).astype(BF16)
            vdup_ref[g] = jnp.where(low_half, v_lo, v_hi).astype(BF16)

    def scores(blk, h):
        t, u = divmod(h, heads_per_tile)
        q = qr_ref[blk * WINDOW:(blk + 1) * WINDOW, t * LANES:(t + 1) * LANES]
        s2 = _dot_nt(q, ksel_ref[2 * (h // SWA_GROUP) + u])
        s_ref[h] = jnp.where(from_prev, s2[:, :WINDOW], s2[:, WINDOW:])

    def softmax(blk, h):
        s = s_ref[h]
        if blk == 0:
            s = jnp.where(kj <= qi + tile_has_prev * WINDOW, s, -jnp.inf)
        sink = sinks_ref[h]
        m = jnp.maximum(jnp.max(s, axis=-1, keepdims=True), sink)
        p = jnp.exp(s - m)
        denom = jnp.sum(p, axis=-1, keepdims=True) + jnp.exp(sink - m)
        p = (p * (1.0 / denom)).astype(BF16)
        zero = jnp.zeros_like(p)
        p_ref[h, :, :WINDOW] = jnp.where(from_prev, p, zero)
        p_ref[h, :, WINDOW:] = jnp.where(from_prev, zero, p)

    def values(blk, t):
        vg = vdup_ref[(t * heads_per_tile) // SWA_GROUP]
        outs = [jnp.dot(p_ref[t * heads_per_tile + u], vg, preferred_element_type=F32)
                for u in range(heads_per_tile)]
        ya_ref[blk * WINDOW:(blk + 1) * WINDOW, t * LANES:(t + 1) * LANES] = (
            jnp.where(low_half, outs[0], outs[1]).astype(ya_ref.dtype))

    for blk in range(n_blocks):
        for t in range(SWA_KV_WIDTH // LANES):
            steps.append(functools.partial(stage_kv, blk, t))
        for h in range(SWA_HEADS):
            steps.append(functools.partial(scores, blk, h))
        for h in range(SWA_HEADS):
            steps.append(functools.partial(softmax, blk, h))
        for t in range(SWA_WIDTH // LANES):
            steps.append(functools.partial(values, blk, t))
    return steps


def _interleave(main_steps, side_steps):
    done = 0
    for idx, step in enumerate(main_steps):
        step()
        upto = (idx + 1) * len(side_steps) // len(main_steps)
        for side in side_steps[done:upto]:
            side()
        done = upto


def _front_kernel(sinks_ref, x_ref, scale_ref, shift_ref, g_ref, w_ref, wg_ref,
                  sgw_ref, sgb_ref, lng_ref, lnb_ref, cos_ref, sin_ref,
                  ya_ref, yb_ref, oc_ref, oz_ref, og_ref,
                  h_ref, uv_ref, a_ref, kband_ref, vband_ref, qr_ref, ksel_ref, vdup_ref, s_ref, p_ref):
    i = pl.program_id(0)

    @pl.when(i == 0)
    def _():
        kband_ref[...] = jnp.zeros_like(kband_ref)
        vband_ref[...] = jnp.zeros_like(vband_ref)

    x = x_ref[...]
    ms = jnp.mean(x * x, axis=-1, keepdims=True)
    y = x * lax.rsqrt(ms + EPS) * g_ref[...]
    h_ref[...] = (y * (1.0 + scale_ref[0]) + shift_ref[0]).astype(BF16)

    def project(o_ref, col, c0, width, post=None):
        r = _dot_nt(h_ref[...], w_ref[0, col + c0:col + c0 + width, :])
        if post is not None:
            r = post(r)
        o_ref[:, c0:c0 + width] = r.astype(o_ref.dtype)

    def projection_steps(o_ref, col, post=None):
        return [functools.partial(project, o_ref, col, c0, INPROJ_CHUNK, post)
                for c0 in range(0, o_ref.shape[1], INPROJ_CHUNK)]

    for step in projection_steps(uv_ref, A_WIDTH):
        step()

    t = lax.broadcasted_iota(jnp.int32, (SG_CHUNK, SG_CHUNK), 0)
    s = lax.broadcasted_iota(jnp.int32, (SG_CHUNK, SG_CHUNK), 1)
    w_bf = [jnp.where(t >= s, sgw_ref[g], 0.0).astype(BF16) for g in range(SG_GROUPS)]

    def mixer_b(r0):
        rows = slice(r0, r0 + SG_CHUNK)
        yb_ref[rows, :] = _spatial_gating_chunk(
            uv_ref[rows, :SG_WIDTH], uv_ref[rows, SG_WIDTH:], w_bf, sgb_ref[...], lng_ref[...],
            lnb_ref[...]).astype(yb_ref.dtype)

    _interleave(projection_steps(a_ref, 0),
                [functools.partial(mixer_b, r0) for r0 in range(0, INPROJ_TM, SG_CHUNK)])

    tile_has_prev = ((i % (SEQ // INPROJ_TM)) != 0).astype(jnp.int32)
    swa_steps = _swa_pieces(tile_has_prev, a_ref, cos_ref, sin_ref, sinks_ref, ya_ref,
                            kband_ref, vband_ref, qr_ref, ksel_ref, vdup_ref, s_ref, p_ref)

    def gate_rank():
        og_ref[...] = _dot_nt(h_ref[...], wg_ref[0])

    silu = lambda z: z * (0.5 * jnp.tanh(0.5 * z) + 0.5)
    _interleave(projection_steps(oc_ref, A_WIDTH + B_WIDTH)
                + projection_steps(oz_ref, A_WIDTH + B_WIDTH + C_WIDTH, silu) + [gate_rank],
                swa_steps)


def _front(layer, x2, scale, shift, g_pre, w_main, w_gate, sg_w, sg_bias_tile, sg_ln_g, sg_ln_b,
           sinks, cos_t, sin_t):
    tiles_per_batch = SEQ // INPROJ_TM
    widths = (SWA_WIDTH, SG_WIDTH, C_WIDTH, Z_WIDTH, G_WIDTH)
    dtypes = (BF16, BF16, BF16, BF16, F32)
    const = pl.Buffered(1)
    return pl.pallas_call(
        _front_kernel,
        grid=(ROWS // INPROJ_TM,),
        in_specs=[
            pl.BlockSpec(memory_space=pltpu.SMEM),
            pl.BlockSpec((INPROJ_TM, D_MODEL), lambda i: (i, 0)),
            pl.BlockSpec((1, 1, D_MODEL), lambda i: (i // tiles_per_batch, 0, 0)),
            pl.BlockSpec((1, 1, D_MODEL), lambda i: (i // tiles_per_batch, 0, 0)),
            pl.BlockSpec((1, D_MODEL), lambda i: (0, 0)),
            pl.BlockSpec((1, MAIN_WIDTH, D_MODEL), lambda i: (layer, 0, 0), pipeline_mode=const),
            pl.BlockSpec((1, G_WIDTH, D_MODEL), lambda i: (layer, 0, 0), pipeline_mode=const),
            pl.BlockSpec((SG_GROUPS, SG_CHUNK, SG_CHUNK), lambda i: (0, 0, 0), pipeline_mode=const),
            pl.BlockSpec((SG_CHUNK, SG_WIDTH), lambda i: (0, 0), pipeline_mode=const),
            pl.BlockSpec((1, SG_WIDTH), lambda i: (0, 0)),
            pl.BlockSpec((1, SG_WIDTH), lambda i: (0, 0)),
            pl.BlockSpec((INPROJ_TM, LANES), lambda i: (i, 0)),
            pl.BlockSpec((INPROJ_TM, LANES), lambda i: (i, 0)),
        ],
        out_specs=[pl.BlockSpec((INPROJ_TM, w), lambda i: (i, 0)) for w in widths],
        out_shape=[jax.ShapeDtypeStruct((ROWS, w), dt) for w, dt in zip(widths, dtypes)],
        scratch_shapes=[
            pltpu.VMEM((INPROJ_TM, D_MODEL), BF16),
            pltpu.VMEM((INPROJ_TM, B_WIDTH), F32),
            pltpu.VMEM((INPROJ_TM, A_WIDTH), F32),
            pltpu.VMEM((INPROJ_TM + WINDOW, SWA_KV_WIDTH), F32),
            pltpu.VMEM((INPROJ_TM + WINDOW, SWA_KV_WIDTH), F32),
            pltpu.VMEM((INPROJ_TM, SWA_WIDTH), BF16),
            pltpu.VMEM((2 * SWA_KV_HEADS, 2 * WINDOW, LANES), BF16),
            pltpu.VMEM((SWA_KV_HEADS, 2 * WINDOW, LANES), BF16),
            pltpu.VMEM((SWA_HEADS, WINDOW, WINDOW), F32),
            pltpu.VMEM((SWA_HEADS, WINDOW, 2 * WINDOW), BF16),
        ],
        compiler_params=pltpu.CompilerParams(
            dimension_semantics=("arbitrary",), vmem_limit_bytes=FRONT_VMEM_LIMIT),
        name="front",
    )(sinks, x2, scale, shift, g_pre, w_main, w_gate, sg_w, sg_bias_tile, sg_ln_g, sg_ln_b, cos_t, sin_t)


def _gla_sum_matrix():
    c = GLA_CHUNK
    mat = np.zeros(((2 + GLA_LEVELS) * c, c), np.float32)
    for t in range(c):
        mat[t, :t + 1] = 1.0
        mat[c + t, t + 1:] = 1.0
        for k in range(GLA_LEVELS):
            m = 1 << k
            r = (t >> (k + 1) << (k + 1)) + m
            row = (2 + k) * c + t
            if (t >> k) & 1:
                mat[row, r + 1:t + 1] = 1.0
            else:
                mat[row, t + 1:r + 1] = 1.0
    return mat


def _split_heads_on_rows(x, low_half):
    zero = jnp.zeros_like(x)
    return jnp.concatenate([jnp.where(low_half, x, zero), jnp.where(low_half, zero, x)], axis=0)


def _gla_kernel(qkv_ref, cg_ref, wup_ref, bg_ref, summat_ref, ng_ref, o_ref, st_ref, sc_ref):
    c = GLA_CHUNK
    heads_per_tile = LANES // GLA_DK
    n_tiles = BATCH * GLA_KEY_WIDTH // LANES

    @pl.when(pl.program_id(0) == 0)
    def _():
        st_ref[...] = jnp.zeros_like(st_ref)

    cg = cg_ref[...].reshape(BATCH * c, G_WIDTH).astype(BF16)
    logits = jnp.dot(cg, wup_ref[...], preferred_element_type=F32) + bg_ref[...]
    log_alpha = (jnp.minimum(logits, 0.0) - jnp.log1p(jnp.exp(-jnp.abs(logits)))) * (1.0 / GLA_GATE_TAU)

    hi = log_alpha.astype(BF16)
    r1 = log_alpha - hi.astype(F32)
    mid = r1.astype(BF16)
    lo = (r1 - mid.astype(F32)).astype(BF16)
    pieces = jnp.concatenate(
        [jnp.concatenate([p[b * c:(b + 1) * c] for p in (hi, mid, lo)], axis=0) for b in range(BATCH)],
        axis=1)
    expo = jnp.dot(summat_ref[0:2 * c, :], pieces, preferred_element_type=F32)

    q = jnp.concatenate([qkv_ref[b, :, :GLA_KEY_WIDTH] for b in range(BATCH)],
                        axis=1).astype(F32) * (GLA_DK ** -0.5)
    k = jnp.concatenate([qkv_ref[b, :, GLA_KEY_WIDTH:2 * GLA_KEY_WIDTH] for b in range(BATCH)],
                        axis=1).astype(F32)
    b_cum = expo[0:c]
    q_inter = (q * jnp.exp(b_cum)).astype(BF16)
    k_state = (k * jnp.exp(expo[c:2 * c])).astype(BF16)
    b_last = b_cum[c - 1:c, :]
    decay = jnp.exp(b_last)

    t_i = lax.broadcasted_iota(jnp.int32, (heads_per_tile * c, c), 0) % c
    s_i = lax.broadcasted_iota(jnp.int32, (heads_per_tile * c, c), 1)
    low_half = lax.broadcasted_iota(jnp.int32, (1, LANES), 1) < GLA_DK

    safe = jnp.min(b_last) >= -GLA_SAFE_DECAY

    @pl.when(safe)
    def _():
        k_grown = (k * jnp.exp(-b_cum)).astype(BF16)
        for j in range(n_tiles):
            sl = slice(j * LANES, (j + 1) * LANES)
            a2 = _dot_nt(_split_heads_on_rows(q_inter[:, sl], low_half), k_grown[:, sl])
            sc_ref[j] = jnp.where(s_i <= t_i, a2, 0.0).astype(BF16)

    @pl.when(jnp.logical_not(safe))
    def _():
        lv_expo = jnp.dot(summat_ref[2 * c:, :], pieces, preferred_element_type=F32)
        q_lv, k_lv, masks = [q.astype(BF16)], [k.astype(BF16)], [t_i == s_i]
        for lv in range(GLA_LEVELS):
            f = jnp.exp(lv_expo[lv * c:(lv + 1) * c])
            q_lv.append((q * f).astype(BF16))
            k_lv.append((k * f).astype(BF16))
            masks.append(((t_i >> (lv + 1)) == (s_i >> (lv + 1)))
                         & (((t_i >> lv) & 1) == 1) & (((s_i >> lv) & 1) == 0))
        for j in range(n_tiles):
            sl = slice(j * LANES, (j + 1) * LANES)
            a2 = jnp.zeros((heads_per_tile * c, c), F32)
            for ql, kl, mask in zip(q_lv, k_lv, masks):
                a2 = a2 + jnp.where(mask, _dot_nt(_split_heads_on_rows(ql[:, sl], low_half), kl[:, sl]), 0.0)
            sc_ref[j] = a2.astype(BF16)

    scores = [sc_ref[j] for j in range(n_tiles)]
    tiles_per_seq = GLA_KEY_WIDTH // LANES
    for j in range(n_tiles):
        b, pair = divmod(j, tiles_per_seq)
        sl = slice(j * LANES, (j + 1) * LANES)
        st_cols = slice(pair * LANES, (pair + 1) * LANES)
        st = st_ref[b, :, st_cols]
        inter = _dot_nt(_split_heads_on_rows(q_inter[:, sl], low_half), st.astype(BF16))
        updates = []
        for u in range(heads_per_tile):
            h = pair * heads_per_tile + u
            vs = slice(h * GLA_DV, (h + 1) * GLA_DV)
            vh = qkv_ref[b, :, 2 * GLA_KEY_WIDTH + h * GLA_DV:2 * GLA_KEY_WIDTH + (h + 1) * GLA_DV]
            o = jnp.dot(scores[j][u * c:(u + 1) * c], vh, preferred_element_type=F32) + inter[u * c:(u + 1) * c]
            y = o * lax.rsqrt(jnp.mean(o * o, axis=-1, keepdims=True) + EPS) * ng_ref[...]
            o_ref[b, :, vs] = y.astype(o_ref.dtype)
            updates.append(_dot_tn(vh, k_state[:, sl]))
        st_ref[b, :, st_cols] = st * decay[:, sl] + jnp.where(low_half, updates[0], updates[1])


GLA_STEPS = SEQ // GLA_CHUNK
HOST_IN_ROWS = MAIN_WIDTH // GLA_STEPS


def _gla_hosting_kernel(qkv_ref, cg_ref, wup_ref, bg_ref, summat_ref, ng_ref,
                        wa_ref, o_ref, main_ref, gate_ref, st_ref, sc_ref):
    _gla_kernel(qkv_ref, cg_ref, wup_ref, bg_ref, summat_ref, ng_ref, o_ref, st_ref, sc_ref)

    c = pl.program_id(0)
    mixed_blk, split = divmod(GATE_COL, HOST_IN_ROWS)

    @pl.when(c < mixed_blk)
    def _():
        main_ref[0] = wa_ref[0, :HOST_IN_ROWS, :].astype(BF16)

    @pl.when(c == mixed_blk)
    def _():
        main_ref[0] = jnp.concatenate(
            [wa_ref[0, :split, :], wa_ref[0, split + GLA_GATE_RANK:, :]], axis=0).astype(BF16)
        gate_ref[0] = jnp.concatenate(
            [wa_ref[0, split:split + GLA_GATE_RANK, :], jnp.zeros((G_WIDTH - GLA_GATE_RANK, D_MODEL), F32)],
            axis=0).astype(BF16)

    @pl.when(c > mixed_blk)
    def _():
        main_ref[0] = wa_ref[0, GLA_GATE_RANK:, :].astype(BF16)


def _gla(proj_c, proj_g, wup_pad, b_gate, summat3, norm_g, next_layer=None, w_in_t=None):
    in_specs = [
        pl.BlockSpec((BATCH, GLA_CHUNK, C_WIDTH), lambda c: (0, c, 0)),
        pl.BlockSpec((BATCH, GLA_CHUNK, G_WIDTH), lambda c: (0, c, 0)),
        pl.BlockSpec((G_WIDTH, GLA_KEY_WIDTH), lambda c: (0, 0)),
        pl.BlockSpec((1, GLA_KEY_WIDTH), lambda c: (0, 0)),
        pl.BlockSpec(((2 + GLA_LEVELS) * GLA_CHUNK, 3 * GLA_CHUNK), lambda c: (0, 0)),
        pl.BlockSpec((1, GLA_DV), lambda c: (0, 0)),
    ]
    out_specs = [pl.BlockSpec((BATCH, GLA_CHUNK, GLA_WIDTH), lambda c: (0, c, 0))]
    out_shape = [jax.ShapeDtypeStruct((BATCH, SEQ, GLA_WIDTH), BF16)]
    operands = [proj_c, proj_g, wup_pad, b_gate, summat3, norm_g]
    body = _gla_kernel
    if next_layer is not None:
        assert GLA_STEPS * HOST_IN_ROWS + GLA_GATE_RANK == IN_PROJ_WIDTH
        in_specs += [
            pl.BlockSpec((pl.Element(1), pl.Element(HOST_IN_ROWS + GLA_GATE_RANK), pl.Element(D_MODEL)),
                         lambda c: (next_layer, c * HOST_IN_ROWS, 0)),
        ]
        out_specs += [
            pl.BlockSpec((1, HOST_IN_ROWS, D_MODEL), lambda c: (0, c, 0)),
            pl.BlockSpec((1, G_WIDTH, D_MODEL), lambda c: (0, 0, 0)),
        ]
        out_shape += [jax.ShapeDtypeStruct((1, MAIN_WIDTH, D_MODEL), BF16),
                      jax.ShapeDtypeStruct((1, G_WIDTH, D_MODEL), BF16)]
        operands += [w_in_t]
        body = _gla_hosting_kernel
    return pl.pallas_call(
        body,
        grid=(GLA_STEPS,),
        in_specs=in_specs,
        out_specs=out_specs,
        out_shape=out_shape,
        scratch_shapes=[pltpu.VMEM((BATCH, GLA_DV, GLA_KEY_WIDTH), F32),
                        pltpu.VMEM((BATCH * GLA_KEY_WIDTH // LANES, 2 * GLA_CHUNK, GLA_CHUNK), BF16)],
        compiler_params=pltpu.CompilerParams(
            dimension_semantics=("arbitrary",), vmem_limit_bytes=VMEM_LIMIT),
        name="gla",
    )(*operands)


OUT_TM = 512
OUT_TN = 256


def _outproj_kernel(ya_ref, yb_ref, yc_ref, sz_ref, x_ref, gate_ref, g_ref, w_ref, o_ref,
                    yg_ref, ssq_ref, inv_ref, *acc_refs):
    i = pl.program_id(0)
    n_tiles = ROWS // OUT_TM

    @pl.when(i == 0)
    def _():
        for acc_ref in acc_refs:
            acc_ref[...] = jnp.zeros_like(acc_ref)
        ssq_ref[...] = jnp.zeros_like(ssq_ref)

    inv_ref[...] = lax.rsqrt(ssq_ref[...] * (1.0 / D_MODEL) + EPS)
    ssq_ref[...] = jnp.zeros_like(ssq_ref)
    out_gain = g_ref[...] * gate_ref[0]

    def finalise(blk):
        cols = slice(blk * OUT_TN, (blk + 1) * OUT_TN)
        o_ref[:, cols] = x_ref[:, cols] + acc_refs[blk][...] * inv_ref[:, 0:1] * out_gain[:, cols]

    def matmul(blk):
        r = jnp.dot(yg_ref[...], w_ref[0, :, blk * OUT_TN:(blk + 1) * OUT_TN], preferred_element_type=F32)
        acc_refs[blk][...] = r
        ssq_ref[...] += jnp.sum(r * r, axis=-1, keepdims=True)

    @pl.when(i < n_tiles)
    def _():
        col = 0
        for y_ref in (ya_ref, yb_ref, yc_ref):
            width = y_ref.shape[1]
            yg_ref[:, col:col + width] = y_ref[...] * sz_ref[:, col:col + width]
            col += width
        for blk in range(len(acc_refs)):
            finalise(blk)
            matmul(blk)

    @pl.when(i == n_tiles)
    def _():
        for blk in range(len(acc_refs)):
            finalise(blk)


OUT_HOST_ROWS = D_MODEL // (ROWS // OUT_TM)


def _outproj_hosting_kernel(*refs):
    n_in = 8
    wo_ref, wout_ref = refs[n_in], refs[n_in + 2]
    _outproj_kernel(*refs[:n_in], refs[n_in + 1], *refs[n_in + 3:])
    wout_ref[0] = wo_ref[0].astype(BF16)


def _outproj(layer, y_a, y_b, y_c, silu_z, x2, gate, g_post, w_out_bf, next_layer=None, w_out=None):
    tiles_per_batch = SEQ // OUT_TM
    n_tiles = ROWS // OUT_TM
    mixed = lambda i: jnp.minimum(i, n_tiles - 1)
    done = lambda i: jnp.maximum(i - 1, 0)
    extra_in, extra_out, extra_shape, extra_args, body = [], [], [], [], _outproj_kernel
    if next_layer is not None:
        extra_in = [pl.BlockSpec((1, OUT_HOST_ROWS, D_MODEL), lambda i: (next_layer, mixed(i), 0))]
        extra_out = [pl.BlockSpec((1, OUT_HOST_ROWS, D_MODEL), lambda i: (0, mixed(i), 0))]
        extra_shape = [jax.ShapeDtypeStruct((1, D_MODEL, D_MODEL), BF16)]
        extra_args = [w_out]
        body = _outproj_hosting_kernel
    return pl.pallas_call(
        body,
        grid=(n_tiles + 1,),
        in_specs=[
            pl.BlockSpec((OUT_TM, SWA_WIDTH), lambda i: (mixed(i), 0)),
            pl.BlockSpec((OUT_TM, SG_WIDTH), lambda i: (mixed(i), 0)),
            pl.BlockSpec((OUT_TM, GLA_WIDTH), lambda i: (mixed(i), 0)),
            pl.BlockSpec((OUT_TM, Z_WIDTH), lambda i: (mixed(i), 0)),
            pl.BlockSpec((OUT_TM, D_MODEL), lambda i: (done(i), 0)),
            pl.BlockSpec((1, 1, D_MODEL), lambda i: (done(i) // tiles_per_batch, 0, 0)),
            pl.BlockSpec((1, D_MODEL), lambda i: (0, 0)),
            pl.BlockSpec((1, D_MODEL, D_MODEL), lambda i: (layer, 0, 0), pipeline_mode=pl.Buffered(1)),
        ] + extra_in,
        out_specs=[pl.BlockSpec((OUT_TM, D_MODEL), lambda i: (done(i), 0))] + extra_out,
        out_shape=[jax.ShapeDtypeStruct((ROWS, D_MODEL), F32)] + extra_shape,
        scratch_shapes=(
            [pltpu.VMEM((OUT_TM, D_MODEL), BF16),
             pltpu.VMEM((OUT_TM, LANES), F32),
             pltpu.VMEM((OUT_TM, LANES), F32)]
            + [pltpu.VMEM((OUT_TM, OUT_TN), F32)] * (D_MODEL // OUT_TN)),
        compiler_params=pltpu.CompilerParams(
            dimension_semantics=("arbitrary",), vmem_limit_bytes=VMEM_LIMIT),
        name="gate_outproj_residual",
    )(y_a, y_b, y_c, silu_z, x2, gate, g_post, w_out_bf, *extra_args)


def kernel(x, c, positions, w_mod, b_mod, g_pre, g_post, w_in, w_out, swa_sinks,
           sg_w, sg_b, sg_ln_g, sg_ln_b, gla_w_gate_up, gla_b_gate, gla_norm_g):
    assert x.shape == (BATCH, SEQ, D_MODEL) and w_in.shape[0] == DEPTH

    c_pad = jnp.pad(c, ((0, SUBLANES - BATCH), (0, 0)))
    mod, cos_t, sin_t = _modulation_and_rope(c_pad, w_mod, b_mod, positions)
    mod = mod[:, :BATCH, :]
    summat3 = jnp.asarray(np.tile(_gla_sum_matrix(), (1, 3)), dtype=BF16)

    w_in_t = jnp.swapaxes(w_in, 1, 2)
    w_main, w_gate = _prep_w_in(w_in_t)
    w_out_bf = _prep_w_out(w_out)

    x2 = x.reshape(ROWS, D_MODEL)
    for l in range(DEPTH):
        shift = mod[l, :, 0:D_MODEL].reshape(BATCH, 1, D_MODEL)
        scale = mod[l, :, D_MODEL:2 * D_MODEL].reshape(BATCH, 1, D_MODEL)
        gate = mod[l, :, 2 * D_MODEL:].reshape(BATCH, 1, D_MODEL)
        bias_tile = jnp.repeat(sg_b[l].T, SG_GROUP_DIM, axis=1)
        y_a, y_b, proj_c, silu_z, proj_g = _front(
            0, x2, scale, shift, g_pre[l].reshape(1, D_MODEL), w_main, w_gate, sg_w[l], bias_tile,
            sg_ln_g[l].reshape(1, SG_WIDTH), sg_ln_b[l].reshape(1, SG_WIDTH), swa_sinks[l], cos_t, sin_t)
        wup_pad = jnp.pad(gla_w_gate_up[l], ((0, G_WIDTH - GLA_GATE_RANK), (0, 0))).astype(BF16)
        gla_args = (proj_c.reshape(BATCH, SEQ, C_WIDTH), proj_g.reshape(BATCH, SEQ, G_WIDTH), wup_pad,
                    gla_b_gate[l].reshape(1, GLA_KEY_WIDTH), summat3, gla_norm_g[l].reshape(1, GLA_DV))
        out_args = (0, y_a, y_b, None, silu_z, x2, gate, g_post[l].reshape(1, D_MODEL), w_out_bf)
        if l + 1 < DEPTH:
            y_c, w_main, w_gate = _gla(*gla_args, next_layer=l + 1, w_in_t=w_in_t)
            x2, w_out_bf = _outproj(*out_args[:3], y_c.reshape(ROWS, GLA_WIDTH), *out_args[4:],
                                    next_layer=l + 1, w_out=w_out)
        else:
            y_c, = _gla(*gla_args)
            x2, = _outproj(*out_args[:3], y_c.reshape(ROWS, GLA_WIDTH), *out_args[4:])
    return x2.reshape(BATCH, SEQ, D_MODEL)
```
